```python
import math
import jax, jax.numpy as jnp
from jax import lax
import numpy as np

D_MODEL = 1024
BATCH = 4
SEQ = 8192
DEPTH = 1

CHUNK = 64
Q_BLOCK = 128
NORM_EPS = 1e-6

DA_HEADS = 4
DA_QK_DIM = 64
DA_V_DIM = 2 * DA_QK_DIM
DA_WIDTH = DA_HEADS * DA_V_DIM
ROPE_THETA = 500000.0
ROT_DIM = DA_QK_DIM // 4

ML_HEADS = 4
ML_DIM = 128
ML_WIDTH = ML_HEADS * ML_DIM
CONV_WIDTH = 4

MIX_WIDTH = DA_WIDTH + ML_WIDTH
D_FF = ((8 * D_MODEL // 3 + 255) // 256) * 256

DA_QK_COLS = DA_HEADS * 2 * DA_QK_DIM
IN_SIZES = (DA_QK_COLS, DA_QK_COLS, DA_WIDTH,
            ML_WIDTH, ML_WIDTH, ML_WIDTH, ML_WIDTH,
            ML_HEADS, ML_HEADS)
IN_WIDTH = sum(IN_SIZES)

kernel_name = "hybrid_diffattn_mlstm_block"


def rms_norm(x, g, eps=NORM_EPS):
    xf = x.astype(jnp.float32)
    y = xf * lax.rsqrt(jnp.mean(xf * xf, axis=-1, keepdims=True) + eps)
    return (y * g.astype(jnp.float32)).astype(x.dtype)


def rope_cos_sin(positions):
    inv_freq = ROPE_THETA ** (-jnp.arange(0, ROT_DIM, 2, dtype=jnp.float32) / ROT_DIM)
    ang = positions.astype(jnp.float32)[..., None] * inv_freq
    return jnp.cos(ang), jnp.sin(ang)


def apply_partial_rope(x, cos, sin):
    half = ROT_DIM // 2
    c = cos[:, :, None, None, :]
    s = sin[:, :, None, None, :]
    x1 = x[..., :half]
    x2 = x[..., half:ROT_DIM]
    return jnp.concatenate([x1 * c - x2 * s, x2 * c + x1 * s, x[..., ROT_DIM:]], axis=-1)


def diff_attention(q, k, v, lam):
    B, S = q.shape[0], q.shape[1]
    nb = S // Q_BLOCK
    scale = DA_QK_DIM ** -0.5
    qb = q.reshape(B, nb, Q_BLOCK, DA_HEADS, 2, DA_QK_DIM).transpose(1, 0, 2, 3, 4, 5)
    key_chunk = jnp.arange(S) // CHUNK

    def block(args):
        q_blk, blk = args
        s = jnp.einsum('bqhmd,bkhmd->bhmqk', q_blk, k) * scale
        q_chunk = (blk * Q_BLOCK + jnp.arange(Q_BLOCK)) // CHUNK
        mask = key_chunk[None, :] <= q_chunk[:, None]
        s = jnp.where(mask, s, -jnp.inf)
        p = jax.nn.softmax(s, axis=-1)
        a = p[:, :, 0] - lam * p[:, :, 1]
        return jnp.einsum('bhqk,bkhe->bqhe', a, v)

    o = lax.map(block, (qb, jnp.arange(nb)))
    return o.transpose(1, 0, 2, 3, 4).reshape(B, S, DA_HEADS, DA_V_DIM)


def mlstm_chunkwise(q, k, v, log_i, log_f):
    B, H, S, D = q.shape
    nc = S // CHUNK
    L = CHUNK

    def to_chunks(a):
        return jnp.moveaxis(a.reshape(a.shape[:2] + (nc, L) + a.shape[3:]), 2, 0)

    xs = (to_chunks(q), to_chunks(k), to_chunks(v), to_chunks(log_i), to_chunks(log_f))
    tril = jnp.tril(jnp.ones((L, L), dtype=bool))

    def step(carry, inp):
        C, n, m = carry
        qc, kc, vc, ic, fc = inp
        b = jnp.cumsum(fc, axis=-1)
        dmat = b[..., :, None] - b[..., None, :] + ic[..., None, :]
        dmat = jnp.where(tril, dmat, -jnp.inf)
        inter = b + m[..., None]
        m_t = jnp.maximum(inter, jnp.max(dmat, axis=-1))
        w_intra = jnp.exp(dmat - m_t[..., None])
        w_inter = jnp.exp(inter - m_t)
        sc = jnp.einsum('bhtd,bhsd->bhts', qc, kc) * w_intra
        num = jnp.einsum('bhts,bhsd->bhtd', sc, vc) \
            + w_inter[..., None] * jnp.einsum('bhtd,bhde->bhte', qc, C)
        nq = jnp.sum(sc, axis=-1) + w_inter * jnp.einsum('bhtd,bhd->bht', qc, n)
        h = num / jnp.maximum(jnp.abs(nq), jnp.exp(-m_t))[..., None]
        bL = b[..., -1]
        g = bL[..., None] - b + ic
        m_new = jnp.maximum(bL + m, jnp.max(g, axis=-1))
        decay = jnp.exp(bL + m - m_new)
        wk = jnp.exp(g - m_new[..., None])
        C_new = decay[..., None, None] * C + jnp.einsum('bhs,bhsd,bhse->bhde', wk, kc, vc)
        n_new = decay[..., None] * n + jnp.einsum('bhs,bhsd->bhd', wk, kc)
        return (C_new, n_new, m_new), h

    init = (jnp.zeros((B, H, D, D), jnp.float32), jnp.zeros((B, H, D), jnp.float32),
            jnp.zeros((B, H), jnp.float32))
    _, hs = lax.scan(step, init, xs)
    return jnp.moveaxis(hs, 0, 2).reshape(B, H, S, D)


def causal_depthwise_conv(x, w, b):
    C = x.shape[-1]
    y = lax.conv_general_dilated(
        x.astype(jnp.float32), w.astype(jnp.float32)[:, None, :],
        window_strides=(1,), padding=[(CONV_WIDTH - 1, 0)],
        dimension_numbers=('NWC', 'WIO', 'NWC'), feature_group_count=C)
    return y + b.astype(jnp.float32)


def setup_inputs(seed: int = 0) -> dict:
    key = jax.random.key(seed)
    ks = jax.random.split(key, 20)
    f32 = jnp.float32
    nrm = lambda k, shape, scale: jax.random.normal(k, shape, f32) * scale
    x = jax.random.normal(ks[0], (BATCH, SEQ, D_MODEL), f32)
    offset = jax.random.randint(ks[1], (BATCH, 1), 0, 64, dtype=jnp.int32) * CHUNK
    positions = offset + jnp.arange(SEQ, dtype=jnp.int32)[None, :]
    gate_b = jnp.stack([
        nrm(ks[2], (DEPTH, ML_HEADS), 0.1),
        jnp.broadcast_to(jnp.linspace(3.0, 6.0, ML_HEADS, dtype=f32), (DEPTH, ML_HEADS))
        + nrm(ks[3], (DEPTH, ML_HEADS), 0.01)], axis=1)
    return {
        "x": x,
        "positions": positions,
        "mix_norm_g": 1.0 + nrm(ks[4], (DEPTH, D_MODEL), 0.02),
        "w_in": nrm(ks[5], (DEPTH, D_MODEL, IN_WIDTH), D_MODEL ** -0.5),
        "da_lambda": nrm(ks[6], (DEPTH, 4, DA_QK_DIM), 0.1),
        "da_subln_g": 1.0 + nrm(ks[7], (DEPTH, DA_V_DIM), 0.02),
        "ml_conv_w": nrm(ks[8], (DEPTH, CONV_WIDTH, 2 * ML_WIDTH), CONV_WIDTH ** -0.5),
        "ml_conv_b": nrm(ks[9], (DEPTH, 2 * ML_WIDTH), 0.01),
        "ml_gate_b": gate_b,
        "ml_norm_g": 1.0 + nrm(ks[10], (DEPTH, ML_WIDTH), 0.02),
        "w_out": nrm(ks[11], (DEPTH, MIX_WIDTH, D_MODEL), MIX_WIDTH ** -0.5),
        "ffn_norm_g": 1.0 + nrm(ks[12], (DEPTH, D_MODEL), 0.02),
        "w_gate": nrm(ks[13], (DEPTH, D_MODEL, D_FF), D_MODEL ** -0.5),
        "w_up": nrm(ks[14], (DEPTH, D_MODEL, D_FF), D_MODEL ** -0.5),
        "w_down": nrm(ks[15], (DEPTH, D_FF, D_MODEL), D_FF ** -0.5),
        "final_norm_g": 1.0 + nrm(ks[16], (D_MODEL,), 0.02),
    }


def reference(x, positions, mix_norm_g, w_in, da_lambda, da_subln_g, ml_conv_w, ml_conv_b,
              ml_gate_b, ml_norm_g, w_out, ffn_norm_g, w_gate, w_up, w_down, final_norm_g):
    B, S, _ = x.shape
    cos, sin = rope_cos_sin(positions)
    split_idx = [sum(IN_SIZES[:i + 1]) for i in range(len(IN_SIZES) - 1)]
    for l in range(DEPTH):
        lam_init = 0.8 - 0.6 * math.exp(-0.3 * l)
        h = rms_norm(x, mix_norm_g[l])
        z = h @ w_in[l]
        (da_q, da_k, da_v, ml_q, ml_k, ml_v, ml_o, ml_i, ml_f) = jnp.split(z, split_idx, axis=-1)

        qa = apply_partial_rope(da_q.astype(jnp.float32).reshape(B, S, DA_HEADS, 2, DA_QK_DIM), cos, sin)
        ka = apply_partial_rope(da_k.astype(jnp.float32).reshape(B, S, DA_HEADS, 2, DA_QK_DIM), cos, sin)
        va = da_v.astype(jnp.float32).reshape(B, S, DA_HEADS, DA_V_DIM)
        lv = da_lambda[l].astype(jnp.float32)
        lam = jnp.exp(jnp.sum(lv[0] * lv[1])) - jnp.exp(jnp.sum(lv[2] * lv[3])) + lam_init
        oa = diff_attention(qa, ka, va, lam)
        oa = rms_norm(oa, da_subln_g[l]) * (1.0 - lam_init)
        attn_out = oa.reshape(B, S, DA_WIDTH).astype(x.dtype)

        qk = jax.nn.silu(causal_depthwise_conv(jnp.concatenate([ml_q, ml_k], axis=-1),
                                               ml_conv_w[l], ml_conv_b[l]))
        to_heads = lambda a: a.reshape(B, S, ML_HEADS, ML_DIM).transpose(0, 2, 1, 3)
        qm = to_heads(qk[..., :ML_WIDTH]) * (ML_DIM ** -0.5)
        km = to_heads(qk[..., ML_WIDTH:])
        vm = to_heads(ml_v.astype(jnp.float32))
        gb = ml_gate_b[l].astype(jnp.float32)
        log_i = (ml_i.astype(jnp.float32) + gb[0]).transpose(0, 2, 1)
        log_f = jax.nn.log_sigmoid(ml_f.astype(jnp.float32) + gb[1]).transpose(0, 2, 1)
        hm = mlstm_chunkwise(qm, km, vm, log_i, log_f)
        hm = rms_norm(hm.transpose(0, 2, 1, 3), ml_norm_g[l].reshape(ML_HEADS, ML_DIM))
        hm = hm * jax.nn.sigmoid(ml_o.astype(jnp.float32)).reshape(B, S, ML_HEADS, ML_DIM)
        mlstm_out = hm.reshape(B, S, ML_WIDTH).astype(x.dtype)

        x = x + jnp.concatenate([attn_out, mlstm_out], axis=-1) @ w_out[l]

        h2 = rms_norm(x, ffn_norm_g[l])
        x = x + (jax.nn.silu(h2 @ w_gate[l]) * (h2 @ w_up[l])) @ w_down[l]
    return rms_norm(x, final_norm_g)
```

```python
import functools
import math

import jax
import jax.numpy as jnp
from jax import lax
from jax.experimental import pallas as pl
from jax.experimental.pallas import tpu as pltpu

F32 = jnp.float32
BF16 = jnp.bfloat16

D_MODEL = 1024
CHUNK = 64
NORM_EPS = 1e-6
DA_HEADS = 4
DA_QK_DIM = 64
DA_V_DIM = 128
DA_WIDTH = DA_HEADS * DA_V_DIM
ROPE_THETA = 500000.0
ROT_DIM = DA_QK_DIM // 4
ROT_HALF = ROT_DIM // 2
ML_HEADS = 4
ML_DIM = 128
ML_WIDTH = ML_HEADS * ML_DIM
CONV_WIDTH = 4
D_FF = 2816
LAM_INIT = 0.8 - 0.6 * math.exp(-0.3 * 0)

OFF_DA_Q = 0
OFF_DA_K = 512
OFF_DA_V = 1024
OFF_ML = 1536
OFF_GATE = 3584

V7X_LANES = 128
V7X_SUBLANES = 8
V7X_VMEM_LIMIT_BYTES = 56 * 1024 * 1024

PROJ_ROWS = 512
ATTN_BLOCK = 512
ML_CHUNK = 256
FFN_ROWS = 512
CONV_HALO = V7X_SUBLANES

_NT = (((1,), (1,)), ((), ()))


def _compiler_params(n_axes):
    return pltpu.CompilerParams(
        dimension_semantics=("arbitrary",) * n_axes,
        vmem_limit_bytes=V7X_VMEM_LIMIT_BYTES,
    )


def _const_spec(shape):
    zeros = (0,) * len(shape)
    return pl.BlockSpec(shape, lambda *_: zeros, pipeline_mode=pl.Buffered(1))


def _rope_rows(zt, cos, sin):
    pieces = []
    for g in range(2):
        base = g * DA_QK_DIM
        x1 = zt[base:base + ROT_HALF]
        x2 = zt[base + ROT_HALF:base + ROT_DIM]
        pieces += [x1 * cos - x2 * sin, x2 * cos + x1 * sin, zt[base + ROT_DIM:base + DA_QK_DIM]]
    return jnp.concatenate(pieces, axis=0)


def _in_proj_kernel(tiles_per_seq,
                    x_ref, pos_ref, g_ref, w_t_ref, wg_t_ref, w_row_ref, invf_ref,
                    convw_ref, convb_ref, gb_col_ref, gb_row_ref,
                    q_t_ref, k_ref, v_t_ref, mq_ref, mk_ref, mv_ref, mo_ref, g_t_ref, g_col_ref,
                    halo_ref):
    rows = x_ref.shape[0]
    x = x_ref[...]
    ms = jnp.mean(x * x, axis=-1, keepdims=True)
    hb = ((x * lax.rsqrt(ms + NORM_EPS)) * g_ref[...]).astype(BF16)

    z_t = lax.dot_general(w_t_ref[...], hb, _NT, preferred_element_type=F32)
    ang = invf_ref[...] * pos_ref[0].astype(F32)
    cos = jnp.cos(ang)
    sin = jnp.sin(ang)
    for h in range(DA_HEADS):
        lo = h * 2 * DA_QK_DIM
        hi = lo + 2 * DA_QK_DIM
        q_rot = _rope_rows(z_t[OFF_DA_Q + lo:OFF_DA_Q + hi], cos, sin)
        q_t_ref[lo:hi, :] = (q_rot * (DA_QK_DIM ** -0.5)).astype(BF16)
        k_rot = _rope_rows(z_t[OFF_DA_K + lo:OFF_DA_K + hi], cos, sin)
        k_ref[:, lo:hi] = k_rot.T.astype(BF16)
    v_t_ref[...] = z_t[OFF_DA_V:OFF_DA_V + DA_WIDTH].astype(BF16)

    gz = lax.dot_general(wg_t_ref[...], hb, _NT, preferred_element_type=F32)
    a_t = gz[0:2 * ML_HEADS] + gb_col_ref[...]
    ls_t = jnp.minimum(a_t, 0.0) - jnp.log1p(jnp.exp(-jnp.abs(a_t)))
    row_id = lax.broadcasted_iota(jnp.int32, a_t.shape, 0)
    g_t_ref[...] = jnp.where(row_id < ML_HEADS, a_t, ls_t)

    mz = jnp.dot(hb, w_row_ref[...], preferred_element_type=F32)
    a_c = mz[:, 4 * ML_WIDTH:] + gb_row_ref[...]
    ls_c = jnp.minimum(a_c, 0.0) - jnp.log1p(jnp.exp(-jnp.abs(a_c)))
    lane_id = lax.broadcasted_iota(jnp.int32, a_c.shape, 1)
    g_col_ref[...] = jnp.where(lane_id < ML_HEADS, a_c, ls_c)

    mv_ref[...] = mz[:, 2 * ML_WIDTH:3 * ML_WIDTH].astype(BF16)
    mo_ref[...] = mz[:, 3 * ML_WIDTH:4 * ML_WIDTH].astype(BF16)

    first = (pl.program_id(0) % tiles_per_seq) == 0

    @pl.when(first)
    def _():
        halo_ref[0:CONV_HALO, :] = jnp.zeros((CONV_HALO, 2 * ML_WIDTH), F32)

    @pl.when(jnp.logical_not(first))
    def _():
        halo_ref[0:CONV_HALO, :] = halo_ref[rows:rows + CONV_HALO, :]

    halo_ref[CONV_HALO:CONV_HALO + rows, :] = mz[:, 0:2 * ML_WIDTH]
    conv = convb_ref[...]
    for j in range(CONV_WIDTH):
        shift = CONV_WIDTH - 1 - j
        conv = conv + convw_ref[j:j + 1, :] * halo_ref[CONV_HALO - shift:CONV_HALO - shift + rows, :]
    act = conv * (1.0 / (1.0 + jnp.exp(-conv)))
    mq_ref[...] = (act[:, 0:ML_WIDTH] * (ML_DIM ** -0.5)).astype(BF16)
    mk_ref[...] = act[:, ML_WIDTH:2 * ML_WIDTH].astype(BF16)


def _in_proj(x2d, pos3d, norm_g, w_t, wg_t, w_row, inv_freq, conv_w, conv_b, gb_col, gb_row, seq_len):
    tokens = x2d.shape[0]
    rows = PROJ_ROWS
    n_tiles = tokens // rows
    row_blk = lambda width: pl.BlockSpec((rows, width), lambda i: (i, 0))
    col_blk = lambda height: pl.BlockSpec((height, rows), lambda i: (0, i))
    out_shape = (
        jax.ShapeDtypeStruct((DA_WIDTH, tokens), BF16),
        jax.ShapeDtypeStruct((tokens, DA_WIDTH), BF16),
        jax.ShapeDtypeStruct((DA_WIDTH, tokens), BF16),
        jax.ShapeDtypeStruct((tokens, ML_WIDTH), BF16),
        jax.ShapeDtypeStruct((tokens, ML_WIDTH), BF16),
        jax.ShapeDtypeStruct((tokens, ML_WIDTH), BF16),
        jax.ShapeDtypeStruct((tokens, ML_WIDTH), BF16),
        jax.ShapeDtypeStruct((2 * ML_HEADS, tokens), F32),
        jax.ShapeDtypeStruct((tokens, V7X_LANES), F32),
    )
    return pl.pallas_call(
        functools.partial(_in_proj_kernel, seq_len // rows),
        grid=(n_tiles,),
        in_specs=[
            row_blk(D_MODEL),
            pl.BlockSpec((1, 1, rows), lambda i: (i, 0, 0)),
            _const_spec(norm_g.shape),
            _const_spec(w_t.shape),
            _const_spec(wg_t.shape),
            _const_spec(w_row.shape),
            _const_spec(inv_freq.shape),
            _const_spec(conv_w.shape),
            _const_spec(conv_b.shape),
            _const_spec(gb_col.shape),
            _const_spec(gb_row.shape),
        ],
        out_specs=(
            col_blk(DA_WIDTH), row_blk(DA_WIDTH), col_blk(DA_WIDTH),
            row_blk(ML_WIDTH), row_blk(ML_WIDTH), row_blk(ML_WIDTH), row_blk(ML_WIDTH),
            col_blk(2 * ML_HEADS), row_blk(V7X_LANES),
        ),
        out_shape=out_shape,
        scratch_shapes=[pltpu.VMEM((rows + CONV_HALO, 2 * ML_WIDTH), F32)],
        compiler_params=_compiler_params(1),
        name="in_proj",
    )(x2d, pos3d, norm_g, w_t, wg_t, w_row, inv_freq, conv_w, conv_b, gb_col, gb_row)


def _attn_kernel(lam_ref, q_t_ref, k_ref, v_t_ref, g_ref, o_ref, acc_ref):
    seq = k_ref.shape[0]
    blk = ATTN_BLOCK
    n_q = seq // blk

    lv = lam_ref[...]
    lam = (jnp.exp(jnp.sum(lv[0:1] * lv[1:2], axis=1, keepdims=True))
           - jnp.exp(jnp.sum(lv[2:3] * lv[3:4], axis=1, keepdims=True)) + LAM_INIT)

    key_chunk = lax.broadcasted_iota(jnp.int32, (blk, blk), 0) // CHUNK
    qry_chunk = lax.broadcasted_iota(jnp.int32, (blk, blk), 1) // CHUNK
    zero_half = jnp.zeros((DA_QK_DIM, blk), BF16)

    def q_tile(i, _):
        q_off = pl.multiple_of(i * blk, blk)
        q_t = q_t_ref[:, pl.ds(q_off, blk)]
        q_maps = (jnp.concatenate([q_t[0:DA_QK_DIM], zero_half], axis=0),
                  jnp.concatenate([zero_half, q_t[DA_QK_DIM:]], axis=0))
        acc_ref[...] = jnp.zeros(acc_ref.shape, F32)

        def kv_step(j, carry, masked):
            k_off = pl.multiple_of(j * blk, blk)
            k_j = k_ref[pl.ds(k_off, blk), :]
            v_t_j = v_t_ref[:, pl.ds(k_off, blk)]
            new = []
            for mi in range(2):
                m_old, l_old = carry[2 * mi], carry[2 * mi + 1]
                s = jnp.dot(k_j, q_maps[mi], preferred_element_type=F32)
                if masked:
                    s = jnp.where(key_chunk <= qry_chunk, s, -jnp.inf)
                m_new = jnp.maximum(m_old, jnp.max(s, axis=0, keepdims=True))
                p = jnp.exp(s - m_new)
                alpha = jnp.exp(m_old - m_new)
                l_new = alpha * l_old + jnp.sum(p, axis=0, keepdims=True)
                pv = jnp.dot(v_t_j, p.astype(BF16), preferred_element_type=F32)
                acc_ref[mi] = alpha * acc_ref[mi] + pv
                new += [m_new, l_new]
            return tuple(new)

        neg = jnp.full((1, blk), -jnp.inf, F32)
        zero = jnp.zeros((1, blk), F32)
        carry = lax.fori_loop(0, i, functools.partial(kv_step, masked=False), (neg, zero, neg, zero))
        _, l1, _, l2 = kv_step(i, carry, masked=True)

        o_t = acc_ref[0] / l1 - lam * (acc_ref[1] / l2)
        ms = jnp.mean(o_t * o_t, axis=0, keepdims=True)
        y_t = (o_t * lax.rsqrt(ms + NORM_EPS)) * g_ref[...] * (1.0 - LAM_INIT)
        o_ref[pl.ds(q_off, blk), :] = y_t.T.astype(BF16)
        return 0

    lax.fori_loop(0, n_q, q_tile, 0)


def _diff_attn(lam_params, q_t, k, v_t, subln_col, batch, seq_len):
    tokens = k.shape[0]
    return pl.pallas_call(
        _attn_kernel,
        grid=(batch, DA_HEADS),
        in_specs=[
            _const_spec(lam_params.shape),
            pl.BlockSpec((DA_V_DIM, seq_len), lambda b, h: (h, b)),
            pl.BlockSpec((seq_len, DA_V_DIM), lambda b, h: (b, h)),
            pl.BlockSpec((DA_V_DIM, seq_len), lambda b, h: (h, b)),
            _const_spec(subln_col.shape),
        ],
        out_specs=pl.BlockSpec((seq_len, DA_V_DIM), lambda b, h: (b, h)),
        out_shape=jax.ShapeDtypeStruct((tokens, DA_WIDTH), BF16),
        scratch_shapes=[pltpu.VMEM((2, DA_V_DIM, ATTN_BLOCK), F32)],
        compiler_params=_compiler_params(2),
        name="diff_attn",
    )(lam_params, q_t, k, v_t, subln_col)


def _mlstm_kernel(q_ref, k_ref, v_ref, o_ref, g_t_ref, g_col_ref, ng_ref, out_ref, c_ref, m_ref):
    chunk = q_ref.shape[0]

    @pl.when(pl.program_id(1) == 0)
    def _():
        c_ref[...] = jnp.zeros(c_ref.shape, F32)
        m_ref[...] = jnp.zeros(m_ref.shape, F32)

    t_id = lax.broadcasted_iota(jnp.int32, (chunk, chunk), 0)
    s_id = lax.broadcasted_iota(jnp.int32, (chunk, chunk), 1)
    causal = s_id <= t_id
    lower = causal.astype(F32)
    upper = (t_id <= s_id).astype(F32)

    g_t = g_t_ref[...]
    cum_row = jnp.dot(g_t, upper, preferred_element_type=F32, precision=lax.Precision.HIGHEST)
    cum_col = jnp.dot(lower, g_col_ref[...], preferred_element_type=F32,
                      precision=lax.Precision.HIGHEST)

    ones_col = (lax.broadcasted_iota(jnp.int32, (chunk, ML_DIM), 1) == 0).astype(BF16)

    for h in range(ML_HEADS):
        lanes = slice(h * ML_DIM, (h + 1) * ML_DIM)
        q = q_ref[:, lanes]
        k = k_ref[:, lanes]
        v_ext = jnp.concatenate([v_ref[:, lanes], ones_col], axis=1)
        b_row = cum_row[ML_HEADS + h:ML_HEADS + h + 1, :]
        i_row = g_t[h:h + 1, :]
        b_col = cum_col[:, ML_HEADS + h:ML_HEADS + h + 1]
        m_prev = m_ref[h]

        dmat = jnp.where(causal, (b_col - b_row) + i_row, -jnp.inf)
        inter = b_col + m_prev
        m_t = jnp.maximum(inter, jnp.max(dmat, axis=1, keepdims=True))
        w_intra = jnp.exp(dmat - m_t)
        w_inter = jnp.exp(inter - m_t)
        qk = lax.dot_general(q, k, _NT, preferred_element_type=F32)
        sc = (qk * w_intra).astype(BF16)
        c_ext = c_ref[h]
        num = (jnp.dot(sc, v_ext, preferred_element_type=F32)
               + w_inter * jnp.dot(q, c_ext.astype(BF16), preferred_element_type=F32))
        nq = num[:, ML_DIM:ML_DIM + 1]
        hid = num[:, 0:ML_DIM] / jnp.maximum(jnp.abs(nq), jnp.exp(-m_t))

        ms = jnp.mean(hid * hid, axis=-1, keepdims=True)
        hn = (hid * lax.rsqrt(ms + NORM_EPS)) * ng_ref[:, lanes]
        og = o_ref[:, lanes].astype(F32)
        out_ref[:, lanes] = (hn * (1.0 / (1.0 + jnp.exp(-og)))).astype(BF16)

        b_last = b_row[:, chunk - 1:chunk]
        g_row = (b_last - b_row) + i_row
        m_new = jnp.maximum(b_last + m_prev, jnp.max(g_row, axis=1, keepdims=True))
        decay = jnp.exp(b_last + m_prev - m_new)
        wk_row = jnp.exp(g_row - m_new)
        k_t_w = (k.astype(F32).T * wk_row).astype(BF16)
        c_ref[h] = decay * c_ext + jnp.dot(k_t_w, v_ext, preferred_element_type=F32)
        m_ref[h] = m_new


def _mlstm(mq, mk, mv, mo, g_t, g_col, norm_g, batch, seq_len):
    tokens = mq.shape[0]
    chunk = ML_CHUNK
    n_chunks = seq_len // chunk
    row_blk = lambda width: pl.BlockSpec((chunk, width), lambda b, c: (b * n_chunks + c, 0))
    return pl.pallas_call(
        _mlstm_kernel,
        grid=(batch, n_chunks),
        in_specs=[
            row_blk(ML_WIDTH), row_blk(ML_WIDTH), row_blk(ML_WIDTH), row_blk(ML_WIDTH),
            pl.BlockSpec((2 * ML_HEADS, chunk), lambda b, c: (0, b * n_chunks + c)),
            row_blk(V7X_LANES),
            _const_spec(norm_g.shape),
        ],
        out_specs=row_blk(ML_WIDTH),
        out_shape=jax.ShapeDtypeStruct((tokens, ML_WIDTH), BF16),
        scratch_shapes=[pltpu.VMEM((ML_HEADS, ML_DIM, 2 * ML_DIM), F32),
                        pltpu.VMEM((ML_HEADS, 1, 1), F32)],
        compiler_params=_compiler_params(2),
        name="mlstm",
    )(mq, mk, mv, mo, g_t, g_col, norm_g)


def _rms(x, g):
    ms = jnp.mean(x * x, axis=-1, keepdims=True)
    return (x * lax.rsqrt(ms + NORM_EPS)) * g


def _out_ffn_kernel(x_ref, attn_ref, ml_ref, w_out_ref, g_ffn_ref, w_gate_ref, w_up_ref, w_down_ref,
                    g_final_ref, out_ref):
    mix = jnp.concatenate([attn_ref[...], ml_ref[...]], axis=1)
    y = x_ref[...] + jnp.dot(mix, w_out_ref[...], preferred_element_type=F32)
    h2 = _rms(y, g_ffn_ref[...]).astype(BF16)
    gate = jnp.dot(h2, w_gate_ref[...], preferred_element_type=F32)
    up = jnp.dot(h2, w_up_ref[...], preferred_element_type=F32)
    act = ((gate * (1.0 / (1.0 + jnp.exp(-gate)))) * up).astype(BF16)
    y2 = y + jnp.dot(act, w_down_ref[...], preferred_element_type=F32)
    out_ref[...] = _rms(y2, g_final_ref[...])


def _out_ffn(x2d, attn, ml, w_out, g_ffn, w_gate, w_up, w_down, g_final):
    tokens = x2d.shape[0]
    rows = FFN_ROWS
    row_blk = lambda width: pl.BlockSpec((rows, width), lambda i: (i, 0))
    return pl.pallas_call(
        _out_ffn_kernel,
        grid=(tokens // rows,),
        in_specs=[
            row_blk(D_MODEL), row_blk(DA_WIDTH), row_blk(ML_WIDTH),
            _const_spec(w_out.shape), _const_spec(g_ffn.shape),
            _const_spec(w_gate.shape), _const_spec(w_up.shape), _const_spec(w_down.shape),
            _const_spec(g_final.shape),
        ],
        out_specs=row_blk(D_MODEL),
        out_shape=jax.ShapeDtypeStruct((tokens, D_MODEL), F32),
        compiler_params=_compiler_params(1),
        name="out_ffn",
    )(x2d, attn, ml, w_out, g_ffn, w_gate, w_up, w_down, g_final)


def kernel(x, positions, mix_norm_g, w_in, da_lambda, da_subln_g, ml_conv_w, ml_conv_b, ml_gate_b,
           ml_norm_g, w_out, ffn_norm_g, w_gate, w_up, w_down, final_norm_g):
    batch, seq_len, _ = x.shape
    tokens = batch * seq_len
    depth = w_in.shape[0]
    assert depth == 1, "one trunk layer"
    assert seq_len % PROJ_ROWS == 0 and seq_len % ATTN_BLOCK == 0 and seq_len % ML_CHUNK == 0
    assert ATTN_BLOCK % CHUNK == 0 and tokens % FFN_ROWS == 0

    x2d = x.reshape(tokens, D_MODEL)
    pos3d = positions.reshape(tokens // PROJ_ROWS, 1, PROJ_ROWS)

    w = w_in[0]
    w_t = w[:, OFF_DA_Q:OFF_ML].T.astype(BF16)
    gate_cols = w[:, OFF_GATE:OFF_GATE + 2 * ML_HEADS]
    wg_t = jnp.pad(gate_cols.T, ((0, 2 * ML_HEADS), (0, 0))).astype(BF16)
    w_row = jnp.concatenate(
        [w[:, OFF_ML:OFF_GATE], jnp.pad(gate_cols, ((0, 0), (0, V7X_LANES - 2 * ML_HEADS)))],
        axis=1).astype(BF16)
    inv_freq = (ROPE_THETA ** (-jnp.arange(0, ROT_DIM, 2, dtype=F32) / ROT_DIM)).reshape(ROT_HALF, 1)
    gb = ml_gate_b[0].astype(F32).reshape(2 * ML_HEADS)
    gb_col = gb.reshape(2 * ML_HEADS, 1)
    gb_row = jnp.pad(gb, (0, V7X_LANES - 2 * ML_HEADS)).reshape(1, V7X_LANES)

    q_t, k, v_t, mq, mk, mv, mo, g_t, g_col = _in_proj(
        x2d, pos3d, mix_norm_g[0].reshape(1, D_MODEL).astype(F32), w_t, wg_t, w_row, inv_freq,
        ml_conv_w[0].astype(F32), ml_conv_b[0].reshape(1, 2 * ML_WIDTH).astype(F32),
        gb_col, gb_row, seq_len)

    attn = _diff_attn(da_lambda[0].astype(F32), q_t, k, v_t,
                      da_subln_g[0].astype(F32).reshape(DA_V_DIM, 1), batch, seq_len)
    ml = _mlstm(mq, mk, mv, mo, g_t, g_col, ml_norm_g[0].astype(F32).reshape(1, ML_WIDTH),
                batch, seq_len)

    out = _out_ffn(x2d, attn, ml, w_out[0].astype(BF16),
                   ffn_norm_g[0].reshape(1, D_MODEL).astype(F32),
                   w_gate[0].astype(BF16), w_up[0].astype(BF16), w_down[0].astype(BF16),
                   final_norm_g.reshape(1, D_MODEL).astype(F32))
    return out.reshape(batch, seq_len, D_MODEL)
```

```python
import functools
import math

import jax
import jax.numpy as jnp
from jax import lax
from jax.experimental import pallas as pl
from jax.experimental.pallas import tpu as pltpu

F32 = jnp.float32
BF16 = jnp.bfloat16

D_MODEL = 1024
CHUNK = 64
NORM_EPS = 1e-6
DA_HEADS = 4
DA_QK_DIM = 64
DA_V_DIM = 128
DA_WIDTH = DA_HEADS * DA_V_DIM
ROPE_THETA = 500000.0
ROT_DIM = DA_QK_DIM // 4
ROT_HALF = ROT_DIM // 2
ML_HEADS = 4
ML_DIM = 128
ML_WIDTH = ML_HEADS * ML_DIM
CONV_WIDTH = 4
D_FF = 2816
LAM_INIT = 0.8 - 0.6 * math.exp(-0.3 * 0)
Q_SCALE = DA_QK_DIM ** -0.5 * math.log2(math.e)

OFF_DA_Q = 0
OFF_DA_K = 512
OFF_DA_V = 1024
OFF_ML = 1536
OFF_GATE = 3584

V7X_LANES = 128
V7X_SUBLANES = 8
V7X_VMEM_LIMIT_BYTES = 56 * 1024 * 1024

PROJ_ROWS = 512
ATTN_BLOCK = 512
ATTN_HALF = ATTN_BLOCK // 2
ML_CHUNK = 256
FFN_ROWS = 512
CONV_HALO = V7X_SUBLANES

_NT = (((1,), (1,)), ((), ()))


def _compiler_params(n_axes):
    return pltpu.CompilerParams(
        dimension_semantics=("arbitrary",) * n_axes,
        vmem_limit_bytes=V7X_VMEM_LIMIT_BYTES,
    )


def _const_spec(shape):
    zeros = (0,) * len(shape)
    return pl.BlockSpec(shape, lambda *_: zeros, pipeline_mode=pl.Buffered(1))


def _rope_rows(zt, cos, sin):
    pieces = []
    for g in range(2):
        base = g * DA_QK_DIM
        x1 = zt[base:base + ROT_HALF]
        x2 = zt[base + ROT_HALF:base + ROT_DIM]
        pieces += [x1 * cos - x2 * sin, x2 * cos + x1 * sin, zt[base + ROT_DIM:base + DA_QK_DIM]]
    return jnp.concatenate(pieces, axis=0)


def _in_proj_kernel(tiles_per_seq,
                    x_ref, pos_ref, g_ref, w_t_ref, wg_t_ref, w_row_ref, invf_ref,
                    convw_ref, convb_ref, gb_col_ref, gb_row_ref,
                    q_t_ref, k_ref, v_t_ref, mq_ref, mk_ref, mv_ref, mo_ref, g_t_ref, g_col_ref,
                    halo_ref):
    rows = x_ref.shape[0]
    x = x_ref[...]
    ms = jnp.mean(x * x, axis=-1, keepdims=True)
    hb = ((x * lax.rsqrt(ms + NORM_EPS)) * g_ref[...]).astype(BF16)

    z_t = lax.dot_general(w_t_ref[...], hb, _NT, preferred_element_type=F32)
    ang = invf_ref[...] * pos_ref[0].astype(F32)
    cos = jnp.cos(ang)
    sin = jnp.sin(ang)
    for h in range(DA_HEADS):
        lo = h * 2 * DA_QK_DIM
        hi = lo + 2 * DA_QK_DIM
        q_rot = _rope_rows(z_t[OFF_DA_Q + lo:OFF_DA_Q + hi], cos, sin)
        q_t_ref[lo:hi, :] = (q_rot * Q_SCALE).astype(BF16)
        k_rot = _rope_rows(z_t[OFF_DA_K + lo:OFF_DA_K + hi], cos, sin)
        k_ref[:, lo:hi] = k_rot.T.astype(BF16)
    v_t_ref[...] = z_t[OFF_DA_V:OFF_DA_V + DA_WIDTH].astype(BF16)

    gz = lax.dot_general(wg_t_ref[...], hb, _NT, preferred_element_type=F32)
    a_t = gz[0:2 * ML_HEADS] + gb_col_ref[...]
    ls_t = jnp.minimum(a_t, 0.0) - jnp.log1p(jnp.exp(-jnp.abs(a_t)))
    row_id = lax.broadcasted_iota(jnp.int32, a_t.shape, 0)
    g_t_ref[...] = jnp.where(row_id < ML_HEADS, a_t, ls_t)

    mz = jnp.dot(hb, w_row_ref[...], preferred_element_type=F32)
    a_c = mz[:, 4 * ML_WIDTH:] + gb_row_ref[...]
    ls_c = jnp.minimum(a_c, 0.0) - jnp.log1p(jnp.exp(-jnp.abs(a_c)))
    lane_id = lax.broadcasted_iota(jnp.int32, a_c.shape, 1)
    g_col_ref[...] = jnp.where(lane_id < ML_HEADS, a_c, ls_c)

    mv_ref[...] = mz[:, 2 * ML_WIDTH:3 * ML_WIDTH].astype(BF16)
    mo_ref[...] = mz[:, 3 * ML_WIDTH:4 * ML_WIDTH].astype(BF16)

    first = (pl.program_id(0) % tiles_per_seq) == 0

    @pl.when(first)
    def _():
        halo_ref[0:CONV_HALO, :] = jnp.zeros((CONV_HALO, 2 * ML_WIDTH), F32)

    @pl.when(jnp.logical_not(first))
    def _():
        halo_ref[0:CONV_HALO, :] = halo_ref[rows:rows + CONV_HALO, :]

    halo_ref[CONV_HALO:CONV_HALO + rows, :] = mz[:, 0:2 * ML_WIDTH]
    conv = convb_ref[...]
    for j in range(CONV_WIDTH):
        shift = CONV_WIDTH - 1 - j
        conv = conv + convw_ref[j:j + 1, :] * halo_ref[CONV_HALO - shift:CONV_HALO - shift + rows, :]
    act = conv * (1.0 / (1.0 + jnp.exp(-conv)))
    mq_ref[...] = (act[:, 0:ML_WIDTH] * (ML_DIM ** -0.5)).astype(BF16)
    mk_ref[...] = act[:, ML_WIDTH:2 * ML_WIDTH].astype(BF16)


def _in_proj(x2d, pos3d, norm_g, w_t, wg_t, w_row, inv_freq, conv_w, conv_b, gb_col, gb_row, seq_len):
    tokens = x2d.shape[0]
    rows = PROJ_ROWS
    n_tiles = tokens // rows
    row_blk = lambda width: pl.BlockSpec((rows, width), lambda i: (i, 0))
    col_blk = lambda height: pl.BlockSpec((height, rows), lambda i: (0, i))
    out_shape = (
        jax.ShapeDtypeStruct((DA_WIDTH, tokens), BF16),
        jax.ShapeDtypeStruct((tokens, DA_WIDTH), BF16),
        jax.ShapeDtypeStruct((DA_WIDTH, tokens), BF16),
        jax.ShapeDtypeStruct((tokens, ML_WIDTH), BF16),
        jax.ShapeDtypeStruct((tokens, ML_WIDTH), BF16),
        jax.ShapeDtypeStruct((tokens, ML_WIDTH), BF16),
        jax.ShapeDtypeStruct((tokens, ML_WIDTH), BF16),
        jax.ShapeDtypeStruct((2 * ML_HEADS, tokens), F32),
        jax.ShapeDtypeStruct((tokens, V7X_LANES), F32),
    )
    return pl.pallas_call(
        functools.partial(_in_proj_kernel, seq_len // rows),
        grid=(n_tiles,),
        in_specs=[
            row_blk(D_MODEL),
            pl.BlockSpec((1, 1, rows), lambda i: (i, 0, 0)),
            _const_spec(norm_g.shape),
            _const_spec(w_t.shape),
            _const_spec(wg_t.shape),
            _const_spec(w_row.shape),
            _const_spec(inv_freq.shape),
            _const_spec(conv_w.shape),
            _const_spec(conv_b.shape),
            _const_spec(gb_col.shape),
            _const_spec(gb_row.shape),
        ],
        out_specs=(
            col_blk(DA_WIDTH), row_blk(DA_WIDTH), col_blk(DA_WIDTH),
            row_blk(ML_WIDTH), row_blk(ML_WIDTH), row_blk(ML_WIDTH), row_blk(ML_WIDTH),
            col_blk(2 * ML_HEADS), row_blk(V7X_LANES),
        ),
        out_shape=out_shape,
        scratch_shapes=[pltpu.VMEM((rows + CONV_HALO, 2 * ML_WIDTH), F32)],
        compiler_params=_compiler_params(1),
        name="in_proj",
    )(x2d, pos3d, norm_g, w_t, wg_t, w_row, inv_freq, conv_w, conv_b, gb_col, gb_row)


def _attn_kernel(lam_ref, q_t_ref, k_ref, v_t_ref, g_ref, o_ref, qm_ref, s_ref, p_ref, acc_ref):
    seq = k_ref.shape[0]
    blk = ATTN_BLOCK
    half = ATTN_HALF
    n_q = seq // blk

    lv = lam_ref[...]
    lam = (jnp.exp(jnp.sum(lv[0:1] * lv[1:2], axis=1, keepdims=True))
           - jnp.exp(jnp.sum(lv[2:3] * lv[3:4], axis=1, keepdims=True)) + LAM_INIT)

    key_chunk = lax.broadcasted_iota(jnp.int32, (blk, half), 0) // CHUNK
    qry_chunk = [(lax.broadcasted_iota(jnp.int32, (blk, half), 1) + hf * half) // CHUNK
                 for hf in range(2)]
    zero_half = jnp.zeros((DA_QK_DIM, blk), BF16)

    def score(hf, t):
        k_t = k_ref[pl.ds(pl.multiple_of(t * blk, blk), blk), :]
        for mi in range(2):
            s_ref[hf, mi] = jnp.dot(k_t, qm_ref[mi, :, hf * half:(hf + 1) * half],
                                    preferred_element_type=F32)

    def softmax(hf, stats, masked):
        new, alphas = [], []
        for mi in range(2):
            m_old, l_old = stats[2 * mi], stats[2 * mi + 1]
            s = s_ref[hf, mi]
            if masked:
                s = jnp.where(key_chunk <= qry_chunk[hf], s, -jnp.inf)
            m_new = jnp.maximum(m_old, jnp.max(s, axis=0, keepdims=True))
            p = jnp.exp2(s - m_new)
            alpha = jnp.exp2(m_old - m_new)
            new += [m_new, alpha * l_old + jnp.sum(p, axis=0, keepdims=True)]
            alphas.append(alpha)
            p_ref[hf, mi] = p.astype(BF16)
        return tuple(new), tuple(alphas)

    def accumulate(hf, t, alphas):
        v_t = v_t_ref[:, pl.ds(pl.multiple_of(t * blk, blk), blk)]
        for mi in range(2):
            pv = jnp.dot(v_t, p_ref[hf, mi], preferred_element_type=F32)
            acc_ref[hf, mi] = alphas[mi] * acc_ref[hf, mi] + pv

    def q_tile(i, _):
        q_off = pl.multiple_of(i * blk, blk)
        q_t = q_t_ref[:, pl.ds(q_off, blk)]
        qm_ref[0] = jnp.concatenate([q_t[0:DA_QK_DIM], zero_half], axis=0)
        qm_ref[1] = jnp.concatenate([zero_half, q_t[DA_QK_DIM:]], axis=0)
        acc_ref[...] = jnp.zeros(acc_ref.shape, F32)
        p_ref[1] = jnp.zeros(p_ref.shape[1:], BF16)
        score(0, 0)

        neg = jnp.full((1, half), -jnp.inf, F32)
        zero = jnp.zeros((1, half), F32)
        one = jnp.ones((1, half), F32)
        stats0 = (neg, zero, neg, zero)

        def body(t, carry):
            st_lo, st_hi, al_hi = carry
            accumulate(1, jnp.maximum(t - 1, 0), al_hi)
            score(1, t)
            st_lo, al_lo = softmax(0, st_lo, False)
            accumulate(0, t, al_lo)
            score(0, t + 1)
            st_hi, al_hi = softmax(1, st_hi, False)
            return st_lo, st_hi, al_hi

        st_lo, st_hi, al_hi = lax.fori_loop(0, i, body, (stats0, stats0, (one, one)))
        accumulate(1, jnp.maximum(i - 1, 0), al_hi)
        score(1, i)
        st_lo, al_lo = softmax(0, st_lo, True)
        accumulate(0, i, al_lo)
        st_hi, al_hi = softmax(1, st_hi, True)
        accumulate(1, i, al_hi)

        for hf, st in enumerate((st_lo, st_hi)):
            o_t = acc_ref[hf, 0] / st[1] - lam * (acc_ref[hf, 1] / st[3])
            ms = jnp.mean(o_t * o_t, axis=0, keepdims=True)
            y_t = (o_t * lax.rsqrt(ms + NORM_EPS)) * g_ref[...] * (1.0 - LAM_INIT)
            o_ref[pl.ds(pl.multiple_of(q_off + hf * half, half), half), :] = y_t.T.astype(BF16)
        return 0

    lax.fori_loop(0, n_q, q_tile, 0)


def _diff_attn(lam_params, q_t, k, v_t, subln_col, batch, seq_len):
    tokens = k.shape[0]
    blk, half = ATTN_BLOCK, ATTN_HALF
    return pl.pallas_call(
        _attn_kernel,
        grid=(batch, DA_HEADS),
        in_specs=[
            _const_spec(lam_params.shape),
            pl.BlockSpec((DA_V_DIM, seq_len), lambda b, h: (h, b)),
            pl.BlockSpec((seq_len, DA_V_DIM), lambda b, h: (b, h)),
            pl.BlockSpec((DA_V_DIM, seq_len), lambda b, h: (h, b)),
            _const_spec(subln_col.shape),
        ],
        out_specs=pl.BlockSpec((seq_len, DA_V_DIM), lambda b, h: (b, h)),
        out_shape=jax.ShapeDtypeStruct((tokens, DA_WIDTH), BF16),
        scratch_shapes=[
            pltpu.VMEM((2, 2 * DA_QK_DIM, blk), BF16),
            pltpu.VMEM((2, 2, blk, half), F32),
            pltpu.VMEM((2, 2, blk, half), BF16),
            pltpu.VMEM((2, 2, DA_V_DIM, half), F32),
        ],
        compiler_params=_compiler_params(2),
        name="diff_attn",
    )(lam_params, q_t, k, v_t, subln_col)


def _mlstm_kernel(q_ref, k_ref, v_ref, o_ref, g_t_ref, g_col_ref, ng_ref, out_ref, c_ref, m_ref):
    chunk = q_ref.shape[0]

    @pl.when(pl.program_id(1) == 0)
    def _():
        c_ref[...] = jnp.zeros(c_ref.shape, F32)
        m_ref[...] = jnp.zeros(m_ref.shape, F32)

    t_id = lax.broadcasted_iota(jnp.int32, (chunk, chunk), 0)
    s_id = lax.broadcasted_iota(jnp.int32, (chunk, chunk), 1)
    causal = s_id <= t_id
    lower = causal.astype(F32)
    upper = (t_id <= s_id).astype(F32)

    g_t = g_t_ref[...]
    cum_row = jnp.dot(g_t, upper, preferred_element_type=F32, precision=lax.Precision.HIGHEST)
    cum_col = jnp.dot(lower, g_col_ref[...], preferred_element_type=F32,
                      precision=lax.Precision.HIGHEST)

    ones_col = (lax.broadcasted_iota(jnp.int32, (chunk, ML_DIM), 1) == 0).astype(BF16)

    for h in range(ML_HEADS):
        lanes = slice(h * ML_DIM, (h + 1) * ML_DIM)
        q = q_ref[:, lanes]
        k = k_ref[:, lanes]
        v_ext = jnp.concatenate([v_ref[:, lanes], ones_col], axis=1)
        b_row = cum_row[ML_HEADS + h:ML_HEADS + h + 1, :]
        i_row = g_t[h:h + 1, :]
        b_col = cum_col[:, ML_HEADS + h:ML_HEADS + h + 1]
        m_prev = m_ref[h]

        dmat = jnp.where(causal, (b_col - b_row) + i_row, -jnp.inf)
        inter = b_col + m_prev
        m_t = jnp.maximum(inter, jnp.max(dmat, axis=1, keepdims=True))
        w_intra = jnp.exp(dmat - m_t)
        w_inter = jnp.exp(inter - m_t)
        qk = lax.dot_general(q, k, _NT, preferred_element_type=F32)
        sc = (qk * w_intra).astype(BF16)
        c_ext = c_ref[h]
        num = (jnp.dot(sc, v_ext, preferred_element_type=F32)
               + w_inter * jnp.dot(q, c_ext.astype(BF16), preferred_element_type=F32))
        nq = num[:, ML_DIM:ML_DIM + 1]
        hid = num[:, 0:ML_DIM] / jnp.maximum(jnp.abs(nq), jnp.exp(-m_t))

        ms = jnp.mean(hid * hid, axis=-1, keepdims=True)
        hn = (hid * lax.rsqrt(ms + NORM_EPS)) * ng_ref[:, lanes]
        og = o_ref[:, lanes].astype(F32)
        out_ref[:, lanes] = (hn * (1.0 / (1.0 + jnp.exp(-og)))).astype(BF16)

        b_last = b_row[:, chunk - 1:chunk]
        g_row = (b_last - b_row) + i_row
        m_new = jnp.maximum(b_last + m_prev, jnp.max(g_row, axis=1, keepdims=True))
        decay = jnp.exp(b_last + m_prev - m_new)
        wk_row = jnp.exp(g_row - m_new)
        k_t_w = (k.astype(F32).T * wk_row).astype(BF16)
        c_ref[h] = decay * c_ext + jnp.dot(k_t_w, v_ext, preferred_element_type=F32)
        m_ref[h] = m_new


def _mlstm(mq, mk, mv, mo, g_t, g_col, norm_g, batch, seq_len):
    tokens = mq.shape[0]
    chunk = ML_CHUNK
    n_chunks = seq_len // chunk
    row_blk = lambda width: pl.BlockSpec((chunk, width), lambda b, c: (b * n_chunks + c, 0))
    return pl.pallas_call(
        _mlstm_kernel,
        grid=(batch, n_chunks),
        in_specs=[
            row_blk(ML_WIDTH), row_blk(ML_WIDTH), row_blk(ML_WIDTH), row_blk(ML_WIDTH),
            pl.BlockSpec((2 * ML_HEADS, chunk), lambda b, c: (0, b * n_chunks + c)),
            row_blk(V7X_LANES),
            _const_spec(norm_g.shape),
        ],
        out_specs=row_blk(ML_WIDTH),
        out_shape=jax.ShapeDtypeStruct((tokens, ML_WIDTH), BF16),
        scratch_shapes=[pltpu.VMEM((ML_HEADS, ML_DIM, 2 * ML_DIM), F32),
                        pltpu.VMEM((ML_HEADS, 1, 1), F32)],
        compiler_params=_compiler_params(2),
        name="mlstm",
    )(mq, mk, mv, mo, g_t, g_col, norm_g)


def _rms(x, g):
    ms = jnp.mean(x * x, axis=-1, keepdims=True)
    return (x * lax.rsqrt(ms + NORM_EPS)) * g


def _out_ffn_kernel(x_ref, attn_ref, ml_ref, w_out_ref, g_ffn_ref, w_gate_ref, w_up_ref, w_down_ref,
                    g_final_ref, out_ref):
    mix = jnp.concatenate([attn_ref[...], ml_ref[...]], axis=1)
    y = x_ref[...] + jnp.dot(mix, w_out_ref[...], preferred_element_type=F32)
    h2 = _rms(y, g_ffn_ref[...]).astype(BF16)
    gate = jnp.dot(h2, w_gate_ref[...], preferred_element_type=F32)
    up = jnp.dot(h2, w_up_ref[...], preferred_element_type=F32)
    act = ((gate * (1.0 / (1.0 + jnp.exp(-gate)))) * up).astype(BF16)
    y2 = y + jnp.dot(act, w_down_ref[...], preferred_element_type=F32)
    out_ref[...] = _rms(y2, g_final_ref[...])


def _out_ffn(x2d, attn, ml, w_out, g_ffn, w_gate, w_up, w_down, g_final):
    tokens = x2d.shape[0]
    rows = FFN_ROWS
    row_blk = lambda width: pl.BlockSpec((rows, width), lambda i: (i, 0))
    return pl.pallas_call(
        _out_ffn_kernel,
        grid=(tokens // rows,),
        in_specs=[
            row_blk(D_MODEL), row_blk(DA_WIDTH), row_blk(ML_WIDTH),
            _const_spec(w_out.shape), _const_spec(g_ffn.shape),
            _const_spec(w_gate.shape), _const_spec(w_up.shape), _const_spec(w_down.shape),
            _const_spec(g_final.shape),
        ],
        out_specs=row_blk(D_MODEL),
        out_shape=jax.ShapeDtypeStruct((tokens, D_MODEL), F32),
        compiler_params=_compiler_params(1),
        name="out_ffn",
    )(x2d, attn, ml, w_out, g_ffn, w_gate, w_up, w_down, g_final)


def kernel(x, positions, mix_norm_g, w_in, da_lambda, da_subln_g, ml_conv_w, ml_conv_b, ml_gate_b,
           ml_norm_g, w_out, ffn_norm_g, w_gate, w_up, w_down, final_norm_g):
    batch, seq_len, _ = x.shape
    tokens = batch * seq_len
    depth = w_in.shape[0]
    assert depth == 1, "one trunk layer"
    assert seq_len % PROJ_ROWS == 0 and seq_len % ATTN_BLOCK == 0 and seq_len % ML_CHUNK == 0
    assert ATTN_BLOCK % CHUNK == 0 and tokens % FFN_ROWS == 0

    x2d = x.reshape(tokens, D_MODEL)
    pos3d = positions.reshape(tokens // PROJ_ROWS, 1, PROJ_ROWS)

    w = w_in[0]
    w_t = w[:, OFF_DA_Q:OFF_ML].T.astype(BF16)
    gate_cols = w[:, OFF_GATE:OFF_GATE + 2 * ML_HEADS]
    wg_t = jnp.pad(gate_cols.T, ((0, 2 * ML_HEADS), (0, 0))).astype(BF16)
    w_row = jnp.concatenate(
        [w[:, OFF_ML:OFF_GATE], jnp.pad(gate_cols, ((0, 0), (0, V7X_LANES - 2 * ML_HEADS)))],
        axis=1).astype(BF16)
    inv_freq = (ROPE_THETA ** (-jnp.arange(0, ROT_DIM, 2, dtype=F32) / ROT_DIM)).reshape(ROT_HALF, 1)
    gb = ml_gate_b[0].astype(F32).reshape(2 * ML_HEADS)
    gb_col = gb.reshape(2 * ML_HEADS, 1)
    gb_row = jnp.pad(gb, (0, V7X_LANES - 2 * ML_HEADS)).reshape(1, V7X_LANES)

    q_t, k, v_t, mq, mk, mv, mo, g_t, g_col = _in_proj(
        x2d, pos3d, mix_norm_g[0].reshape(1, D_MODEL).astype(F32), w_t, wg_t, w_row, inv_freq,
        ml_conv_w[0].astype(F32), ml_conv_b[0].reshape(1, 2 * ML_WIDTH).astype(F32),
        gb_col, gb_row, seq_len)

    attn = _diff_attn(da_lambda[0].astype(F32), q_t, k, v_t,
                      da_subln_g[0].astype(F32).reshape(DA_V_DIM, 1), batch, seq_len)
    ml = _mlstm(mq, mk, mv, mo, g_t, g_col, ml_norm_g[0].astype(F32).reshape(1, ML_WIDTH),
                batch, seq_len)

    out = _out_ffn(x2d, attn, ml, w_out[0].astype(BF16),
                   ffn_norm_g[0].reshape(1, D_MODEL).astype(F32),
                   w_gate[0].astype(BF16), w_up[0].astype(BF16), w_down[0].astype(BF16),
                   final_norm_g.reshape(1, D_MODEL).astype(F32))
    return out.reshape(batch, seq_len, D_MODEL)
```

```python
import functools
import math

import jax
import jax.numpy as jnp
from jax import lax
from jax.experimental import pallas as pl
from jax.experimental.pallas import tpu as pltpu

F32 = jnp.float32
BF16 = jnp.bfloat16

D_MODEL = 1024
CHUNK = 64
NORM_EPS = 1e-6
DA_HEADS = 4
DA_QK_DIM = 64
DA_V_DIM = 128
DA_WIDTH = DA_HEADS * DA_V_DIM
ROPE_THETA = 500000.0
ROT_DIM = DA_QK_DIM // 4
ROT_HALF = ROT_DIM // 2
ML_HEADS = 4
ML_DIM = 128
ML_WIDTH = ML_HEADS * ML_DIM
CONV_WIDTH = 4
D_FF = 2816
LAM_INIT = 0.8 - 0.6 * math.exp(-0.3 * 0)
Q_SCALE = DA_QK_DIM ** -0.5 * math.log2(math.e)

OFF_DA_Q = 0
OFF_DA_K = 512
OFF_DA_V = 1024
OFF_ML = 1536
OFF_GATE = 3584

V7X_LANES = 128
V7X_SUBLANES = 8
V7X_BF16_ROWS_PER_VREG = 16
V7X_VMEM_LIMIT_BYTES = 56 * 1024 * 1024

PROJ_ROWS = 512
ATTN_BLOCK = 512
V_EXT_ROWS = DA_V_DIM + V7X_BF16_ROWS_PER_VREG
ML_CHUNK = 256
FFN_ROWS = 512
CONV_HALO = V7X_SUBLANES

_NT = (((1,), (1,)), ((), ()))


def _compiler_params(n_axes):
    return pltpu.CompilerParams(
        dimension_semantics=("arbitrary",) * n_axes,
        vmem_limit_bytes=V7X_VMEM_LIMIT_BYTES,
    )


def _const_spec(shape):
    zeros = (0,) * len(shape)
    return pl.BlockSpec(shape, lambda *_: zeros, pipeline_mode=pl.Buffered(1))


def _rope_rows(zt, cos, sin):
    pieces = []
    for g in range(2):
        base = g * DA_QK_DIM
        x1 = zt[base:base + ROT_HALF]
        x2 = zt[base + ROT_HALF:base + ROT_DIM]
        pieces += [x1 * cos - x2 * sin, x2 * cos + x1 * sin, zt[base + ROT_DIM:base + DA_QK_DIM]]
    return jnp.concatenate(pieces, axis=0)


def _in_proj_kernel(tiles_per_seq,
                    x_ref, pos_ref, g_ref, w_t_ref, wg_t_ref, w_row_ref, invf_ref,
                    convw_ref, convb_ref, gb_col_ref, gb_row_ref,
                    q_t_ref, k_ref, v_t_ref, mq_ref, mk_ref, mv_ref, mo_ref, g_t_ref, g_col_ref,
                    halo_ref):
    rows = x_ref.shape[0]
    x = x_ref[...]
    ms = jnp.mean(x * x, axis=-1, keepdims=True)
    hb = ((x * lax.rsqrt(ms + NORM_EPS)) * g_ref[...]).astype(BF16)

    z_t = lax.dot_general(w_t_ref[...], hb, _NT, preferred_element_type=F32)
    ang = invf_ref[...] * pos_ref[0].astype(F32)
    cos = jnp.cos(ang)
    sin = jnp.sin(ang)
    for h in range(DA_HEADS):
        lo = h * 2 * DA_QK_DIM
        hi = lo + 2 * DA_QK_DIM
        q_rot = _rope_rows(z_t[OFF_DA_Q + lo:OFF_DA_Q + hi], cos, sin)
        q_t_ref[lo:hi, :] = (q_rot * Q_SCALE).astype(BF16)
        k_rot = _rope_rows(z_t[OFF_DA_K + lo:OFF_DA_K + hi], cos, sin)
        k_ref[:, lo:hi] = k_rot.T.astype(BF16)
    pad_rows = V_EXT_ROWS - DA_V_DIM
    ones_row = (lax.broadcasted_iota(jnp.int32, (pad_rows, rows), 0) == 0).astype(BF16)
    for h in range(DA_HEADS):
        v_lo = OFF_DA_V + h * DA_V_DIM
        v_t_ref[h * V_EXT_ROWS:h * V_EXT_ROWS + DA_V_DIM, :] = z_t[v_lo:v_lo + DA_V_DIM].astype(BF16)
        v_t_ref[h * V_EXT_ROWS + DA_V_DIM:(h + 1) * V_EXT_ROWS, :] = ones_row

    gz = lax.dot_general(wg_t_ref[...], hb, _NT, preferred_element_type=F32)
    a_t = gz[0:2 * ML_HEADS] + gb_col_ref[...]
    ls_t = jnp.minimum(a_t, 0.0) - jnp.log1p(jnp.exp(-jnp.abs(a_t)))
    row_id = lax.broadcasted_iota(jnp.int32, a_t.shape, 0)
    g_t_ref[...] = jnp.where(row_id < ML_HEADS, a_t, ls_t)

    mz = jnp.dot(hb, w_row_ref[...], preferred_element_type=F32)
    a_c = mz[:, 4 * ML_WIDTH:] + gb_row_ref[...]
    ls_c = jnp.minimum(a_c, 0.0) - jnp.log1p(jnp.exp(-jnp.abs(a_c)))
    lane_id = lax.broadcasted_iota(jnp.int32, a_c.shape, 1)
    g_col_ref[...] = jnp.where(lane_id < ML_HEADS, a_c, ls_c)

    mv_ref[...] = mz[:, 2 * ML_WIDTH:3 * ML_WIDTH].astype(BF16)
    mo_ref[...] = mz[:, 3 * ML_WIDTH:4 * ML_WIDTH].astype(BF16)

    first = (pl.program_id(0) % tiles_per_seq) == 0

    @pl.when(first)
    def _():
        halo_ref[0:CONV_HALO, :] = jnp.zeros((CONV_HALO, 2 * ML_WIDTH), F32)

    @pl.when(jnp.logical_not(first))
    def _():
        halo_ref[0:CONV_HALO, :] = halo_ref[rows:rows + CONV_HALO, :]

    halo_ref[CONV_HALO:CONV_HALO + rows, :] = mz[:, 0:2 * ML_WIDTH]
    conv = convb_ref[...]
    for j in range(CONV_WIDTH):
        shift = CONV_WIDTH - 1 - j
        conv = conv + convw_ref[j:j + 1, :] * halo_ref[CONV_HALO - shift:CONV_HALO - shift + rows, :]
    act = conv * (1.0 / (1.0 + jnp.exp(-conv)))
    mq_ref[...] = (act[:, 0:ML_WIDTH] * (ML_DIM ** -0.5)).astype(BF16)
    mk_ref[...] = act[:, ML_WIDTH:2 * ML_WIDTH].astype(BF16)


def _in_proj(x2d, pos3d, norm_g, w_t, wg_t, w_row, inv_freq, conv_w, conv_b, gb_col, gb_row, seq_len):
    tokens = x2d.shape[0]
    rows = PROJ_ROWS
    n_tiles = tokens // rows
    row_blk = lambda width: pl.BlockSpec((rows, width), lambda i: (i, 0))
    col_blk = lambda height: pl.BlockSpec((height, rows), lambda i: (0, i))
    out_shape = (
        jax.ShapeDtypeStruct((DA_WIDTH, tokens), BF16),
        jax.ShapeDtypeStruct((tokens, DA_WIDTH), BF16),
        jax.ShapeDtypeStruct((DA_HEADS * V_EXT_ROWS, tokens), BF16),
        jax.ShapeDtypeStruct((tokens, ML_WIDTH), BF16),
        jax.ShapeDtypeStruct((tokens, ML_WIDTH), BF16),
        jax.ShapeDtypeStruct((tokens, ML_WIDTH), BF16),
        jax.ShapeDtypeStruct((tokens, ML_WIDTH), BF16),
        jax.ShapeDtypeStruct((2 * ML_HEADS, tokens), F32),
        jax.ShapeDtypeStruct((tokens, V7X_LANES), F32),
    )
    return pl.pallas_call(
        functools.partial(_in_proj_kernel, seq_len // rows),
        grid=(n_tiles,),
        in_specs=[
            row_blk(D_MODEL),
            pl.BlockSpec((1, 1, rows), lambda i: (i, 0, 0)),
            _const_spec(norm_g.shape),
            _const_spec(w_t.shape),
            _const_spec(wg_t.shape),
            _const_spec(w_row.shape),
            _const_spec(inv_freq.shape),
            _const_spec(conv_w.shape),
            _const_spec(conv_b.shape),
            _const_spec(gb_col.shape),
            _const_spec(gb_row.shape),
        ],
        out_specs=(
            col_blk(DA_WIDTH), row_blk(DA_WIDTH), col_blk(DA_HEADS * V_EXT_ROWS),
            row_blk(ML_WIDTH), row_blk(ML_WIDTH), row_blk(ML_WIDTH), row_blk(ML_WIDTH),
            col_blk(2 * ML_HEADS), row_blk(V7X_LANES),
        ),
        out_shape=out_shape,
        scratch_shapes=[pltpu.VMEM((rows + CONV_HALO, 2 * ML_WIDTH), F32)],
        compiler_params=_compiler_params(1),
        name="in_proj",
    )(x2d, pos3d, norm_g, w_t, wg_t, w_row, inv_freq, conv_w, conv_b, gb_col, gb_row)


def _attn_kernel(lam_ref, q_t_ref, k_ref, v_t_ref, g_ref, o_ref, qm_ref, s_ref, p_ref, acc_ref):
    seq = k_ref.shape[0]
    blk = ATTN_BLOCK
    n_q = seq // blk

    lv = lam_ref[...]
    lam = (jnp.exp(jnp.sum(lv[0:1] * lv[1:2], axis=1, keepdims=True))
           - jnp.exp(jnp.sum(lv[2:3] * lv[3:4], axis=1, keepdims=True)) + LAM_INIT)

    key_chunk = lax.broadcasted_iota(jnp.int32, (blk, blk), 0) // CHUNK
    qry_chunk = lax.broadcasted_iota(jnp.int32, (blk, blk), 1) // CHUNK
    zero_half = jnp.zeros((DA_QK_DIM, blk), BF16)

    def score(par, t):
        k_t = k_ref[pl.ds(pl.multiple_of(t * blk, blk), blk), :]
        s_ref[par] = jnp.dot(k_t, qm_ref[...], preferred_element_type=F32)

    def softmax(par, maxes, masked):
        new, alphas = [], []
        for mi in range(2):
            lanes = slice(mi * blk, (mi + 1) * blk)
            s = s_ref[par, :, lanes]
            if masked:
                s = jnp.where(key_chunk <= qry_chunk, s, -jnp.inf)
            m_new = jnp.maximum(maxes[mi], jnp.max(s, axis=0, keepdims=True))
            p_ref[par, :, lanes] = jnp.exp2(s - m_new).astype(BF16)
            alphas.append(jnp.exp2(maxes[mi] - m_new))
            new.append(m_new)
        return tuple(new), tuple(alphas)

    def accumulate(par, t, alphas):
        v_t = v_t_ref[:, pl.ds(pl.multiple_of(t * blk, blk), blk)]
        pv = jnp.dot(v_t, p_ref[par], preferred_element_type=F32)
        for mi in range(2):
            lanes = slice(mi * blk, (mi + 1) * blk)
            acc_ref[:, lanes] = alphas[mi] * acc_ref[:, lanes] + pv[:, lanes]

    def step(par, t, carry, masked=False, last=False):
        maxes, alphas = carry
        if not last:
            score(1 - par, t + 1)
        accumulate(1 - par, jnp.maximum(t - 1, 0), alphas)
        return softmax(par, maxes, masked)

    def q_tile(i, _):
        q_off = pl.multiple_of(i * blk, blk)
        q_t = q_t_ref[:, pl.ds(q_off, blk)]
        qm_ref[:, 0:blk] = jnp.concatenate([q_t[0:DA_QK_DIM], zero_half], axis=0)
        qm_ref[:, blk:2 * blk] = jnp.concatenate([zero_half, q_t[DA_QK_DIM:]], axis=0)
        acc_ref[...] = jnp.zeros(acc_ref.shape, F32)
        p_ref[1] = jnp.zeros(p_ref.shape[1:], BF16)
        score(0, 0)

        neg = jnp.full((1, blk), -jnp.inf, F32)
        one = jnp.ones((1, blk), F32)
        carry0 = ((neg, neg), (one, one))

        def pair(u, carry):
            carry = step(0, 2 * u, carry)
            return step(1, 2 * u + 1, carry)

        carry = lax.fori_loop(0, lax.shift_right_logical(i, 1), pair, carry0)

        def finish(par, carry):
            _, alphas = step(par, i, carry, masked=True, last=True)
            accumulate(par, i, alphas)
            acc = acc_ref[...]
            o1 = acc[0:DA_V_DIM, 0:blk] / acc[DA_V_DIM:DA_V_DIM + 1, 0:blk]
            o2 = acc[0:DA_V_DIM, blk:2 * blk] / acc[DA_V_DIM:DA_V_DIM + 1, blk:2 * blk]
            o_t = o1 - lam * o2
            ms = jnp.mean(o_t * o_t, axis=0, keepdims=True)
            y_t = (o_t * lax.rsqrt(ms + NORM_EPS)) * g_ref[...] * (1.0 - LAM_INIT)
            o_ref[pl.ds(q_off, blk), :] = y_t.T.astype(BF16)

        odd = (i & 1) == 1

        @pl.when(jnp.logical_not(odd))
        def _():
            finish(0, carry)

        @pl.when(odd)
        def _():
            finish(1, step(0, i - 1, carry))

        return 0

    lax.fori_loop(0, n_q, q_tile, 0)


def _diff_attn(lam_params, q_t, k, v_t, subln_col, batch, seq_len):
    tokens = k.shape[0]
    blk = ATTN_BLOCK
    return pl.pallas_call(
        _attn_kernel,
        grid=(batch, DA_HEADS),
        in_specs=[
            _const_spec(lam_params.shape),
            pl.BlockSpec((DA_V_DIM, seq_len), lambda b, h: (h, b)),
            pl.BlockSpec((seq_len, DA_V_DIM), lambda b, h: (b, h)),
            pl.BlockSpec((V_EXT_ROWS, seq_len), lambda b, h: (h, b)),
            _const_spec(subln_col.shape),
        ],
        out_specs=pl.BlockSpec((seq_len, DA_V_DIM), lambda b, h: (b, h)),
        out_shape=jax.ShapeDtypeStruct((tokens, DA_WIDTH), BF16),
        scratch_shapes=[
            pltpu.VMEM((2 * DA_QK_DIM, 2 * blk), BF16),
            pltpu.VMEM((2, blk, 2 * blk), F32),
            pltpu.VMEM((2, blk, 2 * blk), BF16),
            pltpu.VMEM((V_EXT_ROWS, 2 * blk), F32),
        ],
        compiler_params=_compiler_params(2),
        name="diff_attn",
    )(lam_params, q_t, k, v_t, subln_col)


def _mlstm_kernel(q_ref, k_ref, v_ref, o_ref, g_t_ref, g_col_ref, ng_ref, out_ref, c_ref, m_ref):
    chunk = q_ref.shape[0]

    @pl.when(pl.program_id(1) == 0)
    def _():
        c_ref[...] = jnp.zeros(c_ref.shape, F32)
        m_ref[...] = jnp.zeros(m_ref.shape, F32)

    t_id = lax.broadcasted_iota(jnp.int32, (chunk, chunk), 0)
    s_id = lax.broadcasted_iota(jnp.int32, (chunk, chunk), 1)
    causal = s_id <= t_id
    lower = causal.astype(F32)
    upper = (t_id <= s_id).astype(F32)

    g_t = g_t_ref[...]
    cum_row = jnp.dot(g_t, upper, preferred_element_type=F32, precision=lax.Precision.HIGHEST)
    cum_col = jnp.dot(lower, g_col_ref[...], preferred_element_type=F32,
                      precision=lax.Precision.HIGHEST)

    ones_col = (lax.broadcasted_iota(jnp.int32, (chunk, ML_DIM), 1) == 0).astype(BF16)

    for h in range(ML_HEADS):
        lanes = slice(h * ML_DIM, (h + 1) * ML_DIM)
        q = q_ref[:, lanes]
        k = k_ref[:, lanes]
        v_ext = jnp.concatenate([v_ref[:, lanes], ones_col], axis=1)
        b_row = cum_row[ML_HEADS + h:ML_HEADS + h + 1, :]
        i_row = g_t[h:h + 1, :]
        b_col = cum_col[:, ML_HEADS + h:ML_HEADS + h + 1]
        m_prev = m_ref[h]

        dmat = jnp.where(causal, (b_col - b_row) + i_row, -jnp.inf)
        inter = b_col + m_prev
        m_t = jnp.maximum(inter, jnp.max(dmat, axis=1, keepdims=True))
        w_intra = jnp.exp(dmat - m_t)
        w_inter = jnp.exp(inter - m_t)
        qk = lax.dot_general(q, k, _NT, preferred_element_type=F32)
        sc = (qk * w_intra).astype(BF16)
        c_ext = c_ref[h]
        num = (jnp.dot(sc, v_ext, preferred_element_type=F32)
               + w_inter * jnp.dot(q, c_ext.astype(BF16), preferred_element_type=F32))
        nq = num[:, ML_DIM:ML_DIM + 1]
        hid = num[:, 0:ML_DIM] / jnp.maximum(jnp.abs(nq), jnp.exp(-m_t))

        ms = jnp.mean(hid * hid, axis=-1, keepdims=True)
        hn = (hid * lax.rsqrt(ms + NORM_EPS)) * ng_ref[:, lanes]
        og = o_ref[:, lanes].astype(F32)
        out_ref[:, lanes] = (hn * (1.0 / (1.0 + jnp.exp(-og)))).astype(BF16)

        b_last = b_row[:, chunk - 1:chunk]
        g_row = (b_last - b_row) + i_row
        m_new = jnp.maximum(b_last + m_prev, jnp.max(g_row, axis=1, keepdims=True))
        decay = jnp.exp(b_last + m_prev - m_new)
        wk_row = jnp.exp(g_row - m_new)
        k_t_w = (k.astype(F32).T * wk_row).astype(BF16)
        c_ref[h] = decay * c_ext + jnp.dot(k_t_w, v_ext, preferred_element_type=F32)
        m_ref[h] = m_new


def _mlstm(mq, mk, mv, mo, g_t, g_col, norm_g, batch, seq_len):
    tokens = mq.shape[0]
    chunk = ML_CHUNK
    n_chunks = seq_len // chunk
    row_blk = lambda width: pl.BlockSpec((chunk, width), lambda b, c: (b * n_chunks + c, 0))
    return pl.pallas_call(
        _mlstm_kernel,
        grid=(batch, n_chunks),
        in_specs=[
            row_blk(ML_WIDTH), row_blk(ML_WIDTH), row_blk(ML_WIDTH), row_blk(ML_WIDTH),
            pl.BlockSpec((2 * ML_HEADS, chunk), lambda b, c: (0, b * n_chunks + c)),
            row_blk(V7X_LANES),
            _const_spec(norm_g.shape),
        ],
        out_specs=row_blk(ML_WIDTH),
        out_shape=jax.ShapeDtypeStruct((tokens, ML_WIDTH), BF16),
        scratch_shapes=[pltpu.VMEM((ML_HEADS, ML_DIM, 2 * ML_DIM), F32),
                        pltpu.VMEM((ML_HEADS, 1, 1), F32)],
        compiler_params=_compiler_params(2),
        name="mlstm",
    )(mq, mk, mv, mo, g_t, g_col, norm_g)


def _rms(x, g):
    ms = jnp.mean(x * x, axis=-1, keepdims=True)
    return (x * lax.rsqrt(ms + NORM_EPS)) * g


def _out_ffn_kernel(x_ref, attn_ref, ml_ref, w_out_ref, g_ffn_ref, w_gate_ref, w_up_ref, w_down_ref,
                    g_final_ref, out_ref):
    mix = jnp.concatenate([attn_ref[...], ml_ref[...]], axis=1)
    y = x_ref[...] + jnp.dot(mix, w_out_ref[...], preferred_element_type=F32)
    h2 = _rms(y, g_ffn_ref[...]).astype(BF16)
    gate = jnp.dot(h2, w_gate_ref[...], preferred_element_type=F32)
    up = jnp.dot(h2, w_up_ref[...], preferred_element_type=F32)
    act = ((gate * (1.0 / (1.0 + jnp.exp(-gate)))) * up).astype(BF16)
    y2 = y + jnp.dot(act, w_down_ref[...], preferred_element_type=F32)
    out_ref[...] = _rms(y2, g_final_ref[...])


def _out_ffn(x2d, attn, ml, w_out, g_ffn, w_gate, w_up, w_down, g_final):
    tokens = x2d.shape[0]
    rows = FFN_ROWS
    row_blk = lambda width: pl.BlockSpec((rows, width), lambda i: (i, 0))
    return pl.pallas_call(
        _out_ffn_kernel,
        grid=(tokens // rows,),
        in_specs=[
            row_blk(D_MODEL), row_blk(DA_WIDTH), row_blk(ML_WIDTH),
            _const_spec(w_out.shape), _const_spec(g_ffn.shape),
            _const_spec(w_gate.shape), _const_spec(w_up.shape), _const_spec(w_down.shape),
            _const_spec(g_final.shape),
        ],
        out_specs=row_blk(D_MODEL),
        out_shape=jax.ShapeDtypeStruct((tokens, D_MODEL), F32),
        compiler_params=_compiler_params(1),
        name="out_ffn",
    )(x2d, attn, ml, w_out, g_ffn, w_gate, w_up, w_down, g_final)


def kernel(x, positions, mix_norm_g, w_in, da_lambda, da_subln_g, ml_conv_w, ml_conv_b, ml_gate_b,
           ml_norm_g, w_out, ffn_norm_g, w_gate, w_up, w_down, final_norm_g):
    batch, seq_len, _ = x.shape
    tokens = batch * seq_len
    depth = w_in.shape[0]
    assert depth == 1, "one trunk layer"
    assert seq_len % PROJ_ROWS == 0 and seq_len % ATTN_BLOCK == 0 and seq_len % ML_CHUNK == 0
    assert ATTN_BLOCK % CHUNK == 0 and tokens % FFN_ROWS == 0

    x2d = x.reshape(tokens, D_MODEL)
    pos3d = positions.reshape(tokens // PROJ_ROWS, 1, PROJ_ROWS)

    w = w_in[0]
    w_t = w[:, OFF_DA_Q:OFF_ML].T.astype(BF16)
    gate_cols = w[:, OFF_GATE:OFF_GATE + 2 * ML_HEADS]
    wg_t = jnp.pad(gate_cols.T, ((0, 2 * ML_HEADS), (0, 0))).astype(BF16)
    w_row = jnp.concatenate(
        [w[:, OFF_ML:OFF_GATE], jnp.pad(gate_cols, ((0, 0), (0, V7X_LANES - 2 * ML_HEADS)))],
        axis=1).astype(BF16)
    inv_freq = (ROPE_THETA ** (-jnp.arange(0, ROT_DIM, 2, dtype=F32) / ROT_DIM)).reshape(ROT_HALF, 1)
    gb = ml_gate_b[0].astype(F32).reshape(2 * ML_HEADS)
    gb_col = gb.reshape(2 * ML_HEADS, 1)
    gb_row = jnp.pad(gb, (0, V7X_LANES - 2 * ML_HEADS)).reshape(1, V7X_LANES)

    q_t, k, v_t, mq, mk, mv, mo, g_t, g_col = _in_proj(
        x2d, pos3d, mix_norm_g[0].reshape(1, D_MODEL).astype(F32), w_t, wg_t, w_row, inv_freq,
        ml_conv_w[0].astype(F32), ml_conv_b[0].reshape(1, 2 * ML_WIDTH).astype(F32),
        gb_col, gb_row, seq_len)

    attn = _diff_attn(da_lambda[0].astype(F32), q_t, k, v_t,
                      da_subln_g[0].astype(F32).reshape(DA_V_DIM, 1), batch, seq_len)
    ml = _mlstm(mq, mk, mv, mo, g_t, g_col, ml_norm_g[0].astype(F32).reshape(1, ML_WIDTH),
                batch, seq_len)

    out = _out_ffn(x2d, attn, ml, w_out[0].astype(BF16),
                   ffn_norm_g[0].reshape(1, D_MODEL).astype(F32),
                   w_gate[0].astype(BF16), w_up[0].astype(BF16), w_down[0].astype(BF16),
                   final_norm_g.reshape(1, D_MODEL).astype(F32))
    return out.reshape(batch, seq_len, D_MODEL)
```

```python
import functools
import math

import jax
import jax.numpy as jnp
from jax import lax
from jax.experimental import pallas as pl
from jax.experimental.pallas import tpu as pltpu

F32 = jnp.float32
BF16 = jnp.bfloat16

D_MODEL = 1024
CHUNK = 64
NORM_EPS = 1e-6
DA_HEADS = 4
DA_QK_DIM = 64
DA_V_DIM = 128
DA_WIDTH = DA_HEADS * DA_V_DIM
ROPE_THETA = 500000.0
ROT_DIM = DA_QK_DIM // 4
ROT_HALF = ROT_DIM // 2
ML_HEADS = 4
ML_DIM = 128
ML_WIDTH = ML_HEADS * ML_DIM
CONV_WIDTH = 4
D_FF = 2816
LAM_INIT = 0.8 - 0.6 * math.exp(-0.3 * 0)
Q_SCALE = DA_QK_DIM ** -0.5 * math.log2(math.e)

OFF_DA_Q = 0
OFF_DA_K = 512
OFF_DA_V = 1024
OFF_ML = 1536
OFF_GATE = 3584

V7X_LANES = 128
V7X_SUBLANES = 8
V7X_BF16_ROWS_PER_VREG = 16
V7X_MXU_DEPTH = 256
V7X_VMEM_LIMIT_BYTES = 56 * 1024 * 1024

PROJ_ROWS = 512
ATTN_Q = 1024
ATTN_K = 512
MASK_FEATS = ATTN_Q // CHUNK
MASK_BIG = 1e30
V_EXT_ROWS = DA_V_DIM + V7X_BF16_ROWS_PER_VREG
ML_CHUNK = 256
FFN_ROWS = 512
CONV_HALO = V7X_SUBLANES

_NT = (((1,), (1,)), ((), ()))


def _compiler_params(n_axes):
    return pltpu.CompilerParams(
        dimension_semantics=("arbitrary",) * n_axes,
        vmem_limit_bytes=V7X_VMEM_LIMIT_BYTES,
    )


def _const_spec(shape):
    zeros = (0,) * len(shape)
    return pl.BlockSpec(shape, lambda *_: zeros, pipeline_mode=pl.Buffered(1))


def _rope_rows(zt, cos, sin):
    pieces = []
    for g in range(2):
        base = g * DA_QK_DIM
        x1 = zt[base:base + ROT_HALF]
        x2 = zt[base + ROT_HALF:base + ROT_DIM]
        pieces += [x1 * cos - x2 * sin, x2 * cos + x1 * sin, zt[base + ROT_DIM:base + DA_QK_DIM]]
    return jnp.concatenate(pieces, axis=0)


def _in_proj_kernel(tiles_per_seq,
                    x_ref, pos_ref, g_ref, w_t_ref, wg_t_ref, w_row_ref, invf_ref,
                    convw_ref, convb_ref, gb_col_ref, gb_row_ref,
                    q_t_ref, k_ref, v_t_ref, mq_ref, mk_ref, mv_ref, mo_ref, g_t_ref, g_col_ref,
                    halo_ref):
    rows = x_ref.shape[0]
    x = x_ref[...]
    ms = jnp.mean(x * x, axis=-1, keepdims=True)
    hb = ((x * lax.rsqrt(ms + NORM_EPS)) * g_ref[...]).astype(BF16)

    first = (pl.program_id(0) % tiles_per_seq) == 0
    halo_ref[0:CONV_HALO, :] = jnp.where(first, 0.0, halo_ref[rows:rows + CONV_HALO, :])
    halo_ref[CONV_HALO:CONV_HALO + rows, :] = jnp.dot(
        hb, w_row_ref[:, 0:2 * ML_WIDTH], preferred_element_type=F32)

    z_t = lax.dot_general(w_t_ref[...], hb, _NT, preferred_element_type=F32)

    xe = halo_ref[...]
    conv = convw_ref[0:1, :] * xe
    for j in range(1, CONV_WIDTH):
        conv = pltpu.roll(conv, 1, axis=0) + convw_ref[j:j + 1, :] * xe
    conv = conv[CONV_HALO:, :] + convb_ref[...]
    act = conv * (1.0 / (1.0 + jnp.exp(-conv)))
    mq_ref[...] = (act[:, 0:ML_WIDTH] * (ML_DIM ** -0.5)).astype(BF16)
    mk_ref[...] = act[:, ML_WIDTH:2 * ML_WIDTH].astype(BF16)

    ang = invf_ref[...] * pos_ref[0].astype(F32)
    cos = jnp.cos(ang)
    sin = jnp.sin(ang)
    for h in range(DA_HEADS):
        lo = h * 2 * DA_QK_DIM
        hi = lo + 2 * DA_QK_DIM
        q_rot = _rope_rows(z_t[OFF_DA_Q + lo:OFF_DA_Q + hi], cos, sin)
        q_t_ref[lo:hi, :] = (q_rot * Q_SCALE).astype(BF16)
        k_rot = _rope_rows(z_t[OFF_DA_K + lo:OFF_DA_K + hi], cos, sin)
        k_ref[:, lo:hi] = k_rot.T.astype(BF16)
    pad_rows = V_EXT_ROWS - DA_V_DIM
    ones_row = (lax.broadcasted_iota(jnp.int32, (pad_rows, rows), 0) == 0).astype(BF16)
    for h in range(DA_HEADS):
        v_lo = OFF_DA_V + h * DA_V_DIM
        v_t_ref[h * V_EXT_ROWS:h * V_EXT_ROWS + DA_V_DIM, :] = z_t[v_lo:v_lo + DA_V_DIM].astype(BF16)
        v_t_ref[h * V_EXT_ROWS + DA_V_DIM:(h + 1) * V_EXT_ROWS, :] = ones_row

    gz = lax.dot_general(wg_t_ref[...], hb, _NT, preferred_element_type=F32)
    a_t = gz[0:2 * ML_HEADS] + gb_col_ref[...]
    ls_t = jnp.minimum(a_t, 0.0) - jnp.log1p(jnp.exp(-jnp.abs(a_t)))
    row_id = lax.broadcasted_iota(jnp.int32, a_t.shape, 0)
    g_t_ref[...] = jnp.where(row_id < ML_HEADS, a_t, ls_t)

    mz = jnp.dot(hb, w_row_ref[:, 2 * ML_WIDTH:], preferred_element_type=F32)
    a_c = mz[:, 2 * ML_WIDTH:] + gb_row_ref[...]
    ls_c = jnp.minimum(a_c, 0.0) - jnp.log1p(jnp.exp(-jnp.abs(a_c)))
    lane_id = lax.broadcasted_iota(jnp.int32, a_c.shape, 1)
    g_col_ref[...] = jnp.where(lane_id < ML_HEADS, a_c, ls_c)

    mv_ref[...] = mz[:, 0:ML_WIDTH].astype(BF16)
    mo_ref[...] = mz[:, ML_WIDTH:2 * ML_WIDTH].astype(BF16)


def _in_proj(x2d, pos3d, norm_g, w_t, wg_t, w_row, inv_freq, conv_w, conv_b, gb_col, gb_row, seq_len):
    tokens = x2d.shape[0]
    rows = PROJ_ROWS
    n_tiles = tokens // rows
    row_blk = lambda width: pl.BlockSpec((rows, width), lambda i: (i, 0))
    col_blk = lambda height: pl.BlockSpec((height, rows), lambda i: (0, i))
    out_shape = (
        jax.ShapeDtypeStruct((DA_WIDTH, tokens), BF16),
        jax.ShapeDtypeStruct((tokens, DA_WIDTH), BF16),
        jax.ShapeDtypeStruct((DA_HEADS * V_EXT_ROWS, tokens), BF16),
        jax.ShapeDtypeStruct((tokens, ML_WIDTH), BF16),
        jax.ShapeDtypeStruct((tokens, ML_WIDTH), BF16),
        jax.ShapeDtypeStruct((tokens, ML_WIDTH), BF16),
        jax.ShapeDtypeStruct((tokens, ML_WIDTH), BF16),
        jax.ShapeDtypeStruct((2 * ML_HEADS, tokens), F32),
        jax.ShapeDtypeStruct((tokens, V7X_LANES), F32),
    )
    return pl.pallas_call(
        functools.partial(_in_proj_kernel, seq_len // rows),
        grid=(n_tiles,),
        in_specs=[
            row_blk(D_MODEL),
            pl.BlockSpec((1, 1, rows), lambda i: (i, 0, 0)),
            _const_spec(norm_g.shape),
            _const_spec(w_t.shape),
            _const_spec(wg_t.shape),
            _const_spec(w_row.shape),
            _const_spec(inv_freq.shape),
            _const_spec(conv_w.shape),
            _const_spec(conv_b.shape),
            _const_spec(gb_col.shape),
            _const_spec(gb_row.shape),
        ],
        out_specs=(
            col_blk(DA_WIDTH), row_blk(DA_WIDTH), col_blk(DA_HEADS * V_EXT_ROWS),
            row_blk(ML_WIDTH), row_blk(ML_WIDTH), row_blk(ML_WIDTH), row_blk(ML_WIDTH),
            col_blk(2 * ML_HEADS), row_blk(V7X_LANES),
        ),
        out_shape=out_shape,
        scratch_shapes=[pltpu.VMEM((rows + CONV_HALO, 2 * ML_WIDTH), F32)],
        compiler_params=_compiler_params(1),
        name="in_proj",
    )(x2d, pos3d, norm_g, w_t, wg_t, w_row, inv_freq, conv_w, conv_b, gb_col, gb_row)


def _attn_kernel(lam_ref, q_t_ref, k_ref, v_t_ref, g_ref, o_ref,
                 qm_ref, ind_ref, s_ref, p_ref, acc_ref):
    seq = k_ref.shape[0]
    tq, tk = ATTN_Q, ATTN_K
    n_q = seq // tq
    diag_tiles = tq // tk

    lv = lam_ref[...]
    lam = (jnp.exp(jnp.sum(lv[0:1] * lv[1:2], axis=1, keepdims=True))
           - jnp.exp(jnp.sum(lv[2:3] * lv[3:4], axis=1, keepdims=True)) + LAM_INIT)

    feat = lax.broadcasted_iota(jnp.int32, (MASK_FEATS, 2 * tq), 0)
    qchunk = (lax.broadcasted_iota(jnp.int32, (MASK_FEATS, 2 * tq), 1) % tq) // CHUNK
    qm_ref[...] = jnp.zeros(qm_ref.shape, BF16)
    qm_ref[2 * DA_QK_DIM:2 * DA_QK_DIM + MASK_FEATS, :] = jnp.where(
        feat > qchunk, -MASK_BIG, 0.0).astype(BF16)
    kchunk = lax.broadcasted_iota(jnp.int32, (tk, 2 * DA_QK_DIM), 0) // CHUNK
    lane = lax.broadcasted_iota(jnp.int32, (tk, 2 * DA_QK_DIM), 1)
    ind_ref[0] = jnp.zeros((tk, 2 * DA_QK_DIM), BF16)
    for d in range(diag_tiles):
        ind_ref[d + 1] = (lane == kchunk + d * (tk // CHUNK)).astype(BF16)

    def q_tile(i, _):
        q_off = pl.multiple_of(i * tq, tq)
        n_t = diag_tiles * (i + 1)

        def score(par, t):
            k_t = k_ref[pl.ds(pl.multiple_of(t * tk, tk), tk), :]
            which = jnp.maximum(t - (n_t - diag_tiles) + 1, 0)
            k_ext = jnp.concatenate([k_t, ind_ref[which]], axis=1)
            s_ref[par] = jnp.dot(k_ext, qm_ref[...], preferred_element_type=F32)

        def softmax(par, maxes):
            new, alphas = [], []
            for mi in range(2):
                lanes = slice(mi * tq, (mi + 1) * tq)
                s = s_ref[par, :, lanes]
                m_new = jnp.maximum(maxes[mi], jnp.max(s, axis=0, keepdims=True))
                p_ref[par, :, lanes] = jnp.exp2(s - m_new).astype(BF16)
                alphas.append(jnp.exp2(maxes[mi] - m_new))
                new.append(m_new)
            return tuple(new), tuple(alphas)

        def accumulate(par, t, alphas):
            v_t = v_t_ref[:, pl.ds(pl.multiple_of(t * tk, tk), tk)]
            pv = jnp.dot(v_t, p_ref[par], preferred_element_type=F32)
            for mi in range(2):
                lanes = slice(mi * tq, (mi + 1) * tq)
                acc_ref[:, lanes] = alphas[mi] * acc_ref[:, lanes] + pv[:, lanes]

        def step(par, t, carry, last=False):
            maxes, alphas = carry
            if not last:
                score(1 - par, t + 1)
            accumulate(1 - par, jnp.maximum(t - 1, 0), alphas)
            return softmax(par, maxes)

        q_t = q_t_ref[:, pl.ds(q_off, tq)]
        qm_ref[0:DA_QK_DIM, 0:tq] = q_t[0:DA_QK_DIM]
        qm_ref[DA_QK_DIM:2 * DA_QK_DIM, tq:2 * tq] = q_t[DA_QK_DIM:]
        acc_ref[...] = jnp.zeros(acc_ref.shape, F32)
        p_ref[1] = jnp.zeros(p_ref.shape[1:], BF16)
        score(0, 0)

        neg = jnp.full((1, tq), -jnp.inf, F32)
        one = jnp.ones((1, tq), F32)

        def pair(u, carry):
            carry = step(0, 2 * u, carry)
            return step(1, 2 * u + 1, carry)

        carry = lax.fori_loop(0, n_t // 2 - 1, pair, ((neg, neg), (one, one)))
        carry = step(0, n_t - 2, carry)
        _, alphas = step(1, n_t - 1, carry, last=True)
        accumulate(1, n_t - 1, alphas)

        acc = acc_ref[...]
        o1 = acc[0:DA_V_DIM, 0:tq] / acc[DA_V_DIM:DA_V_DIM + 1, 0:tq]
        o2 = acc[0:DA_V_DIM, tq:2 * tq] / acc[DA_V_DIM:DA_V_DIM + 1, tq:2 * tq]
        o_t = o1 - lam * o2
        ms = jnp.mean(o_t * o_t, axis=0, keepdims=True)
        y_t = (o_t * lax.rsqrt(ms + NORM_EPS)) * g_ref[...] * (1.0 - LAM_INIT)
        o_ref[pl.ds(q_off, tq), :] = y_t.T.astype(BF16)
        return 0

    lax.fori_loop(0, n_q, q_tile, 0)


def _diff_attn(lam_params, q_t, k, v_t, subln_col, batch, seq_len):
    tokens = k.shape[0]
    tq, tk = ATTN_Q, ATTN_K
    return pl.pallas_call(
        _attn_kernel,
        grid=(batch, DA_HEADS),
        in_specs=[
            _const_spec(lam_params.shape),
            pl.BlockSpec((DA_V_DIM, seq_len), lambda b, h: (h, b)),
            pl.BlockSpec((seq_len, DA_V_DIM), lambda b, h: (b, h)),
            pl.BlockSpec((V_EXT_ROWS, seq_len), lambda b, h: (h, b)),
            _const_spec(subln_col.shape),
        ],
        out_specs=pl.BlockSpec((seq_len, DA_V_DIM), lambda b, h: (b, h)),
        out_shape=jax.ShapeDtypeStruct((tokens, DA_WIDTH), BF16),
        scratch_shapes=[
            pltpu.VMEM((V7X_MXU_DEPTH, 2 * tq), BF16),
            pltpu.VMEM((1 + tq // tk, tk, 2 * DA_QK_DIM), BF16),
            pltpu.VMEM((2, tk, 2 * tq), F32),
            pltpu.VMEM((2, tk, 2 * tq), BF16),
            pltpu.VMEM((V_EXT_ROWS, 2 * tq), F32),
        ],
        compiler_params=_compiler_params(2),
        name="diff_attn",
    )(lam_params, q_t, k, v_t, subln_col)


def _mlstm_kernel(q_ref, k_ref, v_ref, o_ref, g_t_ref, g_col_ref, ng_ref, out_ref, c_ref, m_ref):
    chunk = q_ref.shape[0]

    @pl.when(pl.program_id(1) == 0)
    def _():
        c_ref[...] = jnp.zeros(c_ref.shape, F32)
        m_ref[...] = jnp.zeros(m_ref.shape, F32)

    t_id = lax.broadcasted_iota(jnp.int32, (chunk, chunk), 0)
    s_id = lax.broadcasted_iota(jnp.int32, (chunk, chunk), 1)
    causal = s_id <= t_id
    lower = causal.astype(F32)
    upper = (t_id <= s_id).astype(F32)

    g_t = g_t_ref[...]
    cum_row = jnp.dot(g_t, upper, preferred_element_type=F32, precision=lax.Precision.HIGHEST)
    cum_col = jnp.dot(lower, g_col_ref[...], preferred_element_type=F32,
                      precision=lax.Precision.HIGHEST)

    ones_col = (lax.broadcasted_iota(jnp.int32, (chunk, ML_DIM), 1) == 0).astype(BF16)

    heads = range(ML_HEADS)
    lanes = [slice(h * ML_DIM, (h + 1) * ML_DIM) for h in heads]
    q = [q_ref[:, lanes[h]] for h in heads]
    k = [k_ref[:, lanes[h]] for h in heads]
    v_ext = [jnp.concatenate([v_ref[:, lanes[h]], ones_col], axis=1) for h in heads]
    c_ext = [c_ref[h] for h in heads]
    m_prev = [m_ref[h] for h in heads]
    b_row = [cum_row[ML_HEADS + h:ML_HEADS + h + 1, :] for h in heads]
    i_row = [g_t[h:h + 1, :] for h in heads]
    b_col = [cum_col[:, ML_HEADS + h:ML_HEADS + h + 1] for h in heads]

    qk = [lax.dot_general(q[h], k[h], _NT, preferred_element_type=F32) for h in heads]
    inter_mm = [jnp.dot(q[h], c_ext[h].astype(BF16), preferred_element_type=F32) for h in heads]

    m_t, w_inter, sc = [], [], []
    for h in heads:
        dmat = jnp.where(causal, (b_col[h] - b_row[h]) + i_row[h], -jnp.inf)
        inter = b_col[h] + m_prev[h]
        m_t.append(jnp.maximum(inter, jnp.max(dmat, axis=1, keepdims=True)))
        w_inter.append(jnp.exp(inter - m_t[h]))
        sc.append((qk[h] * jnp.exp(dmat - m_t[h])).astype(BF16))
    intra_mm = [jnp.dot(sc[h], v_ext[h], preferred_element_type=F32) for h in heads]

    for h in heads:
        b_last = b_row[h][:, chunk - 1:chunk]
        g_row = (b_last - b_row[h]) + i_row[h]
        m_new = jnp.maximum(b_last + m_prev[h], jnp.max(g_row, axis=1, keepdims=True))
        decay = jnp.exp(b_last + m_prev[h] - m_new)
        wk_row = jnp.exp(g_row - m_new)
        k_t_w = (k[h].astype(F32).T * wk_row).astype(BF16)
        c_ref[h] = decay * c_ext[h] + jnp.dot(k_t_w, v_ext[h], preferred_element_type=F32)
        m_ref[h] = m_new

    for h in heads:
        num = intra_mm[h] + w_inter[h] * inter_mm[h]
        nq = num[:, ML_DIM:ML_DIM + 1]
        hid = num[:, 0:ML_DIM] / jnp.maximum(jnp.abs(nq), jnp.exp(-m_t[h]))
        ms = jnp.mean(hid * hid, axis=-1, keepdims=True)
        hn = (hid * lax.rsqrt(ms + NORM_EPS)) * ng_ref[:, lanes[h]]
        og = o_ref[:, lanes[h]].astype(F32)
        out_ref[:, lanes[h]] = (hn * (1.0 / (1.0 + jnp.exp(-og)))).astype(BF16)


def _mlstm(mq, mk, mv, mo, g_t, g_col, norm_g, batch, seq_len):
    tokens = mq.shape[0]
    chunk = ML_CHUNK
    n_chunks = seq_len // chunk
    row_blk = lambda width: pl.BlockSpec((chunk, width), lambda b, c: (b * n_chunks + c, 0))
    return pl.pallas_call(
        _mlstm_kernel,
        grid=(batch, n_chunks),
        in_specs=[
            row_blk(ML_WIDTH), row_blk(ML_WIDTH), row_blk(ML_WIDTH), row_blk(ML_WIDTH),
            pl.BlockSpec((2 * ML_HEADS, chunk), lambda b, c: (0, b * n_chunks + c)),
            row_blk(V7X_LANES),
            _const_spec(norm_g.shape),
        ],
        out_specs=row_blk(ML_WIDTH),
        out_shape=jax.ShapeDtypeStruct((tokens, ML_WIDTH), BF16),
        scratch_shapes=[pltpu.VMEM((ML_HEADS, ML_DIM, 2 * ML_DIM), F32),
                        pltpu.VMEM((ML_HEADS, 1, 1), F32)],
        compiler_params=_compiler_params(2),
        name="mlstm",
    )(mq, mk, mv, mo, g_t, g_col, norm_g)


def _rms(x, g):
    ms = jnp.mean(x * x, axis=-1, keepdims=True)
    return (x * lax.rsqrt(ms + NORM_EPS)) * g


def _out_ffn_kernel(x_ref, attn_ref, ml_ref, w_out_ref, g_ffn_ref, w_gate_ref, w_up_ref, w_down_ref,
                    g_final_ref, out_ref):
    mix = jnp.concatenate([attn_ref[...], ml_ref[...]], axis=1)
    y = x_ref[...] + jnp.dot(mix, w_out_ref[...], preferred_element_type=F32)
    h2 = _rms(y, g_ffn_ref[...]).astype(BF16)
    gate = jnp.dot(h2, w_gate_ref[...], preferred_element_type=F32)
    up = jnp.dot(h2, w_up_ref[...], preferred_element_type=F32)
    act = ((gate * (1.0 / (1.0 + jnp.exp(-gate)))) * up).astype(BF16)
    y2 = y + jnp.dot(act, w_down_ref[...], preferred_element_type=F32)
    out_ref[...] = _rms(y2, g_final_ref[...])


def _out_ffn(x2d, attn, ml, w_out, g_ffn, w_gate, w_up, w_down, g_final):
    tokens = x2d.shape[0]
    rows = FFN_ROWS
    row_blk = lambda width: pl.BlockSpec((rows, width), lambda i: (i, 0))
    return pl.pallas_call(
        _out_ffn_kernel,
        grid=(tokens // rows,),
        in_specs=[
            row_blk(D_MODEL), row_blk(DA_WIDTH), row_blk(ML_WIDTH),
            _const_spec(w_out.shape), _const_spec(g_ffn.shape),
            _const_spec(w_gate.shape), _const_spec(w_up.shape), _const_spec(w_down.shape),
            _const_spec(g_final.shape),
        ],
        out_specs=row_blk(D_MODEL),
        out_shape=jax.ShapeDtypeStruct((tokens, D_MODEL), F32),
        compiler_params=_compiler_params(1),
        name="out_ffn",
    )(x2d, attn, ml, w_out, g_ffn, w_gate, w_up, w_down, g_final)


def kernel(x, positions, mix_norm_g, w_in, da_lambda, da_subln_g, ml_conv_w, ml_conv_b, ml_gate_b,
           ml_norm_g, w_out, ffn_norm_g, w_gate, w_up, w_down, final_norm_g):
    batch, seq_len, _ = x.shape
    tokens = batch * seq_len
    depth = w_in.shape[0]
    assert depth == 1, "one trunk layer"
    assert seq_len % PROJ_ROWS == 0 and seq_len % ATTN_Q == 0 and seq_len % ML_CHUNK == 0
    assert ATTN_Q % ATTN_K == 0 and ATTN_K % CHUNK == 0 and tokens % FFN_ROWS == 0
    assert 2 * DA_QK_DIM + MASK_FEATS <= V7X_MXU_DEPTH and MASK_FEATS <= 2 * DA_QK_DIM

    x2d = x.reshape(tokens, D_MODEL)
    pos3d = positions.reshape(tokens // PROJ_ROWS, 1, PROJ_ROWS)

    w = w_in[0]
    w_t = w[:, OFF_DA_Q:OFF_ML].T.astype(BF16)
    gate_cols = w[:, OFF_GATE:OFF_GATE + 2 * ML_HEADS]
    wg_t = jnp.pad(gate_cols.T, ((0, 2 * ML_HEADS), (0, 0))).astype(BF16)
    w_row = jnp.concatenate(
        [w[:, OFF_ML:OFF_GATE], jnp.pad(gate_cols, ((0, 0), (0, V7X_LANES - 2 * ML_HEADS)))],
        axis=1).astype(BF16)
    inv_freq = (ROPE_THETA ** (-jnp.arange(0, ROT_DIM, 2, dtype=F32) / ROT_DIM)).reshape(ROT_HALF, 1)
    gb = ml_gate_b[0].astype(F32).reshape(2 * ML_HEADS)
    gb_col = gb.reshape(2 * ML_HEADS, 1)
    gb_row = jnp.pad(gb, (0, V7X_LANES - 2 * ML_HEADS)).reshape(1, V7X_LANES)

    q_t, k, v_t, mq, mk, mv, mo, g_t, g_col = _in_proj(
        x2d, pos3d, mix_norm_g[0].reshape(1, D_MODEL).astype(F32), w_t, wg_t, w_row, inv_freq,
        ml_conv_w[0].astype(F32), ml_conv_b[0].reshape(1, 2 * ML_WIDTH).astype(F32),
        gb_col, gb_row, seq_len)

    attn = _diff_attn(da_lambda[0].astype(F32), q_t, k, v_t,
                      da_subln_g[0].astype(F32).reshape(DA_V_DIM, 1), batch, seq_len)
    ml = _mlstm(mq, mk, mv, mo, g_t, g_col, ml_norm_g[0].astype(F32).reshape(1, ML_WIDTH),
                batch, seq_len)

    out = _out_ffn(x2d, attn, ml, w_out[0].astype(BF16),
                   ffn_norm_g[0].reshape(1, D_MODEL).astype(F32),
                   w_gate[0].astype(BF16), w_up[0].astype(BF16), w_down[0].astype(BF16),
                   final_norm_g.reshape(1, D_MODEL).astype(F32))
    return out.reshape(batch, seq_len, D_MODEL)
```

```python
import functools
import math

import jax
import jax.numpy as jnp
from jax import lax
from jax.experimental import pallas as pl
from jax.experimental.pallas import tpu as pltpu

F32 = jnp.float32
BF16 = jnp.bfloat16

D_MODEL = 1024
CHUNK = 64
NORM_EPS = 1e-6
DA_HEADS = 4
DA_QK_DIM = 64
DA_V_DIM = 128
DA_WIDTH = DA_HEADS * DA_V_DIM
ROPE_THETA = 500000.0
ROT_DIM = DA_QK_DIM // 4
ROT_HALF = ROT_DIM // 2
ML_HEADS = 4
ML_DIM = 128
ML_WIDTH = ML_HEADS * ML_DIM
CONV_WIDTH = 4
D_FF = 2816
LAM_INIT = 0.8 - 0.6 * math.exp(-0.3 * 0)
Q_SCALE = DA_QK_DIM ** -0.5 * math.log2(math.e)

OFF_DA_Q = 0
OFF_DA_K = 512
OFF_DA_V = 1024
OFF_ML = 1536
OFF_GATE = 3584
ZT_ML_V = 1536
ZT_ML_O = 2048

V7X_LANES = 128
V7X_SUBLANES = 8
V7X_BF16_ROWS_PER_VREG = 16
V7X_MXU_DEPTH = 256
V7X_VMEM_LIMIT_BYTES = 56 * 1024 * 1024

PROJ_ROWS = 512
ATTN_Q = 1024
ATTN_K = 512
MASK_FEATS = ATTN_Q // CHUNK
MASK_BIG = 1e30
V_EXT_ROWS = DA_V_DIM + V7X_BF16_ROWS_PER_VREG
ML_CHUNK = 256
FFN_ROWS = 512
CONV_HALO = V7X_SUBLANES

_NT = (((1,), (1,)), ((), ()))


def _compiler_params(n_axes):
    return pltpu.CompilerParams(
        dimension_semantics=("arbitrary",) * n_axes,
        vmem_limit_bytes=V7X_VMEM_LIMIT_BYTES,
    )


def _const_spec(shape):
    zeros = (0,) * len(shape)
    return pl.BlockSpec(shape, lambda *_: zeros, pipeline_mode=pl.Buffered(1))


def _rope_rows(zt, cos, sin):
    pieces = []
    for g in range(2):
        base = g * DA_QK_DIM
        x1 = zt[base:base + ROT_HALF]
        x2 = zt[base + ROT_HALF:base + ROT_DIM]
        pieces += [x1 * cos - x2 * sin, x2 * cos + x1 * sin, zt[base + ROT_DIM:base + DA_QK_DIM]]
    return jnp.concatenate(pieces, axis=0)


def _in_proj_kernel(tiles_per_seq,
                    x_ref, pos_ref, g_ref, w_t_ref, wg_t_ref, w_row_ref, invf_ref,
                    convw_ref, convb_ref, gb_col_ref,
                    q_t_ref, k_ref, v_t_ref, mq_t_ref, mk_ref, mv_t_ref, mo_t_ref, g_t_ref, gcum_t_ref,
                    halo_ref):
    rows = x_ref.shape[0]
    x = x_ref[...]
    ms = jnp.mean(x * x, axis=-1, keepdims=True)
    hb = ((x * lax.rsqrt(ms + NORM_EPS)) * g_ref[...]).astype(BF16)

    first = (pl.program_id(0) % tiles_per_seq) == 0
    halo_ref[0:CONV_HALO, :] = jnp.where(first, 0.0, halo_ref[rows:rows + CONV_HALO, :])
    halo_ref[CONV_HALO:CONV_HALO + rows, :] = jnp.dot(hb, w_row_ref[...], preferred_element_type=F32)

    z_t = lax.dot_general(w_t_ref[...], hb, _NT, preferred_element_type=F32)

    xe = halo_ref[...]
    conv = convw_ref[0:1, :] * xe
    for j in range(1, CONV_WIDTH):
        conv = pltpu.roll(conv, 1, axis=0) + convw_ref[j:j + 1, :] * xe
    conv = conv[CONV_HALO:, :] + convb_ref[...]
    act = conv * (1.0 / (1.0 + jnp.exp(-conv)))
    mq_t_ref[...] = (act[:, 0:ML_WIDTH] * (ML_DIM ** -0.5)).T.astype(BF16)
    mk_ref[...] = act[:, ML_WIDTH:2 * ML_WIDTH].astype(BF16)

    ang = invf_ref[...] * pos_ref[0].astype(F32)
    cos = jnp.cos(ang)
    sin = jnp.sin(ang)
    for h in range(DA_HEADS):
        lo = h * 2 * DA_QK_DIM
        hi = lo + 2 * DA_QK_DIM
        q_rot = _rope_rows(z_t[OFF_DA_Q + lo:OFF_DA_Q + hi], cos, sin)
        q_t_ref[lo:hi, :] = (q_rot * Q_SCALE).astype(BF16)
        k_rot = _rope_rows(z_t[OFF_DA_K + lo:OFF_DA_K + hi], cos, sin)
        k_ref[:, lo:hi] = k_rot.T.astype(BF16)
    pad_rows = V_EXT_ROWS - DA_V_DIM
    ones_row = (lax.broadcasted_iota(jnp.int32, (pad_rows, rows), 0) == 0).astype(BF16)
    for out_ref, base in ((v_t_ref, OFF_DA_V), (mv_t_ref, ZT_ML_V)):
        for h in range(DA_HEADS):
            v_lo = base + h * DA_V_DIM
            out_ref[h * V_EXT_ROWS:h * V_EXT_ROWS + DA_V_DIM, :] = z_t[v_lo:v_lo + DA_V_DIM].astype(BF16)
            out_ref[h * V_EXT_ROWS + DA_V_DIM:(h + 1) * V_EXT_ROWS, :] = ones_row
    mo_t_ref[...] = z_t[ZT_ML_O:ZT_ML_O + ML_WIDTH].astype(BF16)

    gz = lax.dot_general(wg_t_ref[...], hb, _NT, preferred_element_type=F32)
    a_t = gz[0:2 * ML_HEADS] + gb_col_ref[...]
    ls_t = jnp.minimum(a_t, 0.0) - jnp.log1p(jnp.exp(-jnp.abs(a_t)))
    row_id = lax.broadcasted_iota(jnp.int32, a_t.shape, 0)
    log_gates = jnp.where(row_id < ML_HEADS, a_t, ls_t)
    g_t_ref[...] = log_gates
    src = lax.broadcasted_iota(jnp.int32, (rows, rows), 0)
    dst = lax.broadcasted_iota(jnp.int32, (rows, rows), 1)
    tri = ((src <= dst) & (src // ML_CHUNK == dst // ML_CHUNK)).astype(F32)
    gcum_t_ref[...] = jnp.dot(log_gates, tri, preferred_element_type=F32,
                              precision=lax.Precision.HIGHEST)


def _in_proj(x2d, pos3d, norm_g, w_t, wg_t, w_row, inv_freq, conv_w, conv_b, gb_col, seq_len):
    tokens = x2d.shape[0]
    rows = PROJ_ROWS
    n_tiles = tokens // rows
    row_blk = lambda width: pl.BlockSpec((rows, width), lambda i: (i, 0))
    col_blk = lambda height: pl.BlockSpec((height, rows), lambda i: (0, i))
    out_shape = (
        jax.ShapeDtypeStruct((DA_WIDTH, tokens), BF16),
        jax.ShapeDtypeStruct((tokens, DA_WIDTH), BF16),
        jax.ShapeDtypeStruct((DA_HEADS * V_EXT_ROWS, tokens), BF16),
        jax.ShapeDtypeStruct((ML_WIDTH, tokens), BF16),
        jax.ShapeDtypeStruct((tokens, ML_WIDTH), BF16),
        jax.ShapeDtypeStruct((ML_HEADS * V_EXT_ROWS, tokens), BF16),
        jax.ShapeDtypeStruct((ML_WIDTH, tokens), BF16),
        jax.ShapeDtypeStruct((2 * ML_HEADS, tokens), F32),
        jax.ShapeDtypeStruct((2 * ML_HEADS, tokens), F32),
    )
    return pl.pallas_call(
        functools.partial(_in_proj_kernel, seq_len // rows),
        grid=(n_tiles,),
        in_specs=[
            row_blk(D_MODEL),
            pl.BlockSpec((1, 1, rows), lambda i: (i, 0, 0)),
            _const_spec(norm_g.shape),
            _const_spec(w_t.shape),
            _const_spec(wg_t.shape),
            _const_spec(w_row.shape),
            _const_spec(inv_freq.shape),
            _const_spec(conv_w.shape),
            _const_spec(conv_b.shape),
            _const_spec(gb_col.shape),
        ],
        out_specs=(
            col_blk(DA_WIDTH), row_blk(DA_WIDTH), col_blk(DA_HEADS * V_EXT_ROWS),
            col_blk(ML_WIDTH), row_blk(ML_WIDTH), col_blk(ML_HEADS * V_EXT_ROWS), col_blk(ML_WIDTH),
            col_blk(2 * ML_HEADS), col_blk(2 * ML_HEADS),
        ),
        out_shape=out_shape,
        scratch_shapes=[pltpu.VMEM((rows + CONV_HALO, 2 * ML_WIDTH), F32)],
        compiler_params=_compiler_params(1),
        name="in_proj",
    )(x2d, pos3d, norm_g, w_t, wg_t, w_row, inv_freq, conv_w, conv_b, gb_col)


def _attn_kernel(lam_ref, q_t_ref, k_ref, v_t_ref, g_ref, o_ref,
                 qm_ref, ind_ref, s_ref, p_ref, acc_ref):
    seq = k_ref.shape[0]
    tq, tk = ATTN_Q, ATTN_K
    n_q = seq // tq
    diag_tiles = tq // tk

    lv = lam_ref[...]
    lam = (jnp.exp(jnp.sum(lv[0:1] * lv[1:2], axis=1, keepdims=True))
           - jnp.exp(jnp.sum(lv[2:3] * lv[3:4], axis=1, keepdims=True)) + LAM_INIT)

    feat = lax.broadcasted_iota(jnp.int32, (MASK_FEATS, 2 * tq), 0)
    qchunk = (lax.broadcasted_iota(jnp.int32, (MASK_FEATS, 2 * tq), 1) % tq) // CHUNK
    qm_ref[...] = jnp.zeros(qm_ref.shape, BF16)
    qm_ref[2 * DA_QK_DIM:2 * DA_QK_DIM + MASK_FEATS, :] = jnp.where(
        feat > qchunk, -MASK_BIG, 0.0).astype(BF16)
    kchunk = lax.broadcasted_iota(jnp.int32, (tk, 2 * DA_QK_DIM), 0) // CHUNK
    lane = lax.broadcasted_iota(jnp.int32, (tk, 2 * DA_QK_DIM), 1)
    ind_ref[0] = jnp.zeros((tk, 2 * DA_QK_DIM), BF16)
    for d in range(diag_tiles):
        ind_ref[d + 1] = (lane == kchunk + d * (tk // CHUNK)).astype(BF16)

    def q_tile(i, _):
        q_off = pl.multiple_of(i * tq, tq)
        n_t = diag_tiles * (i + 1)

        def score(par, t):
            k_t = k_ref[pl.ds(pl.multiple_of(t * tk, tk), tk), :]
            which = jnp.maximum(t - (n_t - diag_tiles) + 1, 0)
            k_ext = jnp.concatenate([k_t, ind_ref[which]], axis=1)
            s_ref[par] = jnp.dot(k_ext, qm_ref[...], preferred_element_type=F32)

        def softmax(par, maxes):
            new, alphas = [], []
            for mi in range(2):
                lanes = slice(mi * tq, (mi + 1) * tq)
                s = s_ref[par, :, lanes]
                m_new = jnp.maximum(maxes[mi], jnp.max(s, axis=0, keepdims=True))
                p_ref[par, :, lanes] = jnp.exp2(s - m_new).astype(BF16)
                alphas.append(jnp.exp2(maxes[mi] - m_new))
                new.append(m_new)
            return tuple(new), tuple(alphas)

        def accumulate(par, t, alphas):
            v_t = v_t_ref[:, pl.ds(pl.multiple_of(t * tk, tk), tk)]
            pv = jnp.dot(v_t, p_ref[par], preferred_element_type=F32)
            for mi in range(2):
                lanes = slice(mi * tq, (mi + 1) * tq)
                acc_ref[:, lanes] = alphas[mi] * acc_ref[:, lanes] + pv[:, lanes]

        def step(par, t, carry, last=False):
            maxes, alphas = carry
            if not last:
                score(1 - par, t + 1)
            accumulate(1 - par, jnp.maximum(t - 1, 0), alphas)
            return softmax(par, maxes)

        q_t = q_t_ref[:, pl.ds(q_off, tq)]
        qm_ref[0:DA_QK_DIM, 0:tq] = q_t[0:DA_QK_DIM]
        qm_ref[DA_QK_DIM:2 * DA_QK_DIM, tq:2 * tq] = q_t[DA_QK_DIM:]
        acc_ref[...] = jnp.zeros(acc_ref.shape, F32)
        p_ref[1] = jnp.zeros(p_ref.shape[1:], BF16)
        score(0, 0)

        neg = jnp.full((1, tq), -jnp.inf, F32)
        one = jnp.ones((1, tq), F32)

        def pair(u, carry):
            carry = step(0, 2 * u, carry)
            return step(1, 2 * u + 1, carry)

        carry = lax.fori_loop(0, n_t // 2 - 1, pair, ((neg, neg), (one, one)))
        carry = step(0, n_t - 2, carry)
        _, alphas = step(1, n_t - 1, carry, last=True)
        accumulate(1, n_t - 1, alphas)

        acc = acc_ref[...]
        o1 = acc[0:DA_V_DIM, 0:tq] / acc[DA_V_DIM:DA_V_DIM + 1, 0:tq]
        o2 = acc[0:DA_V_DIM, tq:2 * tq] / acc[DA_V_DIM:DA_V_DIM + 1, tq:2 * tq]
        o_t = o1 - lam * o2
        ms = jnp.mean(o_t * o_t, axis=0, keepdims=True)
        y_t = (o_t * lax.rsqrt(ms + NORM_EPS)) * g_ref[...] * (1.0 - LAM_INIT)
        o_ref[pl.ds(q_off, tq), :] = y_t.T.astype(BF16)
        return 0

    lax.fori_loop(0, n_q, q_tile, 0)


def _diff_attn(lam_params, q_t, k, v_t, subln_col, batch, seq_len):
    tokens = k.shape[0]
    tq, tk = ATTN_Q, ATTN_K
    return pl.pallas_call(
        _attn_kernel,
        grid=(batch, DA_HEADS),
        in_specs=[
            _const_spec(lam_params.shape),
            pl.BlockSpec((DA_V_DIM, seq_len), lambda b, h: (h, b)),
            pl.BlockSpec((seq_len, DA_V_DIM), lambda b, h: (b, h)),
            pl.BlockSpec((V_EXT_ROWS, seq_len), lambda b, h: (h, b)),
            _const_spec(subln_col.shape),
        ],
        out_specs=pl.BlockSpec((seq_len, DA_V_DIM), lambda b, h: (b, h)),
        out_shape=jax.ShapeDtypeStruct((tokens, DA_WIDTH), BF16),
        scratch_shapes=[
            pltpu.VMEM((V7X_MXU_DEPTH, 2 * tq), BF16),
            pltpu.VMEM((1 + tq // tk, tk, 2 * DA_QK_DIM), BF16),
            pltpu.VMEM((2, tk, 2 * tq), F32),
            pltpu.VMEM((2, tk, 2 * tq), BF16),
            pltpu.VMEM((V_EXT_ROWS, 2 * tq), F32),
        ],
        compiler_params=_compiler_params(2),
        name="diff_attn",
    )(lam_params, q_t, k, v_t, subln_col)


def _mlstm_kernel(q_t_ref, k_ref, v_t_ref, o_t_ref, g_t_ref, gcum_t_ref, ng_ref, out_ref,
                  c_ref, m_ref):
    chunk = k_ref.shape[0]

    @pl.when(pl.program_id(1) == 0)
    def _():
        c_ref[...] = jnp.zeros(c_ref.shape, F32)
        m_ref[...] = jnp.zeros(m_ref.shape, F32)

    s_id = lax.broadcasted_iota(jnp.int32, (chunk, chunk), 0)
    t_id = lax.broadcasted_iota(jnp.int32, (chunk, chunk), 1)
    causal = s_id <= t_id

    g_t = g_t_ref[...]
    cum_row = gcum_t_ref[...]
    key_rows = g_t - pltpu.roll(cum_row, ML_HEADS, axis=0)
    key_cols = jnp.concatenate(
        [key_rows, jnp.zeros((V7X_LANES - 2 * ML_HEADS, chunk), F32)], axis=0).T

    heads = range(ML_HEADS)
    q_t = [q_t_ref[h * ML_DIM:(h + 1) * ML_DIM, :] for h in heads]
    k = [k_ref[:, h * ML_DIM:(h + 1) * ML_DIM] for h in heads]
    v_t = [v_t_ref[h * V_EXT_ROWS:(h + 1) * V_EXT_ROWS, :] for h in heads]
    c_t = [c_ref[h] for h in heads]
    m_prev = [m_ref[h] for h in heads]
    b_row = [cum_row[ML_HEADS + h:ML_HEADS + h + 1, :] for h in heads]
    i_row = [g_t[h:h + 1, :] for h in heads]
    e_mat = [jnp.broadcast_to(key_cols[:, h:h + 1], (chunk, chunk)) for h in heads]

    kq = [jnp.dot(k[h], q_t[h], preferred_element_type=F32) for h in heads]
    inter_mm = [jnp.dot(c_t[h].astype(BF16), q_t[h], preferred_element_type=F32) for h in heads]

    m_row, w_inter, sc = [], [], []
    for h in heads:
        d_mat = jnp.where(causal, e_mat[h] + b_row[h], -jnp.inf)
        inter = b_row[h] + m_prev[h]
        m_row.append(jnp.maximum(inter, jnp.max(d_mat, axis=0, keepdims=True)))
        w_inter.append(jnp.exp(inter - m_row[h]))
        sc.append((kq[h] * jnp.exp(d_mat - m_row[h])).astype(BF16))
    intra_mm = [jnp.dot(v_t[h], sc[h], preferred_element_type=F32) for h in heads]

    for h in heads:
        b_last = b_row[h][:, chunk - 1:chunk]
        g_row = (b_last - b_row[h]) + i_row[h]
        m_new = jnp.maximum(b_last + m_prev[h], jnp.max(g_row, axis=1, keepdims=True))
        decay = jnp.exp(b_last + m_prev[h] - m_new)
        wk = jnp.exp(e_mat[h][:, 0:ML_DIM] + (b_last - m_new))
        kw = (k[h].astype(F32) * wk).astype(BF16)
        c_ref[h] = decay * c_t[h] + jnp.dot(v_t[h], kw, preferred_element_type=F32)
        m_ref[h] = m_new

    for h in heads:
        num = intra_mm[h] + w_inter[h] * inter_mm[h]
        nq = num[ML_DIM:ML_DIM + 1, :]
        hid = num[0:ML_DIM, :] / jnp.maximum(jnp.abs(nq), jnp.exp(-m_row[h]))
        ms = jnp.mean(hid * hid, axis=0, keepdims=True)
        hn = (hid * lax.rsqrt(ms + NORM_EPS)) * ng_ref[h * ML_DIM:(h + 1) * ML_DIM, :]
        og = o_t_ref[h * ML_DIM:(h + 1) * ML_DIM, :].astype(F32)
        gated = hn * (1.0 / (1.0 + jnp.exp(-og)))
        out_ref[:, h * ML_DIM:(h + 1) * ML_DIM] = gated.T.astype(BF16)


def _mlstm(mq_t, mk, mv_t, mo_t, g_t, gcum_t, gain, batch, seq_len):
    tokens = mk.shape[0]
    chunk = ML_CHUNK
    n_chunks = seq_len // chunk
    row_blk = lambda width: pl.BlockSpec((chunk, width), lambda b, c: (b * n_chunks + c, 0))
    col_blk = lambda height: pl.BlockSpec((height, chunk), lambda b, c: (0, b * n_chunks + c))
    return pl.pallas_call(
        _mlstm_kernel,
        grid=(batch, n_chunks),
        in_specs=[
            col_blk(ML_WIDTH), row_blk(ML_WIDTH), col_blk(ML_HEADS * V_EXT_ROWS), col_blk(ML_WIDTH),
            col_blk(2 * ML_HEADS), col_blk(2 * ML_HEADS),
            _const_spec(gain.shape),
        ],
        out_specs=row_blk(ML_WIDTH),
        out_shape=jax.ShapeDtypeStruct((tokens, ML_WIDTH), BF16),
        scratch_shapes=[pltpu.VMEM((ML_HEADS, V_EXT_ROWS, ML_DIM), F32),
                        pltpu.VMEM((ML_HEADS, 1, 1), F32)],
        compiler_params=_compiler_params(2),
        name="mlstm",
    )(mq_t, mk, mv_t, mo_t, g_t, gcum_t, gain)


def _rms(x, g):
    ms = jnp.mean(x * x, axis=-1, keepdims=True)
    return (x * lax.rsqrt(ms + NORM_EPS)) * g


def _out_ffn_kernel(x_ref, attn_ref, ml_ref, w_out_ref, g_ffn_ref, w_gate_ref, w_up_ref, w_down_ref,
                    g_final_ref, out_ref):
    mix = jnp.concatenate([attn_ref[...], ml_ref[...]], axis=1)
    y = x_ref[...] + jnp.dot(mix, w_out_ref[...], preferred_element_type=F32)
    h2 = _rms(y, g_ffn_ref[...]).astype(BF16)
    gate = jnp.dot(h2, w_gate_ref[...], preferred_element_type=F32)
    up = jnp.dot(h2, w_up_ref[...], preferred_element_type=F32)
    act = ((gate * (1.0 / (1.0 + jnp.exp(-gate)))) * up).astype(BF16)
    y2 = y + jnp.dot(act, w_down_ref[...], preferred_element_type=F32)
    out_ref[...] = _rms(y2, g_final_ref[...])


def _out_ffn(x2d, attn, ml, w_out, g_ffn, w_gate, w_up, w_down, g_final):
    tokens = x2d.shape[0]
    rows = FFN_ROWS
    row_blk = lambda width: pl.BlockSpec((rows, width), lambda i: (i, 0))
    return pl.pallas_call(
        _out_ffn_kernel,
        grid=(tokens // rows,),
        in_specs=[
            row_blk(D_MODEL), row_blk(DA_WIDTH), row_blk(ML_WIDTH),
            _const_spec(w_out.shape), _const_spec(g_ffn.shape),
            _const_spec(w_gate.shape), _const_spec(w_up.shape), _const_spec(w_down.shape),
            _const_spec(g_final.shape),
        ],
        out_specs=row_blk(D_MODEL),
        out_shape=jax.ShapeDtypeStruct((tokens, D_MODEL), F32),
        compiler_params=_compiler_params(1),
        name="out_ffn",
    )(x2d, attn, ml, w_out, g_ffn, w_gate, w_up, w_down, g_final)


def kernel(x, positions, mix_norm_g, w_in, da_lambda, da_subln_g, ml_conv_w, ml_conv_b, ml_gate_b,
           ml_norm_g, w_out, ffn_norm_g, w_gate, w_up, w_down, final_norm_g):
    batch, seq_len, _ = x.shape
    tokens = batch * seq_len
    depth = w_in.shape[0]
    assert depth == 1, "one trunk layer"
    assert seq_len % PROJ_ROWS == 0 and seq_len % ATTN_Q == 0 and seq_len % ML_CHUNK == 0
    assert ATTN_Q % ATTN_K == 0 and ATTN_K % CHUNK == 0 and tokens % FFN_ROWS == 0
    assert 2 * DA_QK_DIM + MASK_FEATS <= V7X_MXU_DEPTH and MASK_FEATS <= 2 * DA_QK_DIM
    assert DA_V_DIM == ML_DIM and DA_HEADS == ML_HEADS

    x2d = x.reshape(tokens, D_MODEL)
    pos3d = positions.reshape(tokens // PROJ_ROWS, 1, PROJ_ROWS)

    w = w_in[0]
    ml_v = OFF_ML + 2 * ML_WIDTH
    w_t = jnp.concatenate([w[:, OFF_DA_Q:OFF_ML], w[:, ml_v:OFF_GATE]], axis=1).T.astype(BF16)
    gate_cols = w[:, OFF_GATE:OFF_GATE + 2 * ML_HEADS]
    wg_t = jnp.pad(gate_cols.T, ((0, 2 * ML_HEADS), (0, 0))).astype(BF16)
    w_row = w[:, OFF_ML:ml_v].astype(BF16)
    inv_freq = (ROPE_THETA ** (-jnp.arange(0, ROT_DIM, 2, dtype=F32) / ROT_DIM)).reshape(ROT_HALF, 1)
    gb_col = ml_gate_b[0].astype(F32).reshape(2 * ML_HEADS, 1)

    q_t, k, v_t, mq_t, mk, mv_t, mo_t, g_t, gcum_t = _in_proj(
        x2d, pos3d, mix_norm_g[0].reshape(1, D_MODEL).astype(F32), w_t, wg_t, w_row, inv_freq,
        ml_conv_w[0].astype(F32), ml_conv_b[0].reshape(1, 2 * ML_WIDTH).astype(F32),
        gb_col, seq_len)

    attn = _diff_attn(da_lambda[0].astype(F32), q_t, k, v_t,
                      da_subln_g[0].astype(F32).reshape(DA_V_DIM, 1), batch, seq_len)
    ml_gain = jnp.broadcast_to(ml_norm_g[0].astype(F32).reshape(ML_WIDTH, 1), (ML_WIDTH, ML_CHUNK))
    ml = _mlstm(mq_t, mk, mv_t, mo_t, g_t, gcum_t, ml_gain, batch, seq_len)

    out = _out_ffn(x2d, attn, ml, w_out[0].astype(BF16),
                   ffn_norm_g[0].reshape(1, D_MODEL).astype(F32),
                   w_gate[0].astype(BF16), w_up[0].astype(BF16), w_down[0].astype(BF16),
                   final_norm_g.reshape(1, D_MODEL).astype(F32))
    return out.reshape(batch, seq_len, D_MODEL)
```

```python
import functools
import math

import jax
import jax.numpy as jnp
from jax import lax
from jax.experimental import pallas as pl
from jax.experimental.pallas import tpu as pltpu

F32 = jnp.float32
BF16 = jnp.bfloat16

D_MODEL = 1024
CHUNK = 64
NORM_EPS = 1e-6
DA_HEADS = 4
DA_QK_DIM = 64
DA_V_DIM = 128
DA_WIDTH = DA_HEADS * DA_V_DIM
ROPE_THETA = 500000.0
ROT_DIM = DA_QK_DIM // 4
ROT_HALF = ROT_DIM // 2
ML_HEADS = 4
ML_DIM = 128
ML_WIDTH = ML_HEADS * ML_DIM
CONV_WIDTH = 4
D_FF = 2816
LAM_INIT = 0.8 - 0.6 * math.exp(-0.3 * 0)
Q_SCALE = DA_QK_DIM ** -0.5 * math.log2(math.e)

OFF_DA_Q = 0
OFF_DA_K = 512
OFF_DA_V = 1024
OFF_ML = 1536
OFF_GATE = 3584
ZT_ML_V = 1536
ZT_ML_O = 2048

V7X_LANES = 128
V7X_SUBLANES = 8
V7X_BF16_ROWS_PER_VREG = 16
V7X_MXU_DEPTH = 256
V7X_VMEM_LIMIT_BYTES = 56 * 1024 * 1024

PROJ_ROWS = 512
ATTN_Q = 1024
ATTN_K = 512
MASK_FEATS = ATTN_Q // CHUNK
MASK_BIG = 1e30
V_EXT_ROWS = DA_V_DIM + V7X_BF16_ROWS_PER_VREG
ML_CHUNK = 256
FFN_ROWS = 512
CONV_HALO = V7X_SUBLANES

_NT = (((1,), (1,)), ((), ()))


def _compiler_params(n_axes):
    return pltpu.CompilerParams(
        dimension_semantics=("arbitrary",) * n_axes,
        vmem_limit_bytes=V7X_VMEM_LIMIT_BYTES,
    )


def _const_spec(shape):
    zeros = (0,) * len(shape)
    return pl.BlockSpec(shape, lambda *_: zeros, pipeline_mode=pl.Buffered(1))


def _rope_rows(zt, cos, sin):
    pieces = []
    for g in range(2):
        base = g * DA_QK_DIM
        x1 = zt[base:base + ROT_HALF]
        x2 = zt[base + ROT_HALF:base + ROT_DIM]
        pieces += [x1 * cos - x2 * sin, x2 * cos + x1 * sin, zt[base + ROT_DIM:base + DA_QK_DIM]]
    return jnp.concatenate(pieces, axis=0)


def _in_proj_kernel(tiles_per_seq,
                    x_ref, pos_ref, g_ref, w_t_ref, wg_t_ref, w_row_ref, invf_ref,
                    convw_ref, convb_ref, gb_col_ref,
                    q_t_ref, k_ref, v_t_ref, mq_t_ref, mk_ref, mv_t_ref, mo_t_ref, g_t_ref, gcum_t_ref,
                    halo_ref):
    rows = x_ref.shape[0]
    x = x_ref[...]
    ms = jnp.mean(x * x, axis=-1, keepdims=True)
    hb = ((x * lax.rsqrt(ms + NORM_EPS)) * g_ref[...]).astype(BF16)

    first = (pl.program_id(0) % tiles_per_seq) == 0
    halo_ref[0:CONV_HALO, :] = jnp.where(first, 0.0, halo_ref[rows:rows + CONV_HALO, :])
    halo_ref[CONV_HALO:CONV_HALO + rows, :] = jnp.dot(hb, w_row_ref[...], preferred_element_type=F32)

    def proj_t(lo, hi):
        return lax.dot_general(w_t_ref[lo:hi, :], hb, _NT, preferred_element_type=F32)

    zqk_t = proj_t(OFF_DA_Q, OFF_DA_V)

    xe = halo_ref[...]
    conv = convw_ref[0:1, :] * xe
    for j in range(1, CONV_WIDTH):
        conv = pltpu.roll(conv, 1, axis=0) + convw_ref[j:j + 1, :] * xe
    conv = conv[CONV_HALO:, :] + convb_ref[...]
    act = conv * (1.0 / (1.0 + jnp.exp(-conv)))
    mq_t_ref[...] = (act[:, 0:ML_WIDTH] * (ML_DIM ** -0.5)).T.astype(BF16)
    mk_ref[...] = act[:, ML_WIDTH:2 * ML_WIDTH].astype(BF16)

    ang = invf_ref[...] * pos_ref[0].astype(F32)
    cos = jnp.cos(ang)
    sin = jnp.sin(ang)
    for h in range(DA_HEADS):
        lo = h * 2 * DA_QK_DIM
        hi = lo + 2 * DA_QK_DIM
        q_rot = _rope_rows(zqk_t[OFF_DA_Q + lo:OFF_DA_Q + hi], cos, sin)
        q_t_ref[lo:hi, :] = (q_rot * Q_SCALE).astype(BF16)
        k_rot = _rope_rows(zqk_t[OFF_DA_K + lo:OFF_DA_K + hi], cos, sin)
        k_ref[:, lo:hi] = k_rot.T.astype(BF16)
    pad_rows = V_EXT_ROWS - DA_V_DIM
    ones_row = (lax.broadcasted_iota(jnp.int32, (pad_rows, rows), 0) == 0).astype(BF16)
    for out_ref, base in ((v_t_ref, OFF_DA_V), (mv_t_ref, ZT_ML_V)):
        zv_t = proj_t(base, base + DA_WIDTH)
        for h in range(DA_HEADS):
            v_lo = h * DA_V_DIM
            out_ref[h * V_EXT_ROWS:h * V_EXT_ROWS + DA_V_DIM, :] = zv_t[v_lo:v_lo + DA_V_DIM].astype(BF16)
            out_ref[h * V_EXT_ROWS + DA_V_DIM:(h + 1) * V_EXT_ROWS, :] = ones_row
    mo_t_ref[...] = proj_t(ZT_ML_O, ZT_ML_O + ML_WIDTH).astype(BF16)

    gz = lax.dot_general(wg_t_ref[...], hb, _NT, preferred_element_type=F32)
    a_t = gz[0:2 * ML_HEADS] + gb_col_ref[...]
    ls_t = jnp.minimum(a_t, 0.0) - jnp.log1p(jnp.exp(-jnp.abs(a_t)))
    row_id = lax.broadcasted_iota(jnp.int32, a_t.shape, 0)
    log_gates = jnp.where(row_id < ML_HEADS, a_t, ls_t)
    g_t_ref[...] = log_gates
    src = lax.broadcasted_iota(jnp.int32, (rows, rows), 0)
    dst = lax.broadcasted_iota(jnp.int32, (rows, rows), 1)
    tri = ((src <= dst) & (src // ML_CHUNK == dst // ML_CHUNK)).astype(F32)
    gcum_t_ref[...] = jnp.dot(log_gates, tri, preferred_element_type=F32,
                              precision=lax.Precision.HIGHEST)


def _in_proj(x2d, pos3d, norm_g, w_t, wg_t, w_row, inv_freq, conv_w, conv_b, gb_col, seq_len):
    tokens = x2d.shape[0]
    rows = PROJ_ROWS
    n_tiles = tokens // rows
    row_blk = lambda width: pl.BlockSpec((rows, width), lambda i: (i, 0))
    col_blk = lambda height: pl.BlockSpec((height, rows), lambda i: (0, i))
    out_shape = (
        jax.ShapeDtypeStruct((DA_WIDTH, tokens), BF16),
        jax.ShapeDtypeStruct((tokens, DA_WIDTH), BF16),
        jax.ShapeDtypeStruct((DA_HEADS * V_EXT_ROWS, tokens), BF16),
        jax.ShapeDtypeStruct((ML_WIDTH, tokens), BF16),
        jax.ShapeDtypeStruct((tokens, ML_WIDTH), BF16),
        jax.ShapeDtypeStruct((ML_HEADS * V_EXT_ROWS, tokens), BF16),
        jax.ShapeDtypeStruct((ML_WIDTH, tokens), BF16),
        jax.ShapeDtypeStruct((2 * ML_HEADS, tokens), F32),
        jax.ShapeDtypeStruct((2 * ML_HEADS, tokens), F32),
    )
    return pl.pallas_call(
        functools.partial(_in_proj_kernel, seq_len // rows),
        grid=(n_tiles,),
        in_specs=[
            row_blk(D_MODEL),
            pl.BlockSpec((1, 1, rows), lambda i: (i, 0, 0)),
            _const_spec(norm_g.shape),
            _const_spec(w_t.shape),
            _const_spec(wg_t.shape),
            _const_spec(w_row.shape),
            _const_spec(inv_freq.shape),
            _const_spec(conv_w.shape),
            _const_spec(conv_b.shape),
            _const_spec(gb_col.shape),
        ],
        out_specs=(
            col_blk(DA_WIDTH), row_blk(DA_WIDTH), col_blk(DA_HEADS * V_EXT_ROWS),
            col_blk(ML_WIDTH), row_blk(ML_WIDTH), col_blk(ML_HEADS * V_EXT_ROWS), col_blk(ML_WIDTH),
            col_blk(2 * ML_HEADS), col_blk(2 * ML_HEADS),
        ),
        out_shape=out_shape,
        scratch_shapes=[pltpu.VMEM((rows + CONV_HALO, 2 * ML_WIDTH), F32)],
        compiler_params=_compiler_params(1),
        name="in_proj",
    )(x2d, pos3d, norm_g, w_t, wg_t, w_row, inv_freq, conv_w, conv_b, gb_col)


def _attn_kernel(lam_ref, q_t_ref, k_ref, v_t_ref, g_ref, o_ref,
                 qm_ref, ind_ref, s_ref, p_ref, acc_ref):
    seq = k_ref.shape[0]
    tq, tk = ATTN_Q, ATTN_K
    n_q = seq // tq
    diag_tiles = tq // tk

    lv = lam_ref[...]
    lam = (jnp.exp(jnp.sum(lv[0:1] * lv[1:2], axis=1, keepdims=True))
           - jnp.exp(jnp.sum(lv[2:3] * lv[3:4], axis=1, keepdims=True)) + LAM_INIT)

    feat = lax.broadcasted_iota(jnp.int32, (MASK_FEATS, 2 * tq), 0)
    qchunk = (lax.broadcasted_iota(jnp.int32, (MASK_FEATS, 2 * tq), 1) % tq) // CHUNK
    qm_ref[...] = jnp.zeros(qm_ref.shape, BF16)
    qm_ref[2 * DA_QK_DIM:2 * DA_QK_DIM + MASK_FEATS, :] = jnp.where(
        feat > qchunk, -MASK_BIG, 0.0).astype(BF16)
    kchunk = lax.broadcasted_iota(jnp.int32, (tk, 2 * DA_QK_DIM), 0) // CHUNK
    lane = lax.broadcasted_iota(jnp.int32, (tk, 2 * DA_QK_DIM), 1)
    ind_ref[0] = jnp.zeros((tk, 2 * DA_QK_DIM), BF16)
    for d in range(diag_tiles):
        ind_ref[d + 1] = (lane == kchunk + d * (tk // CHUNK)).astype(BF16)

    def q_tile(i, first_tile):
        q_off = pl.multiple_of(i * tq, tq)
        n_t = diag_tiles * (i + 1)

        def score(par, t):
            k_t = k_ref[pl.ds(pl.multiple_of(t * tk, tk), tk), :]
            which = jnp.maximum(t - (n_t - diag_tiles) + 1, 0)
            k_ext = jnp.concatenate([k_t, ind_ref[which]], axis=1)
            s_ref[par] = jnp.dot(k_ext, qm_ref[...], preferred_element_type=F32)

        def softmax(par, maxes):
            new, alphas = [], []
            for mi in range(2):
                lanes = slice(mi * tq, (mi + 1) * tq)
                s = s_ref[par, :, lanes]
                m_new = jnp.maximum(maxes[mi], jnp.max(s, axis=0, keepdims=True))
                p_ref[par, :, lanes] = jnp.exp2(s - m_new).astype(BF16)
                alphas.append(jnp.exp2(maxes[mi] - m_new))
                new.append(m_new)
            return tuple(new), tuple(alphas)

        def accumulate(par, t, alphas):
            v_t = v_t_ref[:, pl.ds(pl.multiple_of(t * tk, tk), tk)]
            pv = jnp.dot(v_t, p_ref[par], preferred_element_type=F32)
            for mi in range(2):
                lanes = slice(mi * tq, (mi + 1) * tq)
                acc_ref[:, lanes] = alphas[mi] * acc_ref[:, lanes] + pv[:, lanes]

        hq = tq // 2

        def score_late(par, t):
            k_t = k_ref[pl.ds(pl.multiple_of(t * tk, tk), tk), :]
            k_ext = jnp.concatenate([k_t, ind_ref[diag_tiles]], axis=1)
            q_late = jnp.concatenate([qm_ref[:, hq:tq], qm_ref[:, tq + hq:2 * tq]], axis=1)
            s_ref[par, :, 0:tq] = jnp.dot(k_ext, q_late, preferred_element_type=F32)

        def softmax_late(par, maxes):
            alphas = []
            for mi in range(2):
                lanes = slice(mi * hq, (mi + 1) * hq)
                s = s_ref[par, :, lanes]
                m_old = maxes[mi][:, hq:tq]
                m_new = jnp.maximum(m_old, jnp.max(s, axis=0, keepdims=True))
                p_ref[par, :, lanes] = jnp.exp2(s - m_new).astype(BF16)
                alphas.append(jnp.exp2(m_old - m_new))
            return tuple(alphas)

        def accumulate_late(par, t, alphas):
            v_t = v_t_ref[:, pl.ds(pl.multiple_of(t * tk, tk), tk)]
            pv = jnp.dot(v_t, p_ref[par, :, 0:tq], preferred_element_type=F32)
            for mi in range(2):
                lanes = slice(mi * tq + hq, (mi + 1) * tq)
                acc_ref[:, lanes] = alphas[mi] * acc_ref[:, lanes] + pv[:, mi * hq:(mi + 1) * hq]

        def step(par, t, carry, first=False, late_next=False):
            maxes, alphas = carry
            (score_late if late_next else score)(1 - par, t + 1)
            if not first:
                accumulate(1 - par, t - 1, alphas)
            return softmax(par, maxes)

        q_t = q_t_ref[:, pl.ds(q_off, tq)]
        qm_ref[0:DA_QK_DIM, 0:tq] = q_t[0:DA_QK_DIM]
        qm_ref[DA_QK_DIM:2 * DA_QK_DIM, tq:2 * tq] = q_t[DA_QK_DIM:]
        acc_ref[...] = jnp.zeros(acc_ref.shape, F32)
        score(0, 0)

        neg = jnp.full((1, tq), -jnp.inf, F32)
        carry = ((neg, neg), None)
        if first_tile:
            carry = step(0, 0, carry, first=True, late_next=True)
        else:
            carry = step(0, 0, carry, first=True)

            def pair(u, carry):
                carry = step(1, 2 * u + 1, carry)
                return step(0, 2 * u + 2, carry)

            carry = lax.fori_loop(0, n_t // 2 - 2, pair, carry)
            carry = step(1, n_t - 3, carry)
            carry = step(0, n_t - 2, carry, late_next=True)
        maxes, alphas = carry
        accumulate(0, n_t - 2, alphas)
        accumulate_late(1, n_t - 1, softmax_late(1, maxes))

        acc = acc_ref[...]
        o1 = acc[0:DA_V_DIM, 0:tq] / acc[DA_V_DIM:DA_V_DIM + 1, 0:tq]
        o2 = acc[0:DA_V_DIM, tq:2 * tq] / acc[DA_V_DIM:DA_V_DIM + 1, tq:2 * tq]
        o_t = o1 - lam * o2
        ms = jnp.mean(o_t * o_t, axis=0, keepdims=True)
        y_t = (o_t * lax.rsqrt(ms + NORM_EPS)) * g_ref[...] * (1.0 - LAM_INIT)
        o_ref[pl.ds(q_off, tq), :] = y_t.T.astype(BF16)
        return 0

    q_tile(jnp.int32(0), True)
    lax.fori_loop(1, n_q, lambda i, c: q_tile(i, False), 0)


def _diff_attn(lam_params, q_t, k, v_t, subln_col, batch, seq_len):
    tokens = k.shape[0]
    tq, tk = ATTN_Q, ATTN_K
    return pl.pallas_call(
        _attn_kernel,
        grid=(batch, DA_HEADS),
        in_specs=[
            _const_spec(lam_params.shape),
            pl.BlockSpec((DA_V_DIM, seq_len), lambda b, h: (h, b)),
            pl.BlockSpec((seq_len, DA_V_DIM), lambda b, h: (b, h)),
            pl.BlockSpec((V_EXT_ROWS, seq_len), lambda b, h: (h, b)),
            _const_spec(subln_col.shape),
        ],
        out_specs=pl.BlockSpec((seq_len, DA_V_DIM), lambda b, h: (b, h)),
        out_shape=jax.ShapeDtypeStruct((tokens, DA_WIDTH), BF16),
        scratch_shapes=[
            pltpu.VMEM((V7X_MXU_DEPTH, 2 * tq), BF16),
            pltpu.VMEM((1 + tq // tk, tk, 2 * DA_QK_DIM), BF16),
            pltpu.VMEM((2, tk, 2 * tq), F32),
            pltpu.VMEM((2, tk, 2 * tq), BF16),
            pltpu.VMEM((V_EXT_ROWS, 2 * tq), F32),
        ],
        compiler_params=_compiler_params(2),
        name="diff_attn",
    )(lam_params, q_t, k, v_t, subln_col)


def _mlstm_kernel(q_t_ref, k_ref, v_t_ref, o_t_ref, g_t_ref, gcum_t_ref, ng_ref, out_ref,
                  c_ref, m_ref):
    chunk = k_ref.shape[0]

    @pl.when(pl.program_id(1) == 0)
    def _():
        c_ref[...] = jnp.zeros(c_ref.shape, F32)
        m_ref[...] = jnp.zeros(m_ref.shape, F32)

    s_id = lax.broadcasted_iota(jnp.int32, (chunk, chunk), 0)
    t_id = lax.broadcasted_iota(jnp.int32, (chunk, chunk), 1)
    causal = s_id <= t_id

    g_t = g_t_ref[...]
    cum_row = gcum_t_ref[...]
    key_rows = g_t - pltpu.roll(cum_row, ML_HEADS, axis=0)
    key_cols = jnp.concatenate(
        [key_rows, jnp.zeros((V7X_LANES - 2 * ML_HEADS, chunk), F32)], axis=0).T

    heads = range(ML_HEADS)
    q_t = [q_t_ref[h * ML_DIM:(h + 1) * ML_DIM, :] for h in heads]
    k = [k_ref[:, h * ML_DIM:(h + 1) * ML_DIM] for h in heads]
    v_t = [v_t_ref[h * V_EXT_ROWS:(h + 1) * V_EXT_ROWS, :] for h in heads]
    c_t = [c_ref[h] for h in heads]
    m_prev = [m_ref[h] for h in heads]
    b_row = [cum_row[ML_HEADS + h:ML_HEADS + h + 1, :] for h in heads]
    i_row = [g_t[h:h + 1, :] for h in heads]
    e_mat = [jnp.broadcast_to(key_cols[:, h:h + 1], (chunk, chunk)) for h in heads]

    kq = [jnp.dot(k[h], q_t[h], preferred_element_type=F32) for h in heads]
    inter_mm = [jnp.dot(c_t[h].astype(BF16), q_t[h], preferred_element_type=F32) for h in heads]

    m_row, w_inter, sc = [], [], []
    for h in heads:
        d_mat = jnp.where(causal, e_mat[h] + b_row[h], -jnp.inf)
        inter = b_row[h] + m_prev[h]
        m_row.append(jnp.maximum(inter, jnp.max(d_mat, axis=0, keepdims=True)))
        w_inter.append(jnp.exp(inter - m_row[h]))
        sc.append((kq[h] * jnp.exp(d_mat - m_row[h])).astype(BF16))
    intra_mm = [jnp.dot(v_t[h], sc[h], preferred_element_type=F32) for h in heads]

    for h in heads:
        b_last = b_row[h][:, chunk - 1:chunk]
        g_row = (b_last - b_row[h]) + i_row[h]
        m_new = jnp.maximum(b_last + m_prev[h], jnp.max(g_row, axis=1, keepdims=True))
        decay = jnp.exp(b_last + m_prev[h] - m_new)
        wk = jnp.exp(e_mat[h][:, 0:ML_DIM] + (b_last - m_new))
        kw = (k[h].astype(F32) * wk).astype(BF16)
        c_ref[h] = decay * c_t[h] + jnp.dot(v_t[h], kw, preferred_element_type=F32)
        m_ref[h] = m_new

    for h in heads:
        num = intra_mm[h] + w_inter[h] * inter_mm[h]
        nq = num[ML_DIM:ML_DIM + 1, :]
        hid = num[0:ML_DIM, :] / jnp.maximum(jnp.abs(nq), jnp.exp(-m_row[h]))
        ms = jnp.mean(hid * hid, axis=0, keepdims=True)
        hn = (hid * lax.rsqrt(ms + NORM_EPS)) * ng_ref[h * ML_DIM:(h + 1) * ML_DIM, :]
        og = o_t_ref[h * ML_DIM:(h + 1) * ML_DIM, :].astype(F32)
        gated = hn * (1.0 / (1.0 + jnp.exp(-og)))
        out_ref[:, h * ML_DIM:(h + 1) * ML_DIM] = gated.T.astype(BF16)


def _mlstm(mq_t, mk, mv_t, mo_t, g_t, gcum_t, gain, batch, seq_len):
    tokens = mk.shape[0]
    chunk = ML_CHUNK
    n_chunks = seq_len // chunk
    row_blk = lambda width: pl.BlockSpec((chunk, width), lambda b, c: (b * n_chunks + c, 0))
    col_blk = lambda height: pl.BlockSpec((height, chunk), lambda b, c: (0, b * n_chunks + c))
    return pl.pallas_call(
        _mlstm_kernel,
        grid=(batch, n_chunks),
        in_specs=[
            col_blk(ML_WIDTH), row_blk(ML_WIDTH), col_blk(ML_HEADS * V_EXT_ROWS), col_blk(ML_WIDTH),
            col_blk(2 * ML_HEADS), col_blk(2 * ML_HEADS),
            _const_spec(gain.shape),
        ],
        out_specs=row_blk(ML_WIDTH),
        out_shape=jax.ShapeDtypeStruct((tokens, ML_WIDTH), BF16),
        scratch_shapes=[pltpu.VMEM((ML_HEADS, V_EXT_ROWS, ML_DIM), F32),
                        pltpu.VMEM((ML_HEADS, 1, 1), F32)],
        compiler_params=_compiler_params(2),
        name="mlstm",
    )(mq_t, mk, mv_t, mo_t, g_t, gcum_t, gain)


def _rms(x, g):
    ms = jnp.mean(x * x, axis=-1, keepdims=True)
    return (x * lax.rsqrt(ms + NORM_EPS)) * g


def _out_ffn_kernel(x_ref, attn_ref, ml_ref, w_out_ref, g_ffn_ref, w_gate_ref, w_up_ref, w_down_ref,
                    g_final_ref, out_ref):
    mix = jnp.concatenate([attn_ref[...], ml_ref[...]], axis=1)
    y = x_ref[...] + jnp.dot(mix, w_out_ref[...], preferred_element_type=F32)
    h2 = _rms(y, g_ffn_ref[...]).astype(BF16)
    gate = jnp.dot(h2, w_gate_ref[...], preferred_element_type=F32)
    up = jnp.dot(h2, w_up_ref[...], preferred_element_type=F32)
    act = ((gate * (1.0 / (1.0 + jnp.exp(-gate)))) * up).astype(BF16)
    y2 = y + jnp.dot(act, w_down_ref[...], preferred_element_type=F32)
    out_ref[...] = _rms(y2, g_final_ref[...])


def _out_ffn(x2d, attn, ml, w_out, g_ffn, w_gate, w_up, w_down, g_final):
    tokens = x2d.shape[0]
    rows = FFN_ROWS
    row_blk = lambda width: pl.BlockSpec((rows, width), lambda i: (i, 0))
    return pl.pallas_call(
        _out_ffn_kernel,
        grid=(tokens // rows,),
        in_specs=[
            row_blk(D_MODEL), row_blk(DA_WIDTH), row_blk(ML_WIDTH),
            _const_spec(w_out.shape), _const_spec(g_ffn.shape),
            _const_spec(w_gate.shape), _const_spec(w_up.shape), _const_spec(w_down.shape),
            _const_spec(g_final.shape),
        ],
        out_specs=row_blk(D_MODEL),
        out_shape=jax.ShapeDtypeStruct((tokens, D_MODEL), F32),
        compiler_params=_compiler_params(1),
        name="out_ffn",
    )(x2d, attn, ml, w_out, g_ffn, w_gate, w_up, w_down, g_final)


def kernel(x, positions, mix_norm_g, w_in, da_lambda, da_subln_g, ml_conv_w, ml_conv_b, ml_gate_b,
           ml_norm_g, w_out, ffn_norm_g, w_gate, w_up, w_down, final_norm_g):
    batch, seq_len, _ = x.shape
    tokens = batch * seq_len
    depth = w_in.shape[0]
    assert depth == 1, "one trunk layer"
    assert seq_len % PROJ_ROWS == 0 and seq_len % ATTN_Q == 0 and seq_len % ML_CHUNK == 0
    assert ATTN_Q % ATTN_K == 0 and ATTN_K % CHUNK == 0 and tokens % FFN_ROWS == 0
    assert 2 * DA_QK_DIM + MASK_FEATS <= V7X_MXU_DEPTH and MASK_FEATS <= 2 * DA_QK_DIM
    assert DA_V_DIM == ML_DIM and DA_HEADS == ML_HEADS

    x2d = x.reshape(tokens, D_MODEL)
    pos3d = positions.reshape(tokens // PROJ_ROWS, 1, PROJ_ROWS)

    w = w_in[0]
    ml_v = OFF_ML + 2 * ML_WIDTH
    w_t = jnp.concatenate([w[:, OFF_DA_Q:OFF_ML], w[:, ml_v:OFF_GATE]], axis=1).T.astype(BF16)
    gate_cols = w[:, OFF_GATE:OFF_GATE + 2 * ML_HEADS]
    wg_t = jnp.pad(gate_cols.T, ((0, 2 * ML_HEADS), (0, 0))).astype(BF16)
    w_row = w[:, OFF_ML:ml_v].astype(BF16)
    inv_freq = (ROPE_THETA ** (-jnp.arange(0, ROT_DIM, 2, dtype=F32) / ROT_DIM)).reshape(ROT_HALF, 1)
    gb_col = ml_gate_b[0].astype(F32).reshape(2 * ML_HEADS, 1)

    q_t, k, v_t, mq_t, mk, mv_t, mo_t, g_t, gcum_t = _in_proj(
        x2d, pos3d, mix_norm_g[0].reshape(1, D_MODEL).astype(F32), w_t, wg_t, w_row, inv_freq,
        ml_conv_w[0].astype(F32), ml_conv_b[0].reshape(1, 2 * ML_WIDTH).astype(F32),
        gb_col, seq_len)

    attn = _diff_attn(da_lambda[0].astype(F32), q_t, k, v_t,
                      da_subln_g[0].astype(F32).reshape(DA_V_DIM, 1), batch, seq_len)
    ml_gain = jnp.broadcast_to(ml_norm_g[0].astype(F32).reshape(ML_WIDTH, 1), (ML_WIDTH, ML_CHUNK))
    ml = _mlstm(mq_t, mk, mv_t, mo_t, g_t, gcum_t, ml_gain, batch, seq_len)

    out = _out_ffn(x2d, attn, ml, w_out[0].astype(BF16),
                   ffn_norm_g[0].reshape(1, D_MODEL).astype(F32),
                   w_gate[0].astype(BF16), w_up[0].astype(BF16), w_down[0].astype(BF16),
                   final_norm_g.reshape(1, D_MODEL).astype(F32))
    return out.reshape(batch, seq_len, D_MODEL)
```

```python
import functools
import math

import jax
import jax.numpy as jnp
from jax import lax
from jax.experimental import pallas as pl
from jax.experimental.pallas import tpu as pltpu

F32 = jnp.float32
BF16 = jnp.bfloat16

D_MODEL = 1024
CHUNK = 64
NORM_EPS = 1e-6
DA_HEADS = 4
DA_QK_DIM = 64
DA_V_DIM = 128
DA_WIDTH = DA_HEADS * DA_V_DIM
ROPE_THETA = 500000.0
ROT_DIM = DA_QK_DIM // 4
ROT_HALF = ROT_DIM // 2
ML_HEADS = 4
ML_DIM = 128
ML_WIDTH = ML_HEADS * ML_DIM
CONV_WIDTH = 4
D_FF = 2816
LAM_INIT = 0.8 - 0.6 * math.exp(-0.3 * 0)
Q_SCALE = DA_QK_DIM ** -0.5 * math.log2(math.e)

OFF_DA_Q = 0
OFF_DA_K = 512
OFF_DA_V = 1024
OFF_ML = 1536
OFF_GATE = 3584
ZT_ML_V = 1536
ZT_ML_O = 2048

V7X_LANES = 128
V7X_SUBLANES = 8
V7X_BF16_ROWS_PER_VREG = 16
V7X_MXU_DEPTH = 256
V7X_VMEM_LIMIT_BYTES = 56 * 1024 * 1024

PROJ_ROWS = 512
ATTN_Q = 1024
ATTN_K = 512
MASK_FEATS = ATTN_Q // CHUNK
MASK_BIG = 1e30
V_EXT_ROWS = DA_V_DIM + V7X_BF16_ROWS_PER_VREG
ML_CHUNK = 256
FFN_ROWS = 512
CONV_HALO = V7X_SUBLANES

_NT = (((1,), (1,)), ((), ()))


def _compiler_params(n_axes):
    return pltpu.CompilerParams(
        dimension_semantics=("arbitrary",) * n_axes,
        vmem_limit_bytes=V7X_VMEM_LIMIT_BYTES,
    )


def _const_spec(shape):
    zeros = (0,) * len(shape)
    return pl.BlockSpec(shape, lambda *_: zeros, pipeline_mode=pl.Buffered(1))


def _rope_rows(zt, cos, sin):
    pieces = []
    for g in range(2):
        base = g * DA_QK_DIM
        x1 = zt[base:base + ROT_HALF]
        x2 = zt[base + ROT_HALF:base + ROT_DIM]
        pieces += [x1 * cos - x2 * sin, x2 * cos + x1 * sin, zt[base + ROT_DIM:base + DA_QK_DIM]]
    return jnp.concatenate(pieces, axis=0)


def _in_proj_kernel(tiles_per_seq,
                    x_ref, pos_ref, g_ref, w_t_ref, wg_t_ref, w_row_ref, invf_ref,
                    convw_ref, convb_ref, gb_col_ref,
                    q_t_ref, k_ref, v_t_ref, mq_t_ref, mk_ref, mv_t_ref, mo_t_ref, g_t_ref, gcum_t_ref,
                    halo_ref):
    rows = x_ref.shape[0]
    x = x_ref[...]
    ms = jnp.mean(x * x, axis=-1, keepdims=True)
    hb = ((x * lax.rsqrt(ms + NORM_EPS)) * g_ref[...]).astype(BF16)

    first = (pl.program_id(0) % tiles_per_seq) == 0
    halo_ref[0:CONV_HALO, :] = jnp.where(first, 0.0, halo_ref[rows:rows + CONV_HALO, :])
    halo_ref[CONV_HALO:CONV_HALO + rows, :] = jnp.dot(hb, w_row_ref[...], preferred_element_type=F32)

    def proj_t(lo, hi):
        return lax.dot_general(w_t_ref[lo:hi, :], hb, _NT, preferred_element_type=F32)

    zqk_t = proj_t(OFF_DA_Q, OFF_DA_V)

    xe = halo_ref[...]
    conv = convw_ref[0:1, :] * xe
    for j in range(1, CONV_WIDTH):
        conv = pltpu.roll(conv, 1, axis=0) + convw_ref[j:j + 1, :] * xe
    conv = conv[CONV_HALO:, :] + convb_ref[...]
    act = conv * (1.0 / (1.0 + jnp.exp(-conv)))
    mq_t_ref[...] = (act[:, 0:ML_WIDTH] * (ML_DIM ** -0.5)).T.astype(BF16)
    mk_ref[...] = act[:, ML_WIDTH:2 * ML_WIDTH].astype(BF16)

    ang = invf_ref[...] * pos_ref[0].astype(F32)
    cos = jnp.cos(ang)
    sin = jnp.sin(ang)
    for h in range(DA_HEADS):
        lo = h * 2 * DA_QK_DIM
        hi = lo + 2 * DA_QK_DIM
        q_rot = _rope_rows(zqk_t[OFF_DA_Q + lo:OFF_DA_Q + hi], cos, sin)
        q_t_ref[lo:hi, :] = (q_rot * Q_SCALE).astype(BF16)
        k_rot = _rope_rows(zqk_t[OFF_DA_K + lo:OFF_DA_K + hi], cos, sin)
        k_ref[:, lo:hi] = k_rot.T.astype(BF16)
    pad_rows = V_EXT_ROWS - DA_V_DIM
    ones_row = (lax.broadcasted_iota(jnp.int32, (pad_rows, rows), 0) == 0).astype(BF16)
    for out_ref, base in ((v_t_ref, OFF_DA_V), (mv_t_ref, ZT_ML_V)):
        zv_t = proj_t(base, base + DA_WIDTH)
        for h in range(DA_HEADS):
            v_lo = h * DA_V_DIM
            out_ref[h * V_EXT_ROWS:h * V_EXT_ROWS + DA_V_DIM, :] = zv_t[v_lo:v_lo + DA_V_DIM].astype(BF16)
            out_ref[h * V_EXT_ROWS + DA_V_DIM:(h + 1) * V_EXT_ROWS, :] = ones_row
    mo_t_ref[...] = proj_t(ZT_ML_O, ZT_ML_O + ML_WIDTH).astype(BF16)

    gz = lax.dot_general(wg_t_ref[...], hb, _NT, preferred_element_type=F32)
    a_t = gz[0:2 * ML_HEADS] + gb_col_ref[...]
    ls_t = jnp.minimum(a_t, 0.0) - jnp.log1p(jnp.exp(-jnp.abs(a_t)))
    row_id = lax.broadcasted_iota(jnp.int32, a_t.shape, 0)
    log_gates = jnp.where(row_id < ML_HEADS, a_t, ls_t)
    g_t_ref[...] = log_gates
    src = lax.broadcasted_iota(jnp.int32, (rows, rows), 0)
    dst = lax.broadcasted_iota(jnp.int32, (rows, rows), 1)
    tri = ((src <= dst) & (src // ML_CHUNK == dst // ML_CHUNK)).astype(F32)
    gcum_t_ref[...] = jnp.dot(log_gates, tri, preferred_element_type=F32,
                              precision=lax.Precision.HIGHEST)


def _in_proj(x2d, pos3d, norm_g, w_t, wg_t, w_row, inv_freq, conv_w, conv_b, gb_col, seq_len):
    tokens = x2d.shape[0]
    rows = PROJ_ROWS
    n_tiles = tokens // rows
    row_blk = lambda width: pl.BlockSpec((rows, width), lambda i: (i, 0))
    col_blk = lambda height: pl.BlockSpec((height, rows), lambda i: (0, i))
    out_shape = (
        jax.ShapeDtypeStruct((DA_WIDTH, tokens), BF16),
        jax.ShapeDtypeStruct((tokens, DA_WIDTH), BF16),
        jax.ShapeDtypeStruct((DA_HEADS * V_EXT_ROWS, tokens), BF16),
        jax.ShapeDtypeStruct((ML_WIDTH, tokens), BF16),
        jax.ShapeDtypeStruct((tokens, ML_WIDTH), BF16),
        jax.ShapeDtypeStruct((ML_HEADS * V_EXT_ROWS, tokens), BF16),
        jax.ShapeDtypeStruct((ML_WIDTH, tokens), BF16),
        jax.ShapeDtypeStruct((2 * ML_HEADS, tokens), F32),
        jax.ShapeDtypeStruct((2 * ML_HEADS, tokens), F32),
    )
    return pl.pallas_call(
        functools.partial(_in_proj_kernel, seq_len // rows),
        grid=(n_tiles,),
        in_specs=[
            row_blk(D_MODEL),
            pl.BlockSpec((1, 1, rows), lambda i: (i, 0, 0)),
            _const_spec(norm_g.shape),
            _const_spec(w_t.shape),
            _const_spec(wg_t.shape),
            _const_spec(w_row.shape),
            _const_spec(inv_freq.shape),
            _const_spec(conv_w.shape),
            _const_spec(conv_b.shape),
            _const_spec(gb_col.shape),
        ],
        out_specs=(
            col_blk(DA_WIDTH), row_blk(DA_WIDTH), col_blk(DA_HEADS * V_EXT_ROWS),
            col_blk(ML_WIDTH), row_blk(ML_WIDTH), col_blk(ML_HEADS * V_EXT_ROWS), col_blk(ML_WIDTH),
            col_blk(2 * ML_HEADS), col_blk(2 * ML_HEADS),
        ),
        out_shape=out_shape,
        scratch_shapes=[pltpu.VMEM((rows + CONV_HALO, 2 * ML_WIDTH), F32)],
        compiler_params=_compiler_params(1),
        name="in_proj",
    )(x2d, pos3d, norm_g, w_t, wg_t, w_row, inv_freq, conv_w, conv_b, gb_col)


def _attn_kernel(lam_ref, q_t_ref, k_ref, v_t_ref, g_ref, o_ref,
                 qm_ref, ind_ref, s_ref, p_ref, acc_ref):
    seq = k_ref.shape[0]
    tq, tk = ATTN_Q, ATTN_K
    n_q = seq // tq
    diag_tiles = tq // tk

    lv = lam_ref[...]
    lam = (jnp.exp(jnp.sum(lv[0:1] * lv[1:2], axis=1, keepdims=True))
           - jnp.exp(jnp.sum(lv[2:3] * lv[3:4], axis=1, keepdims=True)) + LAM_INIT)

    feat = lax.broadcasted_iota(jnp.int32, (MASK_FEATS, 2 * tq), 0)
    qchunk = (lax.broadcasted_iota(jnp.int32, (MASK_FEATS, 2 * tq), 1) % tq) // CHUNK
    qm_ref[...] = jnp.zeros(qm_ref.shape, BF16)
    qm_ref[2 * DA_QK_DIM:2 * DA_QK_DIM + MASK_FEATS, :] = jnp.where(
        feat > qchunk, -MASK_BIG, 0.0).astype(BF16)
    kchunk = lax.broadcasted_iota(jnp.int32, (tk, 2 * DA_QK_DIM), 0) // CHUNK
    lane = lax.broadcasted_iota(jnp.int32, (tk, 2 * DA_QK_DIM), 1)
    ind_ref[0] = jnp.zeros((tk, 2 * DA_QK_DIM), BF16)
    for d in range(diag_tiles):
        ind_ref[d + 1] = (lane == kchunk + d * (tk // CHUNK)).astype(BF16)

    def q_tile(i, first_tile):
        q_off = pl.multiple_of(i * tq, tq)
        t_diag = diag_tiles * i
        n_t = t_diag + diag_tiles

        def score(par, t, off=0, which=None):
            k_t = k_ref[pl.ds(pl.multiple_of(t * tk, tk), tk), :]
            if which is None:
                which = jnp.maximum(t - t_diag + 1, 0)
            k_ext = jnp.concatenate([k_t, ind_ref[which]], axis=1)
            if off:
                rhs = jnp.concatenate([qm_ref[:, off:tq], qm_ref[:, tq + off:2 * tq]], axis=1)
            else:
                rhs = qm_ref[...]
            s_ref[par, :, 0:2 * (tq - off)] = jnp.dot(k_ext, rhs, preferred_element_type=F32)

        def softmax(par, maxes, off=0):
            w = tq - off
            new, alphas = [], []
            for mi in range(2):
                lanes = slice(mi * w, (mi + 1) * w)
                s = s_ref[par, :, lanes]
                m_old = maxes[mi][:, off:tq]
                m_new = jnp.maximum(m_old, jnp.max(s, axis=0, keepdims=True))
                p_ref[par, :, lanes] = jnp.exp2(s - m_new).astype(BF16)
                alphas.append(jnp.exp2(m_old - m_new))
                new.append(jnp.concatenate([maxes[mi][:, 0:off], m_new], axis=1) if off else m_new)
            return tuple(new), tuple(alphas)

        def accumulate(par, t, alphas, off=0):
            w = tq - off
            v_t = v_t_ref[:, pl.ds(pl.multiple_of(t * tk, tk), tk)]
            pv = jnp.dot(v_t, p_ref[par, :, 0:2 * w], preferred_element_type=F32)
            for mi in range(2):
                lanes = slice(mi * tq + off, (mi + 1) * tq)
                acc_ref[:, lanes] = alphas[mi] * acc_ref[:, lanes] + pv[:, mi * w:(mi + 1) * w]

        def step(par, t, carry, first=False):
            maxes, alphas = carry
            score(1 - par, t + 1)
            if not first:
                accumulate(1 - par, t - 1, alphas)
            return softmax(par, maxes)

        q_t = q_t_ref[:, pl.ds(q_off, tq)]
        qm_ref[0:DA_QK_DIM, 0:tq] = q_t[0:DA_QK_DIM]
        qm_ref[DA_QK_DIM:2 * DA_QK_DIM, tq:2 * tq] = q_t[DA_QK_DIM:]
        score(0, 0)
        if not first_tile:
            finish(i - 1)
        acc_ref[...] = jnp.zeros(acc_ref.shape, F32)

        neg = jnp.full((1, tq), -jnp.inf, F32)
        maxes, alphas = (neg, neg), None
        if not first_tile:
            carry = step(0, 0, (maxes, alphas), first=True)

            def pair(u, carry):
                carry = step(1, 2 * u + 1, carry)
                return step(0, 2 * u + 2, carry)

            carry = lax.fori_loop(0, t_diag // 2 - 1, pair, carry)
            maxes, alphas = step(1, t_diag - 1, carry)
        for d in range(diag_tiles):
            t, par = t_diag + d, d % 2
            if d + 1 < diag_tiles:
                score(1 - par, t + 1, off=(d + 1) * tk, which=d + 2)
            if d or not first_tile:
                accumulate(1 - par, t - 1, alphas, off=max(d - 1, 0) * tk)
            maxes, alphas = softmax(par, maxes, off=d * tk)
        accumulate((diag_tiles - 1) % 2, n_t - 1, alphas, off=(diag_tiles - 1) * tk)
        return 0

    def finish(i):
        acc = acc_ref[...]
        o1 = acc[0:DA_V_DIM, 0:tq] / acc[DA_V_DIM:DA_V_DIM + 1, 0:tq]
        o2 = acc[0:DA_V_DIM, tq:2 * tq] / acc[DA_V_DIM:DA_V_DIM + 1, tq:2 * tq]
        o_t = o1 - lam * o2
        ms = jnp.mean(o_t * o_t, axis=0, keepdims=True)
        y_t = (o_t * lax.rsqrt(ms + NORM_EPS)) * g_ref[...] * (1.0 - LAM_INIT)
        o_ref[pl.ds(pl.multiple_of(i * tq, tq), tq), :] = y_t.T.astype(BF16)

    q_tile(jnp.int32(0), True)
    lax.fori_loop(1, n_q, lambda i, c: q_tile(i, False), 0)
    finish(jnp.int32(n_q - 1))


def _diff_attn(lam_params, q_t, k, v_t, subln_col, batch, seq_len):
    tokens = k.shape[0]
    tq, tk = ATTN_Q, ATTN_K
    return pl.pallas_call(
        _attn_kernel,
        grid=(batch, DA_HEADS),
        in_specs=[
            _const_spec(lam_params.shape),
            pl.BlockSpec((DA_V_DIM, seq_len), lambda b, h: (h, b)),
            pl.BlockSpec((seq_len, DA_V_DIM), lambda b, h: (b, h)),
            pl.BlockSpec((V_EXT_ROWS, seq_len), lambda b, h: (h, b)),
            _const_spec(subln_col.shape),
        ],
        out_specs=pl.BlockSpec((seq_len, DA_V_DIM), lambda b, h: (b, h)),
        out_shape=jax.ShapeDtypeStruct((tokens, DA_WIDTH), BF16),
        scratch_shapes=[
            pltpu.VMEM((V7X_MXU_DEPTH, 2 * tq), BF16),
            pltpu.VMEM((1 + tq // tk, tk, 2 * DA_QK_DIM), BF16),
            pltpu.VMEM((2, tk, 2 * tq), F32),
            pltpu.VMEM((2, tk, 2 * tq), BF16),
            pltpu.VMEM((V_EXT_ROWS, 2 * tq), F32),
        ],
        compiler_params=_compiler_params(2),
        name="diff_attn",
    )(lam_params, q_t, k, v_t, subln_col)


def _mlstm_begin(q_t_ref, k_ref, v_t_ref, g_t_ref, gcum_t_ref, c_ref, m_ref, cols, reset):
    chunk = cols.stop - cols.start
    heads = range(ML_HEADS)
    g_t = g_t_ref[:, cols]
    cum_row = gcum_t_ref[:, cols]
    key_rows = g_t - pltpu.roll(cum_row, ML_HEADS, axis=0)
    key_cols = jnp.concatenate(
        [key_rows, jnp.zeros((V7X_LANES - 2 * ML_HEADS, chunk), F32)], axis=0).T
    st = dict(
        chunk=chunk,
        q_t=[q_t_ref[h * ML_DIM:(h + 1) * ML_DIM, cols] for h in heads],
        k=[k_ref[cols, h * ML_DIM:(h + 1) * ML_DIM] for h in heads],
        v_t=[v_t_ref[h * V_EXT_ROWS:(h + 1) * V_EXT_ROWS, cols] for h in heads],
        c_t=[c_ref[h] for h in heads],
        m_prev=[m_ref[h] for h in heads],
        b_row=[cum_row[ML_HEADS + h:ML_HEADS + h + 1, :] for h in heads],
        i_row=[g_t[h:h + 1, :] for h in heads],
        e_mat=[jnp.broadcast_to(key_cols[:, h:h + 1], (chunk, chunk)) for h in heads],
    )
    if reset is not None:
        st["c_t"] = [jnp.where(reset, 0.0, c) for c in st["c_t"]]
        st["m_prev"] = [jnp.where(reset, 0.0, m) for m in st["m_prev"]]
    st["kq"] = [jnp.dot(st["k"][h], st["q_t"][h], preferred_element_type=F32) for h in heads]
    st["inter_mm"] = [jnp.dot(st["c_t"][h].astype(BF16), st["q_t"][h], preferred_element_type=F32)
                      for h in heads]
    return st


def _mlstm_weights(st):
    chunk = st["chunk"]
    s_id = lax.broadcasted_iota(jnp.int32, (chunk, chunk), 0)
    t_id = lax.broadcasted_iota(jnp.int32, (chunk, chunk), 1)
    causal = s_id <= t_id
    st["m_row"], st["w_inter"], st["sc"] = [], [], []
    for h in range(ML_HEADS):
        d_mat = jnp.where(causal, st["e_mat"][h] + st["b_row"][h], -jnp.inf)
        inter = st["b_row"][h] + st["m_prev"][h]
        m_row = jnp.maximum(inter, jnp.max(d_mat, axis=0, keepdims=True))
        st["m_row"].append(m_row)
        st["w_inter"].append(jnp.exp(inter - m_row))
        st["sc"].append((st["kq"][h] * jnp.exp(d_mat - m_row)).astype(BF16))


def _mlstm_matmuls(st, c_ref, m_ref):
    chunk = st["chunk"]
    heads = range(ML_HEADS)
    st["intra_mm"] = [jnp.dot(st["v_t"][h], st["sc"][h], preferred_element_type=F32) for h in heads]
    for h in heads:
        b_row, m_prev = st["b_row"][h], st["m_prev"][h]
        b_last = b_row[:, chunk - 1:chunk]
        g_row = (b_last - b_row) + st["i_row"][h]
        m_new = jnp.maximum(b_last + m_prev, jnp.max(g_row, axis=1, keepdims=True))
        decay = jnp.exp(b_last + m_prev - m_new)
        wk = jnp.exp(st["e_mat"][h][:, 0:ML_DIM] + (b_last - m_new))
        kw = (st["k"][h].astype(F32) * wk).astype(BF16)
        c_ref[h] = decay * st["c_t"][h] + jnp.dot(st["v_t"][h], kw, preferred_element_type=F32)
        m_ref[h] = m_new


def _mlstm_output(st, o_t_ref, ng_ref, cols):
    outs = []
    for h in range(ML_HEADS):
        num = st["intra_mm"][h] + st["w_inter"][h] * st["inter_mm"][h]
        nq = num[ML_DIM:ML_DIM + 1, :]
        hid = num[0:ML_DIM, :] / jnp.maximum(jnp.abs(nq), jnp.exp(-st["m_row"][h]))
        ms = jnp.mean(hid * hid, axis=0, keepdims=True)
        hn = (hid * lax.rsqrt(ms + NORM_EPS)) * ng_ref[h * ML_DIM:(h + 1) * ML_DIM, :]
        og = o_t_ref[h * ML_DIM:(h + 1) * ML_DIM, cols].astype(F32)
        outs.append((hn * (1.0 / (1.0 + jnp.exp(-og)))).T.astype(BF16))
    return jnp.concatenate(outs, axis=1)


def _rms(x, g):
    ms = jnp.mean(x * x, axis=-1, keepdims=True)
    return (x * lax.rsqrt(ms + NORM_EPS)) * g


def _out_ffn_kernel(tiles_per_seq, n_tiles,
                    x_ref, attn_ref, mq_t_ref, mk_ref, mv_t_ref, mo_t_ref, g_t_ref, gcum_t_ref, ng_ref,
                    w_out_ref, g_ffn_ref, w_gate_ref, w_up_ref, w_down_ref, g_final_ref,
                    out_ref, ml_ref, c_ref, m_ref):
    i = pl.program_id(0)
    par = i % 2
    reset = (jnp.minimum(i, n_tiles - 1) % tiles_per_seq) == 0
    chunks = [slice(c * ML_CHUNK, (c + 1) * ML_CHUNK) for c in range(FFN_ROWS // ML_CHUNK)]
    ml_args = (mq_t_ref, mk_ref, mv_t_ref, g_t_ref, gcum_t_ref, c_ref, m_ref)

    @pl.when(i == 0)
    def _():
        ml_ref[...] = jnp.zeros(ml_ref.shape, BF16)
        c_ref[...] = jnp.zeros(c_ref.shape, F32)
        m_ref[...] = jnp.zeros(m_ref.shape, F32)

    mix = jnp.concatenate([attn_ref[...], ml_ref[1 - par]], axis=1)
    first = _mlstm_begin(*ml_args, chunks[0], reset)
    y = x_ref[...] + jnp.dot(mix, w_out_ref[...], preferred_element_type=F32)
    h2 = _rms(y, g_ffn_ref[...]).astype(BF16)
    _mlstm_weights(first)
    _mlstm_matmuls(first, c_ref, m_ref)
    gate = jnp.dot(h2, w_gate_ref[...], preferred_element_type=F32)
    ml_ref[par, chunks[0], :] = _mlstm_output(first, mo_t_ref, ng_ref, chunks[0])
    second = _mlstm_begin(*ml_args, chunks[1], None)
    up = jnp.dot(h2, w_up_ref[...], preferred_element_type=F32)
    _mlstm_weights(second)
    _mlstm_matmuls(second, c_ref, m_ref)
    act = ((gate * (1.0 / (1.0 + jnp.exp(-gate)))) * up).astype(BF16)
    y2 = y + jnp.dot(act, w_down_ref[...], preferred_element_type=F32)
    ml_ref[par, chunks[1], :] = _mlstm_output(second, mo_t_ref, ng_ref, chunks[1])
    out_ref[...] = _rms(y2, g_final_ref[...])


def _out_ffn(x2d, attn, mq_t, mk, mv_t, mo_t, g_t, gcum_t, gain,
             w_out, g_ffn, w_gate, w_up, w_down, g_final, seq_len):
    tokens = x2d.shape[0]
    rows = FFN_ROWS
    n_tiles = tokens // rows
    ffn_blk = lambda width: pl.BlockSpec((rows, width), lambda i: (jnp.maximum(i - 1, 0), 0))
    ml_row = lambda width: pl.BlockSpec((rows, width), lambda i: (jnp.minimum(i, n_tiles - 1), 0))
    ml_col = lambda height: pl.BlockSpec((height, rows), lambda i: (0, jnp.minimum(i, n_tiles - 1)))
    return pl.pallas_call(
        functools.partial(_out_ffn_kernel, seq_len // rows, n_tiles),
        grid=(n_tiles + 1,),
        in_specs=[
            ffn_blk(D_MODEL), ffn_blk(DA_WIDTH),
            ml_col(ML_WIDTH), ml_row(ML_WIDTH), ml_col(ML_HEADS * V_EXT_ROWS), ml_col(ML_WIDTH),
            ml_col(2 * ML_HEADS), ml_col(2 * ML_HEADS),
            _const_spec(gain.shape),
            _const_spec(w_out.shape), _const_spec(g_ffn.shape),
            _const_spec(w_gate.shape), _const_spec(w_up.shape), _const_spec(w_down.shape),
            _const_spec(g_final.shape),
        ],
        out_specs=ffn_blk(D_MODEL),
        out_shape=jax.ShapeDtypeStruct((tokens, D_MODEL), F32),
        scratch_shapes=[pltpu.VMEM((2, rows, ML_WIDTH), BF16),
                        pltpu.VMEM((ML_HEADS, V_EXT_ROWS, ML_DIM), F32),
                        pltpu.VMEM((ML_HEADS, 1, 1), F32)],
        compiler_params=_compiler_params(1),
        name="out_ffn",
    )(x2d, attn, mq_t, mk, mv_t, mo_t, g_t, gcum_t, gain,
      w_out, g_ffn, w_gate, w_up, w_down, g_final)


def kernel(x, positions, mix_norm_g, w_in, da_lambda, da_subln_g, ml_conv_w, ml_conv_b, ml_gate_b,
           ml_norm_g, w_out, ffn_norm_g, w_gate, w_up, w_down, final_norm_g):
    batch, seq_len, _ = x.shape
    tokens = batch * seq_len
    depth = w_in.shape[0]
    assert depth == 1, "one trunk layer"
    assert seq_len % PROJ_ROWS == 0 and seq_len % ATTN_Q == 0 and seq_len % ML_CHUNK == 0
    assert ATTN_Q % ATTN_K == 0 and ATTN_K % CHUNK == 0 and tokens % FFN_ROWS == 0
    assert 2 * DA_QK_DIM + MASK_FEATS <= V7X_MXU_DEPTH and MASK_FEATS <= 2 * DA_QK_DIM
    assert DA_V_DIM == ML_DIM and DA_HEADS == ML_HEADS and FFN_ROWS == 2 * ML_CHUNK

    x2d = x.reshape(tokens, D_MODEL)
    pos3d = positions.reshape(tokens // PROJ_ROWS, 1, PROJ_ROWS)

    w = w_in[0]
    ml_v = OFF_ML + 2 * ML_WIDTH
    w_t = jnp.concatenate([w[:, OFF_DA_Q:OFF_ML], w[:, ml_v:OFF_GATE]], axis=1).T.astype(BF16)
    gate_cols = w[:, OFF_GATE:OFF_GATE + 2 * ML_HEADS]
    wg_t = jnp.pad(gate_cols.T, ((0, 2 * ML_HEADS), (0, 0))).astype(BF16)
    w_row = w[:, OFF_ML:ml_v].astype(BF16)
    inv_freq = (ROPE_THETA ** (-jnp.arange(0, ROT_DIM, 2, dtype=F32) / ROT_DIM)).reshape(ROT_HALF, 1)
    gb_col = ml_gate_b[0].astype(F32).reshape(2 * ML_HEADS, 1)

    q_t, k, v_t, mq_t, mk, mv_t, mo_t, g_t, gcum_t = _in_proj(
        x2d, pos3d, mix_norm_g[0].reshape(1, D_MODEL).astype(F32), w_t, wg_t, w_row, inv_freq,
        ml_conv_w[0].astype(F32), ml_conv_b[0].reshape(1, 2 * ML_WIDTH).astype(F32),
        gb_col, seq_len)

    attn = _diff_attn(da_lambda[0].astype(F32), q_t, k, v_t,
                      da_subln_g[0].astype(F32).reshape(DA_V_DIM, 1), batch, seq_len)
    ml_gain = jnp.broadcast_to(ml_norm_g[0].astype(F32).reshape(ML_WIDTH, 1), (ML_WIDTH, ML_CHUNK))

    out = _out_ffn(x2d, attn, mq_t, mk, mv_t, mo_t, g_t, gcum_t, ml_gain, w_out[0].astype(BF16),
                   ffn_norm_g[0].reshape(1, D_MODEL).astype(F32),
                   w_gate[0].astype(BF16), w_up[0].astype(BF16), w_down[0].astype(BF16),
                   final_norm_g.reshape(1, D_MODEL).astype(F32), seq_len)
    return out.reshape(batch, seq_len, D_MODEL)
```

```python
import functools
import math

import jax
import jax.numpy as jnp
from jax import lax
from jax.experimental import pallas as pl
from jax.experimental.pallas import tpu as pltpu

F32 = jnp.float32
BF16 = jnp.bfloat16

D_MODEL = 1024
CHUNK = 64
NORM_EPS = 1e-6
DA_HEADS = 4
DA_QK_DIM = 64
DA_V_DIM = 128
DA_WIDTH = DA_HEADS * DA_V_DIM
ROPE_THETA = 500000.0
ROT_DIM = DA_QK_DIM // 4
ROT_HALF = ROT_DIM // 2
ML_HEADS = 4
ML_DIM = 128
ML_WIDTH = ML_HEADS * ML_DIM
CONV_WIDTH = 4
D_FF = 2816
LAM_INIT = 0.8 - 0.6 * math.exp(-0.3 * 0)
Q_SCALE = DA_QK_DIM ** -0.5 * math.log2(math.e)

OFF_DA_Q = 0
OFF_DA_K = 512
OFF_DA_V = 1024
OFF_ML = 1536
OFF_GATE = 3584
ZT_ML_V = 1536
ZT_ML_O = 2048

V7X_LANES = 128
V7X_SUBLANES = 8
V7X_BF16_ROWS_PER_VREG = 16
V7X_MXU_DEPTH = 256
V7X_VMEM_LIMIT_BYTES = 56 * 1024 * 1024

PROJ_ROWS = 512
ATTN_Q = 1024
ATTN_K = 512
MASK_FEATS = ATTN_Q // CHUNK
MASK_BIG = 1e30
V_EXT_ROWS = DA_V_DIM + V7X_BF16_ROWS_PER_VREG
ML_CHUNK = 256
FFN_ROWS = 512
CONV_HALO = V7X_SUBLANES

_NT = (((1,), (1,)), ((), ()))


def _compiler_params(n_axes):
    return pltpu.CompilerParams(
        dimension_semantics=("arbitrary",) * n_axes,
        vmem_limit_bytes=V7X_VMEM_LIMIT_BYTES,
    )


def _const_spec(shape):
    zeros = (0,) * len(shape)
    return pl.BlockSpec(shape, lambda *_: zeros, pipeline_mode=pl.Buffered(1))


def _rope_rows(zt, cos, sin):
    pieces = []
    for g in range(2):
        base = g * DA_QK_DIM
        x1 = zt[base:base + ROT_HALF]
        x2 = zt[base + ROT_HALF:base + ROT_DIM]
        pieces += [x1 * cos - x2 * sin, x2 * cos + x1 * sin, zt[base + ROT_DIM:base + DA_QK_DIM]]
    return jnp.concatenate(pieces, axis=0)


def _in_proj_kernel(tiles_per_seq,
                    x_ref, pos_ref, g_ref, w_t_ref, wg_t_ref, w_row_ref, invf_ref,
                    convw_ref, convb_ref, gb_col_ref,
                    q_t_ref, k_ref, v_t_ref, mq_t_ref, mk_ref, mv_t_ref, mo_t_ref, g_t_ref, gcum_t_ref,
                    halo_ref):
    rows = x_ref.shape[0]
    x = x_ref[...]
    ms = jnp.mean(x * x, axis=-1, keepdims=True)
    hb = ((x * lax.rsqrt(ms + NORM_EPS)) * g_ref[...]).astype(BF16)

    first = (pl.program_id(0) % tiles_per_seq) == 0
    halo_ref[0:CONV_HALO, :] = jnp.where(first, 0.0, halo_ref[rows:rows + CONV_HALO, :])
    halo_ref[CONV_HALO:CONV_HALO + rows, :] = jnp.dot(hb, w_row_ref[...], preferred_element_type=F32)

    def proj_t(lo, hi):
        return lax.dot_general(w_t_ref[lo:hi, :], hb, _NT, preferred_element_type=F32)

    zqk_t = proj_t(OFF_DA_Q, OFF_DA_V)

    xe = halo_ref[...]
    conv = convw_ref[0:1, :] * xe
    for j in range(1, CONV_WIDTH):
        conv = pltpu.roll(conv, 1, axis=0) + convw_ref[j:j + 1, :] * xe
    conv = conv[CONV_HALO:, :] + convb_ref[...]
    act = conv * (1.0 / (1.0 + jnp.exp(-conv)))
    mq_t_ref[...] = (act[:, 0:ML_WIDTH] * (ML_DIM ** -0.5)).T.astype(BF16)
    mk_ref[...] = act[:, ML_WIDTH:2 * ML_WIDTH].astype(BF16)

    ang = invf_ref[...] * pos_ref[0].astype(F32)
    cos = jnp.cos(ang)
    sin = jnp.sin(ang)
    for h in range(DA_HEADS):
        lo = h * 2 * DA_QK_DIM
        hi = lo + 2 * DA_QK_DIM
        q_rot = _rope_rows(zqk_t[OFF_DA_Q + lo:OFF_DA_Q + hi], cos, sin)
        q_t_ref[lo:hi, :] = (q_rot * Q_SCALE).astype(BF16)
        k_rot = _rope_rows(zqk_t[OFF_DA_K + lo:OFF_DA_K + hi], cos, sin)
        k_ref[:, lo:hi] = k_rot.T.astype(BF16)
    pad_rows = V_EXT_ROWS - DA_V_DIM
    ones_row = (lax.broadcasted_iota(jnp.int32, (pad_rows, rows), 0) == 0).astype(BF16)
    for out_ref, base in ((v_t_ref, OFF_DA_V), (mv_t_ref, ZT_ML_V)):
        zv_t = proj_t(base, base + DA_WIDTH)
        for h in range(DA_HEADS):
            v_lo = h * DA_V_DIM
            out_ref[h * V_EXT_ROWS:h * V_EXT_ROWS + DA_V_DIM, :] = zv_t[v_lo:v_lo + DA_V_DIM].astype(BF16)
            out_ref[h * V_EXT_ROWS + DA_V_DIM:(h + 1) * V_EXT_ROWS, :] = ones_row
    mo_t_ref[...] = proj_t(ZT_ML_O, ZT_ML_O + ML_WIDTH).astype(BF16)

    gz = lax.dot_general(wg_t_ref[...], hb, _NT, preferred_element_type=F32)
    a_t = gz[0:2 * ML_HEADS] + gb_col_ref[...]
    ls_t = jnp.minimum(a_t, 0.0) - jnp.log1p(jnp.exp(-jnp.abs(a_t)))
    row_id = lax.broadcasted_iota(jnp.int32, a_t.shape, 0)
    log_gates = jnp.where(row_id < ML_HEADS, a_t, ls_t)
    g_t_ref[...] = log_gates
    src = lax.broadcasted_iota(jnp.int32, (rows, rows), 0)
    dst = lax.broadcasted_iota(jnp.int32, (rows, rows), 1)
    tri = ((src <= dst) & (src // ML_CHUNK == dst // ML_CHUNK)).astype(F32)
    gcum_t_ref[...] = jnp.dot(log_gates, tri, preferred_element_type=F32,
                              precision=lax.Precision.HIGHEST)


def _in_proj(x2d, pos3d, norm_g, w_t, wg_t, w_row, inv_freq, conv_w, conv_b, gb_col, seq_len):
    tokens = x2d.shape[0]
    rows = PROJ_ROWS
    n_tiles = tokens // rows
    row_blk = lambda width: pl.BlockSpec((rows, width), lambda i: (i, 0))
    col_blk = lambda height: pl.BlockSpec((height, rows), lambda i: (0, i))
    out_shape = (
        jax.ShapeDtypeStruct((DA_WIDTH, tokens), BF16),
        jax.ShapeDtypeStruct((tokens, DA_WIDTH), BF16),
        jax.ShapeDtypeStruct((DA_HEADS * V_EXT_ROWS, tokens), BF16),
        jax.ShapeDtypeStruct((ML_WIDTH, tokens), BF16),
        jax.ShapeDtypeStruct((tokens, ML_WIDTH), BF16),
        jax.ShapeDtypeStruct((ML_HEADS * V_EXT_ROWS, tokens), BF16),
        jax.ShapeDtypeStruct((ML_WIDTH, tokens), BF16),
        jax.ShapeDtypeStruct((2 * ML_HEADS, tokens), F32),
        jax.ShapeDtypeStruct((2 * ML_HEADS, tokens), F32),
    )
    return pl.pallas_call(
        functools.partial(_in_proj_kernel, seq_len // rows),
        grid=(n_tiles,),
        in_specs=[
            row_blk(D_MODEL),
            pl.BlockSpec((1, 1, rows), lambda i: (i, 0, 0)),
            _const_spec(norm_g.shape),
            _const_spec(w_t.shape),
            _const_spec(wg_t.shape),
            _const_spec(w_row.shape),
            _const_spec(inv_freq.shape),
            _const_spec(conv_w.shape),
            _const_spec(conv_b.shape),
            _const_spec(gb_col.shape),
        ],
        out_specs=(
            col_blk(DA_WIDTH), row_blk(DA_WIDTH), col_blk(DA_HEADS * V_EXT_ROWS),
            col_blk(ML_WIDTH), row_blk(ML_WIDTH), col_blk(ML_HEADS * V_EXT_ROWS), col_blk(ML_WIDTH),
            col_blk(2 * ML_HEADS), col_blk(2 * ML_HEADS),
        ),
        out_shape=out_shape,
        scratch_shapes=[pltpu.VMEM((rows + CONV_HALO, 2 * ML_WIDTH), F32)],
        compiler_params=_compiler_params(1),
        name="in_proj",
    )(x2d, pos3d, norm_g, w_t, wg_t, w_row, inv_freq, conv_w, conv_b, gb_col)


def _attn_kernel(lam_ref, q_t_ref, k_ref, v_t_ref, g_ref, o_ref,
                 qm_ref, ind_ref, s_ref, p_ref, acc_ref):
    seq = k_ref.shape[0]
    tq, tk = ATTN_Q, ATTN_K
    n_q = seq // tq
    diag_tiles = tq // tk

    lv = lam_ref[...]
    lam = (jnp.exp(jnp.sum(lv[0:1] * lv[1:2], axis=1, keepdims=True))
           - jnp.exp(jnp.sum(lv[2:3] * lv[3:4], axis=1, keepdims=True)) + LAM_INIT)

    feat = lax.broadcasted_iota(jnp.int32, (MASK_FEATS, 2 * tq), 0)
    qchunk = (lax.broadcasted_iota(jnp.int32, (MASK_FEATS, 2 * tq), 1) % tq) // CHUNK
    qm_ref[...] = jnp.zeros(qm_ref.shape, BF16)
    qm_ref[2 * DA_QK_DIM:2 * DA_QK_DIM + MASK_FEATS, :] = jnp.where(
        feat > qchunk, -MASK_BIG, 0.0).astype(BF16)
    kchunk = lax.broadcasted_iota(jnp.int32, (tk, 2 * DA_QK_DIM), 0) // CHUNK
    lane = lax.broadcasted_iota(jnp.int32, (tk, 2 * DA_QK_DIM), 1)
    ind_ref[0] = jnp.zeros((tk, 2 * DA_QK_DIM), BF16)
    for d in range(diag_tiles):
        ind_ref[d + 1] = (lane == kchunk + d * (tk // CHUNK)).astype(BF16)

    def q_tile(i, first_tile):
        q_off = pl.multiple_of(i * tq, tq)
        t_diag = diag_tiles * i
        n_t = t_diag + diag_tiles

        def score(par, t, off=0, which=None):
            k_t = k_ref[pl.ds(pl.multiple_of(t * tk, tk), tk), :]
            if which is None:
                which = jnp.maximum(t - t_diag + 1, 0)
            k_ext = jnp.concatenate([k_t, ind_ref[which]], axis=1)
            if off:
                rhs = jnp.concatenate([qm_ref[:, off:tq], qm_ref[:, tq + off:2 * tq]], axis=1)
            else:
                rhs = qm_ref[...]
            w = tq - off
            s = jnp.dot(k_ext, rhs, preferred_element_type=F32)
            s_ref[par, :, 0:2 * w] = s
            return tuple(jnp.max(s[:, mi * w:(mi + 1) * w], axis=0, keepdims=True) for mi in range(2))

        def softmax(par, maxes, tile_max, off=0):
            w = tq - off
            new, alphas = [], []
            for mi in range(2):
                lanes = slice(mi * w, (mi + 1) * w)
                s = s_ref[par, :, lanes]
                m_old = maxes[mi][:, off:tq]
                m_new = jnp.maximum(m_old, tile_max[mi])
                p_ref[par, :, lanes] = jnp.exp2(s - m_new).astype(BF16)
                alphas.append(jnp.exp2(m_old - m_new))
                new.append(jnp.concatenate([maxes[mi][:, 0:off], m_new], axis=1) if off else m_new)
            return tuple(new), tuple(alphas)

        def accumulate(par, t, alphas, off=0):
            w = tq - off
            v_t = v_t_ref[:, pl.ds(pl.multiple_of(t * tk, tk), tk)]
            pv = jnp.dot(v_t, p_ref[par, :, 0:2 * w], preferred_element_type=F32)
            for mi in range(2):
                lanes = slice(mi * tq + off, (mi + 1) * tq)
                acc_ref[:, lanes] = alphas[mi] * acc_ref[:, lanes] + pv[:, mi * w:(mi + 1) * w]

        def step(par, t, carry, first=False):
            maxes, alphas, tile_max = carry
            next_max = score(1 - par, t + 1)
            if not first:
                accumulate(1 - par, t - 1, alphas)
            return softmax(par, maxes, tile_max) + (next_max,)

        q_t = q_t_ref[:, pl.ds(q_off, tq)]
        qm_ref[0:DA_QK_DIM, 0:tq] = q_t[0:DA_QK_DIM]
        qm_ref[DA_QK_DIM:2 * DA_QK_DIM, tq:2 * tq] = q_t[DA_QK_DIM:]
        tile_max = score(0, 0)
        if not first_tile:
            finish(i - 1)
        acc_ref[...] = jnp.zeros(acc_ref.shape, F32)

        neg = jnp.full((1, tq), -jnp.inf, F32)
        maxes, alphas = (neg, neg), None
        if not first_tile:
            carry = step(0, 0, (maxes, alphas, tile_max), first=True)

            def pair(u, carry):
                carry = step(1, 2 * u + 1, carry)
                return step(0, 2 * u + 2, carry)

            carry = lax.fori_loop(0, t_diag // 2 - 1, pair, carry)
            maxes, alphas, tile_max = step(1, t_diag - 1, carry)
        for d in range(diag_tiles):
            t, par = t_diag + d, d % 2
            if d + 1 < diag_tiles:
                next_max = score(1 - par, t + 1, off=(d + 1) * tk, which=d + 2)
            if d or not first_tile:
                accumulate(1 - par, t - 1, alphas, off=max(d - 1, 0) * tk)
            maxes, alphas = softmax(par, maxes, tile_max, off=d * tk)
            tile_max = next_max
        accumulate((diag_tiles - 1) % 2, n_t - 1, alphas, off=(diag_tiles - 1) * tk)
        return 0

    def finish(i):
        acc = acc_ref[...]
        o1 = acc[0:DA_V_DIM, 0:tq] / acc[DA_V_DIM:DA_V_DIM + 1, 0:tq]
        o2 = acc[0:DA_V_DIM, tq:2 * tq] / acc[DA_V_DIM:DA_V_DIM + 1, tq:2 * tq]
        o_t = o1 - lam * o2
        ms = jnp.mean(o_t * o_t, axis=0, keepdims=True)
        y_t = (o_t * lax.rsqrt(ms + NORM_EPS)) * g_ref[...] * (1.0 - LAM_INIT)
        o_ref[pl.ds(pl.multiple_of(i * tq, tq), tq), :] = y_t.T.astype(BF16)

    q_tile(jnp.int32(0), True)
    lax.fori_loop(1, n_q, lambda i, c: q_tile(i, False), 0)
    finish(jnp.int32(n_q - 1))


def _diff_attn(lam_params, q_t, k, v_t, subln_col, batch, seq_len):
    tokens = k.shape[0]
    tq, tk = ATTN_Q, ATTN_K
    return pl.pallas_call(
        _attn_kernel,
        grid=(batch, DA_HEADS),
        in_specs=[
            _const_spec(lam_params.shape),
            pl.BlockSpec((DA_V_DIM, seq_len), lambda b, h: (h, b)),
            pl.BlockSpec((seq_len, DA_V_DIM), lambda b, h: (b, h)),
            pl.BlockSpec((V_EXT_ROWS, seq_len), lambda b, h: (h, b)),
            _const_spec(subln_col.shape),
        ],
        out_specs=pl.BlockSpec((seq_len, DA_V_DIM), lambda b, h: (b, h)),
        out_shape=jax.ShapeDtypeStruct((tokens, DA_WIDTH), BF16),
        scratch_shapes=[
            pltpu.VMEM((V7X_MXU_DEPTH, 2 * tq), BF16),
            pltpu.VMEM((1 + tq // tk, tk, 2 * DA_QK_DIM), BF16),
            pltpu.VMEM((2, tk, 2 * tq), F32),
            pltpu.VMEM((2, tk, 2 * tq), BF16),
            pltpu.VMEM((V_EXT_ROWS, 2 * tq), F32),
        ],
        compiler_params=_compiler_params(2),
        name="diff_attn",
    )(lam_params, q_t, k, v_t, subln_col)


def _mlstm_begin(q_t_ref, k_ref, v_t_ref, g_t_ref, gcum_t_ref, c_ref, m_ref, cols, reset):
    chunk = cols.stop - cols.start
    heads = range(ML_HEADS)
    g_t = g_t_ref[:, cols]
    cum_row = gcum_t_ref[:, cols]
    key_rows = g_t - pltpu.roll(cum_row, ML_HEADS, axis=0)
    key_cols = jnp.concatenate(
        [key_rows, jnp.zeros((V7X_LANES - 2 * ML_HEADS, chunk), F32)], axis=0).T
    st = dict(
        chunk=chunk,
        q_t=[q_t_ref[h * ML_DIM:(h + 1) * ML_DIM, cols] for h in heads],
        k=[k_ref[cols, h * ML_DIM:(h + 1) * ML_DIM] for h in heads],
        v_t=[v_t_ref[h * V_EXT_ROWS:(h + 1) * V_EXT_ROWS, cols] for h in heads],
        c_t=[c_ref[h] for h in heads],
        m_prev=[m_ref[h] for h in heads],
        b_row=[cum_row[ML_HEADS + h:ML_HEADS + h + 1, :] for h in heads],
        i_row=[g_t[h:h + 1, :] for h in heads],
        e_mat=[jnp.broadcast_to(key_cols[:, h:h + 1], (chunk, chunk)) for h in heads],
    )
    if reset is not None:
        st["c_t"] = [jnp.where(reset, 0.0, c) for c in st["c_t"]]
        st["m_prev"] = [jnp.where(reset, 0.0, m) for m in st["m_prev"]]
    st["kq"] = [jnp.dot(st["k"][h], st["q_t"][h], preferred_element_type=F32) for h in heads]
    st["inter_mm"] = [jnp.dot(st["c_t"][h].astype(BF16), st["q_t"][h], preferred_element_type=F32)
                      for h in heads]
    return st


def _mlstm_weights(st):
    chunk = st["chunk"]
    s_id = lax.broadcasted_iota(jnp.int32, (chunk, chunk), 0)
    t_id = lax.broadcasted_iota(jnp.int32, (chunk, chunk), 1)
    causal = s_id <= t_id
    st["m_row"], st["w_inter"], st["sc"] = [], [], []
    for h in range(ML_HEADS):
        d_mat = jnp.where(causal, st["e_mat"][h] + st["b_row"][h], -jnp.inf)
        inter = st["b_row"][h] + st["m_prev"][h]
        m_row = jnp.maximum(inter, jnp.max(d_mat, axis=0, keepdims=True))
        st["m_row"].append(m_row)
        st["w_inter"].append(jnp.exp(inter - m_row))
        st["sc"].append((st["kq"][h] * jnp.exp(d_mat - m_row)).astype(BF16))


def _mlstm_matmuls(st, c_ref, m_ref):
    chunk = st["chunk"]
    heads = range(ML_HEADS)
    st["intra_mm"] = [jnp.dot(st["v_t"][h], st["sc"][h], preferred_element_type=F32) for h in heads]
    for h in heads:
        b_row, m_prev = st["b_row"][h], st["m_prev"][h]
        b_last = b_row[:, chunk - 1:chunk]
        g_row = (b_last - b_row) + st["i_row"][h]
        m_new = jnp.maximum(b_last + m_prev, jnp.max(g_row, axis=1, keepdims=True))
        decay = jnp.exp(b_last + m_prev - m_new)
        wk = jnp.exp(st["e_mat"][h][:, 0:ML_DIM] + (b_last - m_new))
        kw = (st["k"][h].astype(F32) * wk).astype(BF16)
        c_ref[h] = decay * st["c_t"][h] + jnp.dot(st["v_t"][h], kw, preferred_element_type=F32)
        m_ref[h] = m_new


def _mlstm_output(st, o_t_ref, ng_ref, cols):
    outs = []
    for h in range(ML_HEADS):
        num = st["intra_mm"][h] + st["w_inter"][h] * st["inter_mm"][h]
        nq = num[ML_DIM:ML_DIM + 1, :]
        hid = num[0:ML_DIM, :] / jnp.maximum(jnp.abs(nq), jnp.exp(-st["m_row"][h]))
        ms = jnp.mean(hid * hid, axis=0, keepdims=True)
        hn = (hid * lax.rsqrt(ms + NORM_EPS)) * ng_ref[h * ML_DIM:(h + 1) * ML_DIM, :]
        og = o_t_ref[h * ML_DIM:(h + 1) * ML_DIM, cols].astype(F32)
        outs.append((hn * (1.0 / (1.0 + jnp.exp(-og)))).T.astype(BF16))
    return jnp.concatenate(outs, axis=1)


def _rms(x, g):
    ms = jnp.mean(x * x, axis=-1, keepdims=True)
    return (x * lax.rsqrt(ms + NORM_EPS)) * g


def _out_ffn_kernel(tiles_per_seq, n_tiles,
                    x_ref, attn_ref, mq_t_ref, mk_ref, mv_t_ref, mo_t_ref, g_t_ref, gcum_t_ref, ng_ref,
                    w_out_ref, g_ffn_ref, w_gate_ref, w_up_ref, w_down_ref, g_final_ref,
                    out_ref, ml_ref, c_ref, m_ref):
    i = pl.program_id(0)
    par = i % 2
    reset = (jnp.minimum(i, n_tiles - 1) % tiles_per_seq) == 0
    chunks = [slice(c * ML_CHUNK, (c + 1) * ML_CHUNK) for c in range(FFN_ROWS // ML_CHUNK)]
    ml_args = (mq_t_ref, mk_ref, mv_t_ref, g_t_ref, gcum_t_ref, c_ref, m_ref)

    @pl.when(i == 0)
    def _():
        ml_ref[...] = jnp.zeros(ml_ref.shape, BF16)
        c_ref[...] = jnp.zeros(c_ref.shape, F32)
        m_ref[...] = jnp.zeros(m_ref.shape, F32)

    mix = jnp.concatenate([attn_ref[...], ml_ref[1 - par]], axis=1)
    first = _mlstm_begin(*ml_args, chunks[0], reset)
    y = x_ref[...] + jnp.dot(mix, w_out_ref[...], preferred_element_type=F32)
    h2 = _rms(y, g_ffn_ref[...]).astype(BF16)
    _mlstm_weights(first)
    _mlstm_matmuls(first, c_ref, m_ref)
    gate = jnp.dot(h2, w_gate_ref[...], preferred_element_type=F32)
    ml_ref[par, chunks[0], :] = _mlstm_output(first, mo_t_ref, ng_ref, chunks[0])
    second = _mlstm_begin(*ml_args, chunks[1], None)
    up = jnp.dot(h2, w_up_ref[...], preferred_element_type=F32)
    _mlstm_weights(second)
    _mlstm_matmuls(second, c_ref, m_ref)
    act = ((gate * (1.0 / (1.0 + jnp.exp(-gate)))) * up).astype(BF16)
    y2 = y + jnp.dot(act, w_down_ref[...], preferred_element_type=F32)
    ml_ref[par, chunks[1], :] = _mlstm_output(second, mo_t_ref, ng_ref, chunks[1])
    out_ref[...] = _rms(y2, g_final_ref[...])


def _out_ffn(x2d, attn, mq_t, mk, mv_t, mo_t, g_t, gcum_t, gain,
             w_out, g_ffn, w_gate, w_up, w_down, g_final, seq_len):
    tokens = x2d.shape[0]
    rows = FFN_ROWS
    n_tiles = tokens // rows
    ffn_blk = lambda width: pl.BlockSpec((rows, width), lambda i: (jnp.maximum(i - 1, 0), 0))
    ml_row = lambda width: pl.BlockSpec((rows, width), lambda i: (jnp.minimum(i, n_tiles - 1), 0))
    ml_col = lambda height: pl.BlockSpec((height, rows), lambda i: (0, jnp.minimum(i, n_tiles - 1)))
    return pl.pallas_call(
        functools.partial(_out_ffn_kernel, seq_len // rows, n_tiles),
        grid=(n_tiles + 1,),
        in_specs=[
            ffn_blk(D_MODEL), ffn_blk(DA_WIDTH),
            ml_col(ML_WIDTH), ml_row(ML_WIDTH), ml_col(ML_HEADS * V_EXT_ROWS), ml_col(ML_WIDTH),
            ml_col(2 * ML_HEADS), ml_col(2 * ML_HEADS),
            _const_spec(gain.shape),
            _const_spec(w_out.shape), _const_spec(g_ffn.shape),
            _const_spec(w_gate.shape), _const_spec(w_up.shape), _const_spec(w_down.shape),
            _const_spec(g_final.shape),
        ],
        out_specs=ffn_blk(D_MODEL),
        out_shape=jax.ShapeDtypeStruct((tokens, D_MODEL), F32),
        scratch_shapes=[pltpu.VMEM((2, rows, ML_WIDTH), BF16),
                        pltpu.VMEM((ML_HEADS, V_EXT_ROWS, ML_DIM), F32),
                        pltpu.VMEM((ML_HEADS, 1, 1), F32)],
        compiler_params=_compiler_params(1),
        name="out_ffn",
    )(x2d, attn, mq_t, mk, mv_t, mo_t, g_t, gcum_t, gain,
      w_out, g_ffn, w_gate, w_up, w_down, g_final)


def kernel(x, positions, mix_norm_g, w_in, da_lambda, da_subln_g, ml_conv_w, ml_conv_b, ml_gate_b,
           ml_norm_g, w_out, ffn_norm_g, w_gate, w_up, w_down, final_norm_g):
    batch, seq_len, _ = x.shape
    tokens = batch * seq_len
    depth = w_in.shape[0]
    assert depth == 1, "one trunk layer"
    assert seq_len % PROJ_ROWS == 0 and seq_len % ATTN_Q == 0 and seq_len % ML_CHUNK == 0
    assert ATTN_Q % ATTN_K == 0 and ATTN_K % CHUNK == 0 and tokens % FFN_ROWS == 0
    assert 2 * DA_QK_DIM + MASK_FEATS <= V7X_MXU_DEPTH and MASK_FEATS <= 2 * DA_QK_DIM
    assert DA_V_DIM == ML_DIM and DA_HEADS == ML_HEADS and FFN_ROWS == 2 * ML_CHUNK

    x2d = x.reshape(tokens, D_MODEL)
    pos3d = positions.reshape(tokens // PROJ_ROWS, 1, PROJ_ROWS)

    w = w_in[0]
    ml_v = OFF_ML + 2 * ML_WIDTH
    w_t = jnp.concatenate([w[:, OFF_DA_Q:OFF_ML], w[:, ml_v:OFF_GATE]], axis=1).T.astype(BF16)
    gate_cols = w[:, OFF_GATE:OFF_GATE + 2 * ML_HEADS]
    wg_t = jnp.pad(gate_cols.T, ((0, 2 * ML_HEADS), (0, 0))).astype(BF16)
    w_row = w[:, OFF_ML:ml_v].astype(BF16)
    inv_freq = (ROPE_THETA ** (-jnp.arange(0, ROT_DIM, 2, dtype=F32) / ROT_DIM)).reshape(ROT_HALF, 1)
    gb_col = ml_gate_b[0].astype(F32).reshape(2 * ML_HEADS, 1)

    q_t, k, v_t, mq_t, mk, mv_t, mo_t, g_t, gcum_t = _in_proj(
        x2d, pos3d, mix_norm_g[0].reshape(1, D_MODEL).astype(F32), w_t, wg_t, w_row, inv_freq,
        ml_conv_w[0].astype(F32), ml_conv_b[0].reshape(1, 2 * ML_WIDTH).astype(F32),
        gb_col, seq_len)

    attn = _diff_attn(da_lambda[0].astype(F32), q_t, k, v_t,
                      da_subln_g[0].astype(F32).reshape(DA_V_DIM, 1), batch, seq_len)
    ml_gain = jnp.broadcast_to(ml_norm_g[0].astype(F32).reshape(ML_WIDTH, 1), (ML_WIDTH, ML_CHUNK))

    out = _out_ffn(x2d, attn, mq_t, mk, mv_t, mo_t, g_t, gcum_t, ml_gain, w_out[0].astype(BF16),
                   ffn_norm_g[0].reshape(1, D_MODEL).astype(F32),
                   w_gate[0].astype(BF16), w_up[0].astype(BF16), w_down[0].astype(BF16),
                   final_norm_g.reshape(1, D_MODEL).astype(F32), seq_len)
    return out.reshape(batch, seq_len, D_MODEL)
```

```python
import functools
import math

import jax
import jax.numpy as jnp
from jax import lax
from jax.experimental import pallas as pl
from jax.experimental.pallas import tpu as pltpu

F32 = jnp.float32
BF16 = jnp.bfloat16

D_MODEL = 1024
CHUNK = 64
NORM_EPS = 1e-6
DA_HEADS = 4
DA_QK_DIM = 64
DA_V_DIM = 128
DA_WIDTH = DA_HEADS * DA_V_DIM
ROPE_THETA = 500000.0
ROT_DIM = DA_QK_DIM // 4
ROT_HALF = ROT_DIM // 2
ML_HEADS = 4
ML_DIM = 128
ML_WIDTH = ML_HEADS * ML_DIM
CONV_WIDTH = 4
D_FF = 2816
LAM_INIT = 0.8 - 0.6 * math.exp(-0.3 * 0)
Q_SCALE = DA_QK_DIM ** -0.5 * math.log2(math.e)

OFF_DA_Q = 0
OFF_DA_K = 512
OFF_DA_V = 1024
OFF_ML = 1536
OFF_GATE = 3584
ZT_ML_V = 1536
ZT_ML_O = 2048

V7X_LANES = 128
V7X_SUBLANES = 8
V7X_BF16_ROWS_PER_VREG = 16
V7X_MXU_DEPTH = 256
V7X_VMEM_LIMIT_BYTES = 56 * 1024 * 1024

PROJ_ROWS = 512
ATTN_Q = 1024
ATTN_K = 512
MASK_FEATS = ATTN_Q // CHUNK
MASK_BIG = 1e30
V_EXT_ROWS = DA_V_DIM + V7X_BF16_ROWS_PER_VREG
ML_CHUNK = 256
FFN_ROWS = 512
CONV_HALO = V7X_SUBLANES

_NT = (((1,), (1,)), ((), ()))


def _compiler_params(n_axes):
    return pltpu.CompilerParams(
        dimension_semantics=("arbitrary",) * n_axes,
        vmem_limit_bytes=V7X_VMEM_LIMIT_BYTES,
    )


def _const_spec(shape):
    zeros = (0,) * len(shape)
    return pl.BlockSpec(shape, lambda *_: zeros, pipeline_mode=pl.Buffered(1))


def _rope_rows(zt, cos, sin):
    pieces = []
    for g in range(2):
        base = g * DA_QK_DIM
        x1 = zt[base:base + ROT_HALF]
        x2 = zt[base + ROT_HALF:base + ROT_DIM]
        pieces += [x1 * cos - x2 * sin, x2 * cos + x1 * sin, zt[base + ROT_DIM:base + DA_QK_DIM]]
    return jnp.concatenate(pieces, axis=0)


def _in_proj_kernel(tiles_per_seq,
                    x_ref, pos_ref, g_ref, w_t_ref, wg_t_ref, w_row_ref, invf_ref,
                    convw_ref, convb_ref, gb_col_ref,
                    q_t_ref, k_ref, v_t_ref, mq_t_ref, mk_ref, mv_t_ref, mo_t_ref, g_t_ref, gcum_t_ref,
                    halo_ref):
    rows = x_ref.shape[0]
    x = x_ref[...]
    ms = jnp.mean(x * x, axis=-1, keepdims=True)
    hb = ((x * lax.rsqrt(ms + NORM_EPS)) * g_ref[...]).astype(BF16)

    first = (pl.program_id(0) % tiles_per_seq) == 0
    halo_ref[0:CONV_HALO, :] = jnp.where(first, 0.0, halo_ref[rows:rows + CONV_HALO, :])
    halo_ref[CONV_HALO:CONV_HALO + rows, :] = jnp.dot(hb, w_row_ref[...], preferred_element_type=F32)

    def proj_t(lo, hi):
        return lax.dot_general(w_t_ref[lo:hi, :], hb, _NT, preferred_element_type=F32)

    zqk_t = proj_t(OFF_DA_Q, OFF_DA_V)

    xe = halo_ref[...]
    conv = convw_ref[0:1, :] * xe
    for j in range(1, CONV_WIDTH):
        conv = pltpu.roll(conv, 1, axis=0) + convw_ref[j:j + 1, :] * xe
    conv = conv[CONV_HALO:, :] + convb_ref[...]
    act = conv * (1.0 / (1.0 + jnp.exp(-conv)))
    mq_t_ref[...] = (act[:, 0:ML_WIDTH] * (ML_DIM ** -0.5)).T.astype(BF16)
    mk_ref[...] = act[:, ML_WIDTH:2 * ML_WIDTH].astype(BF16)

    ang = invf_ref[...] * pos_ref[0].astype(F32)
    cos = jnp.cos(ang)
    sin = jnp.sin(ang)
    for h in range(DA_HEADS):
        lo = h * 2 * DA_QK_DIM
        hi = lo + 2 * DA_QK_DIM
        q_rot = _rope_rows(zqk_t[OFF_DA_Q + lo:OFF_DA_Q + hi], cos, sin)
        q_t_ref[lo:hi, :] = (q_rot * Q_SCALE).astype(BF16)
        k_rot = _rope_rows(zqk_t[OFF_DA_K + lo:OFF_DA_K + hi], cos, sin)
        k_ref[:, lo:hi] = k_rot.T.astype(BF16)
    pad_rows = V_EXT_ROWS - DA_V_DIM
    ones_row = (lax.broadcasted_iota(jnp.int32, (pad_rows, rows), 0) == 0).astype(BF16)
    for out_ref, base in ((v_t_ref, OFF_DA_V), (mv_t_ref, ZT_ML_V)):
        zv_t = proj_t(base, base + DA_WIDTH)
        for h in range(DA_HEADS):
            v_lo = h * DA_V_DIM
            out_ref[h * V_EXT_ROWS:h * V_EXT_ROWS + DA_V_DIM, :] = zv_t[v_lo:v_lo + DA_V_DIM].astype(BF16)
            out_ref[h * V_EXT_ROWS + DA_V_DIM:(h + 1) * V_EXT_ROWS, :] = ones_row
    mo_t_ref[...] = proj_t(ZT_ML_O, ZT_ML_O + ML_WIDTH).astype(BF16)

    gz = lax.dot_general(wg_t_ref[...], hb, _NT, preferred_element_type=F32)
    a_t = gz[0:2 * ML_HEADS] + gb_col_ref[...]
    ls_t = jnp.minimum(a_t, 0.0) - jnp.log1p(jnp.exp(-jnp.abs(a_t)))
    row_id = lax.broadcasted_iota(jnp.int32, a_t.shape, 0)
    log_gates = jnp.where(row_id < ML_HEADS, a_t, ls_t)
    g_t_ref[...] = log_gates
    src = lax.broadcasted_iota(jnp.int32, (rows, rows), 0)
    dst = lax.broadcasted_iota(jnp.int32, (rows, rows), 1)
    tri = ((src <= dst) & (src // ML_CHUNK == dst // ML_CHUNK)).astype(F32)
    gcum_t_ref[...] = jnp.dot(log_gates, tri, preferred_element_type=F32,
                              precision=lax.Precision.HIGHEST)


def _in_proj(x2d, pos3d, norm_g, w_t, wg_t, w_row, inv_freq, conv_w, conv_b, gb_col, seq_len):
    tokens = x2d.shape[0]
    rows = PROJ_ROWS
    n_tiles = tokens // rows
    row_blk = lambda width: pl.BlockSpec((rows, width), lambda i: (i, 0))
    col_blk = lambda height: pl.BlockSpec((height, rows), lambda i: (0, i))
    out_shape = (
        jax.ShapeDtypeStruct((DA_WIDTH, tokens), BF16),
        jax.ShapeDtypeStruct((tokens, DA_WIDTH), BF16),
        jax.ShapeDtypeStruct((DA_HEADS * V_EXT_ROWS, tokens), BF16),
        jax.ShapeDtypeStruct((ML_WIDTH, tokens), BF16),
        jax.ShapeDtypeStruct((tokens, ML_WIDTH), BF16),
        jax.ShapeDtypeStruct((ML_HEADS * V_EXT_ROWS, tokens), BF16),
        jax.ShapeDtypeStruct((ML_WIDTH, tokens), BF16),
        jax.ShapeDtypeStruct((2 * ML_HEADS, tokens), F32),
        jax.ShapeDtypeStruct((2 * ML_HEADS, tokens), F32),
    )
    return pl.pallas_call(
        functools.partial(_in_proj_kernel, seq_len // rows),
        grid=(n_tiles,),
        in_specs=[
            row_blk(D_MODEL),
            pl.BlockSpec((1, 1, rows), lambda i: (i, 0, 0)),
            _const_spec(norm_g.shape),
            _const_spec(w_t.shape),
            _const_spec(wg_t.shape),
            _const_spec(w_row.shape),
            _const_spec(inv_freq.shape),
            _const_spec(conv_w.shape),
            _const_spec(conv_b.shape),
            _const_spec(gb_col.shape),
        ],
        out_specs=(
            col_blk(DA_WIDTH), row_blk(DA_WIDTH), col_blk(DA_HEADS * V_EXT_ROWS),
            col_blk(ML_WIDTH), row_blk(ML_WIDTH), col_blk(ML_HEADS * V_EXT_ROWS), col_blk(ML_WIDTH),
            col_blk(2 * ML_HEADS), col_blk(2 * ML_HEADS),
        ),
        out_shape=out_shape,
        scratch_shapes=[pltpu.VMEM((rows + CONV_HALO, 2 * ML_WIDTH), F32)],
        compiler_params=_compiler_params(1),
        name="in_proj",
    )(x2d, pos3d, norm_g, w_t, wg_t, w_row, inv_freq, conv_w, conv_b, gb_col)


def _attn_kernel(lam_ref, q_t_ref, k_ref, v_t_ref, g_ref, o_ref,
                 qm_ref, ind_ref, s_ref, p_ref, acc_ref):
    seq = k_ref.shape[0]
    tq, tk = ATTN_Q, ATTN_K
    n_q = seq // tq
    diag_tiles = tq // tk

    lv = lam_ref[...]
    lam = (jnp.exp(jnp.sum(lv[0:1] * lv[1:2], axis=1, keepdims=True))
           - jnp.exp(jnp.sum(lv[2:3] * lv[3:4], axis=1, keepdims=True)) + LAM_INIT)

    feat = lax.broadcasted_iota(jnp.int32, (MASK_FEATS, 2 * tq), 0)
    qchunk = (lax.broadcasted_iota(jnp.int32, (MASK_FEATS, 2 * tq), 1) % tq) // CHUNK
    qm_ref[...] = jnp.zeros(qm_ref.shape, BF16)
    qm_ref[2 * DA_QK_DIM:2 * DA_QK_DIM + MASK_FEATS, :] = jnp.where(
        feat > qchunk, -MASK_BIG, 0.0).astype(BF16)
    kchunk = lax.broadcasted_iota(jnp.int32, (tk, 2 * DA_QK_DIM), 0) // CHUNK
    lane = lax.broadcasted_iota(jnp.int32, (tk, 2 * DA_QK_DIM), 1)
    ind_ref[0] = jnp.zeros((tk, 2 * DA_QK_DIM), BF16)
    for d in range(diag_tiles):
        ind_ref[d + 1] = (lane == kchunk + d * (tk // CHUNK)).astype(BF16)

    def q_tile(i, first_tile):
        q_off = pl.multiple_of(i * tq, tq)
        t_diag = diag_tiles * i
        n_t = t_diag + diag_tiles

        def score(par, t, off=0, which=None):
            k_t = k_ref[pl.ds(pl.multiple_of(t * tk, tk), tk), :]
            if which is None:
                which = jnp.maximum(t - t_diag + 1, 0)
            k_ext = jnp.concatenate([k_t, ind_ref[which]], axis=1)
            if off:
                rhs = jnp.concatenate([qm_ref[:, off:tq], qm_ref[:, tq + off:2 * tq]], axis=1)
            else:
                rhs = qm_ref[...]
            w = tq - off
            s = jnp.dot(k_ext, rhs, preferred_element_type=F32)
            s_ref[par, :, 0:2 * w] = s
            return tuple(jnp.max(s[:, mi * w:(mi + 1) * w], axis=0, keepdims=True) for mi in range(2))

        def softmax(par, maxes, tile_max, off=0):
            w = tq - off
            new, alphas = [], []
            for mi in range(2):
                lanes = slice(mi * w, (mi + 1) * w)
                s = s_ref[par, :, lanes]
                m_old = maxes[mi][:, off:tq]
                m_new = jnp.maximum(m_old, tile_max[mi])
                p_ref[par, :, lanes] = jnp.exp2(s - m_new).astype(BF16)
                alphas.append(jnp.exp2(m_old - m_new))
                new.append(jnp.concatenate([maxes[mi][:, 0:off], m_new], axis=1) if off else m_new)
            return tuple(new), tuple(alphas)

        def accumulate(par, t, alphas, off=0):
            w = tq - off
            v_t = v_t_ref[:, pl.ds(pl.multiple_of(t * tk, tk), tk)]
            pv = jnp.dot(v_t, p_ref[par, :, 0:2 * w], preferred_element_type=F32)
            for mi in range(2):
                lanes = slice(mi * tq + off, (mi + 1) * tq)
                acc_ref[:, lanes] = alphas[mi] * acc_ref[:, lanes] + pv[:, mi * w:(mi + 1) * w]

        def step(par, t, carry, first=False):
            maxes, alphas, tile_max = carry
            next_max = score(1 - par, t + 1)
            if not first:
                accumulate(1 - par, t - 1, alphas)
            return softmax(par, maxes, tile_max) + (next_max,)

        q_t = q_t_ref[:, pl.ds(q_off, tq)]
        qm_ref[0:DA_QK_DIM, 0:tq] = q_t[0:DA_QK_DIM]
        qm_ref[DA_QK_DIM:2 * DA_QK_DIM, tq:2 * tq] = q_t[DA_QK_DIM:]
        tile_max = score(0, 0)
        if not first_tile:
            finish(i - 1)
        acc_ref[...] = jnp.zeros(acc_ref.shape, F32)

        neg = jnp.full((1, tq), -jnp.inf, F32)
        maxes, alphas = (neg, neg), None
        if not first_tile:
            carry = step(0, 0, (maxes, alphas, tile_max), first=True)

            def pair(u, carry):
                carry = step(1, 2 * u + 1, carry)
                return step(0, 2 * u + 2, carry)

            carry = lax.fori_loop(0, t_diag // 2 - 1, pair, carry)
            maxes, alphas, tile_max = step(1, t_diag - 1, carry)
        for d in range(diag_tiles):
            t, par = t_diag + d, d % 2
            if d + 1 < diag_tiles:
                next_max = score(1 - par, t + 1, off=(d + 1) * tk, which=d + 2)
            if d or not first_tile:
                accumulate(1 - par, t - 1, alphas, off=max(d - 1, 0) * tk)
            maxes, alphas = softmax(par, maxes, tile_max, off=d * tk)
            tile_max = next_max
        accumulate((diag_tiles - 1) % 2, n_t - 1, alphas, off=(diag_tiles - 1) * tk)
        return 0

    def finish(i):
        acc = acc_ref[...]
        o1 = acc[0:DA_V_DIM, 0:tq] / acc[DA_V_DIM:DA_V_DIM + 1, 0:tq]
        o2 = acc[0:DA_V_DIM, tq:2 * tq] / acc[DA_V_DIM:DA_V_DIM + 1, tq:2 * tq]
        o_t = o1 - lam * o2
        ms = jnp.mean(o_t * o_t, axis=0, keepdims=True)
        y_t = (o_t * lax.rsqrt(ms + NORM_EPS)) * g_ref[...] * (1.0 - LAM_INIT)
        o_ref[pl.ds(pl.multiple_of(i * tq, tq), tq), :] = y_t.T.astype(BF16)

    q_tile(jnp.int32(0), True)
    lax.fori_loop(1, n_q, lambda i, c: q_tile(i, False), 0)
    finish(jnp.int32(n_q - 1))


def _diff_attn(lam_params, q_t, k, v_t, subln_col, batch, seq_len):
    tokens = k.shape[0]
    tq, tk = ATTN_Q, ATTN_K
    return pl.pallas_call(
        _attn_kernel,
        grid=(batch, DA_HEADS),
        in_specs=[
            _const_spec(lam_params.shape),
            pl.BlockSpec((DA_V_DIM, seq_len), lambda b, h: (h, b)),
            pl.BlockSpec((seq_len, DA_V_DIM), lambda b, h: (b, h)),
            pl.BlockSpec((V_EXT_ROWS, seq_len), lambda b, h: (h, b)),
            _const_spec(subln_col.shape),
        ],
        out_specs=pl.BlockSpec((seq_len, DA_V_DIM), lambda b, h: (b, h)),
        out_shape=jax.ShapeDtypeStruct((tokens, DA_WIDTH), BF16),
        scratch_shapes=[
            pltpu.VMEM((V7X_MXU_DEPTH, 2 * tq), BF16),
            pltpu.VMEM((1 + tq // tk, tk, 2 * DA_QK_DIM), BF16),
            pltpu.VMEM((2, tk, 2 * tq), F32),
            pltpu.VMEM((2, tk, 2 * tq), BF16),
            pltpu.VMEM((V_EXT_ROWS, 2 * tq), F32),
        ],
        compiler_params=_compiler_params(2),
        name="diff_attn",
    )(lam_params, q_t, k, v_t, subln_col)


def _mlstm_begin(q_t_ref, k_ref, v_t_ref, g_t_ref, gcum_t_ref, c_ref, m_ref, cols, reset):
    chunk = cols.stop - cols.start
    heads = range(ML_HEADS)
    g_t = g_t_ref[:, cols]
    cum_row = gcum_t_ref[:, cols]
    key_rows = g_t - pltpu.roll(cum_row, ML_HEADS, axis=0)
    key_cols = jnp.concatenate(
        [key_rows, jnp.zeros((V7X_LANES - 2 * ML_HEADS, chunk), F32)], axis=0).T
    st = dict(
        chunk=chunk,
        q_t=[q_t_ref[h * ML_DIM:(h + 1) * ML_DIM, cols] for h in heads],
        k=[k_ref[cols, h * ML_DIM:(h + 1) * ML_DIM] for h in heads],
        v_t=[v_t_ref[h * V_EXT_ROWS:(h + 1) * V_EXT_ROWS, cols] for h in heads],
        c_t=[c_ref[h] for h in heads],
        m_prev=[m_ref[h] for h in heads],
        b_row=[cum_row[ML_HEADS + h:ML_HEADS + h + 1, :] for h in heads],
        i_row=[g_t[h:h + 1, :] for h in heads],
        e_mat=[jnp.broadcast_to(key_cols[:, h:h + 1], (chunk, chunk)) for h in heads],
    )
    if reset is not None:
        st["c_t"] = [jnp.where(reset, 0.0, c) for c in st["c_t"]]
        st["m_prev"] = [jnp.where(reset, 0.0, m) for m in st["m_prev"]]
    st["kq"] = [jnp.dot(st["k"][h], st["q_t"][h], preferred_element_type=F32) for h in heads]
    st["inter_mm"] = [jnp.dot(st["c_t"][h].astype(BF16), st["q_t"][h], preferred_element_type=F32)
                      for h in heads]
    return st


def _mlstm_weights(st):
    chunk = st["chunk"]
    s_id = lax.broadcasted_iota(jnp.int32, (chunk, chunk), 0)
    t_id = lax.broadcasted_iota(jnp.int32, (chunk, chunk), 1)
    causal = s_id <= t_id
    st["m_row"], st["w_inter"], st["sc"] = [], [], []
    for h in range(ML_HEADS):
        d_mat = jnp.where(causal, st["e_mat"][h] + st["b_row"][h], -jnp.inf)
        inter = st["b_row"][h] + st["m_prev"][h]
        m_row = jnp.maximum(inter, jnp.max(d_mat, axis=0, keepdims=True))
        st["m_row"].append(m_row)
        st["w_inter"].append(jnp.exp(inter - m_row))
        st["sc"].append((st["kq"][h] * jnp.exp(d_mat - m_row)).astype(BF16))


def _mlstm_matmuls(st, c_ref, m_ref):
    chunk = st["chunk"]
    heads = range(ML_HEADS)
    st["intra_mm"] = [jnp.dot(st["v_t"][h], st["sc"][h], preferred_element_type=F32) for h in heads]
    for h in heads:
        b_row, m_prev = st["b_row"][h], st["m_prev"][h]
        b_last = b_row[:, chunk - 1:chunk]
        g_row = (b_last - b_row) + st["i_row"][h]
        m_new = jnp.maximum(b_last + m_prev, jnp.max(g_row, axis=1, keepdims=True))
        decay = jnp.exp(b_last + m_prev - m_new)
        wk = jnp.exp(st["e_mat"][h][:, 0:ML_DIM] + (b_last - m_new))
        kw = (st["k"][h].astype(F32) * wk).astype(BF16)
        c_ref[h] = decay * st["c_t"][h] + jnp.dot(st["v_t"][h], kw, preferred_element_type=F32)
        m_ref[h] = m_new


def _mlstm_output(st, o_t_ref, ng_ref, cols):
    outs = []
    for h in range(ML_HEADS):
        num = st["intra_mm"][h] + st["w_inter"][h] * st["inter_mm"][h]
        nq = num[ML_DIM:ML_DIM + 1, :]
        hid = num[0:ML_DIM, :] / jnp.maximum(jnp.abs(nq), jnp.exp(-st["m_row"][h]))
        ms = jnp.mean(hid * hid, axis=0, keepdims=True)
        hn = (hid * lax.rsqrt(ms + NORM_EPS)) * ng_ref[h * ML_DIM:(h + 1) * ML_DIM, :]
        og = o_t_ref[h * ML_DIM:(h + 1) * ML_DIM, cols].astype(F32)
        outs.append((hn * (1.0 / (1.0 + jnp.exp(-og)))).T.astype(BF16))
    return jnp.concatenate(outs, axis=1)


def _rms(x, g):
    ms = jnp.mean(x * x, axis=-1, keepdims=True)
    return (x * lax.rsqrt(ms + NORM_EPS)) * g


def _out_ffn_kernel(tiles_per_seq, n_tiles,
                    x_ref, attn_ref, mq_t_ref, mk_ref, mv_t_ref, mo_t_ref, g_t_ref, gcum_t_ref, ng_ref,
                    w_out_ref, g_ffn_ref, w_gate_ref, w_up_ref, w_down_ref, g_final_ref,
                    out_ref, ml_ref, c_ref, m_ref):
    i = pl.program_id(0)
    reset = (jnp.minimum(i, n_tiles - 1) % tiles_per_seq) == 0
    chunks = [slice(c * ML_CHUNK, (c + 1) * ML_CHUNK) for c in range(FFN_ROWS // ML_CHUNK)]
    ml_args = (mq_t_ref, mk_ref, mv_t_ref, g_t_ref, gcum_t_ref, c_ref, m_ref)

    @pl.when(i == 0)
    def _():
        ml_ref[...] = jnp.zeros(ml_ref.shape, BF16)
        c_ref[...] = jnp.zeros(c_ref.shape, F32)
        m_ref[...] = jnp.zeros(m_ref.shape, F32)

    halves = chunks
    mix = jnp.concatenate([attn_ref[...], ml_ref[...]], axis=1)
    first = _mlstm_begin(*ml_args, chunks[0], reset)
    y = [x_ref[hs, :] + jnp.dot(mix[hs], w_out_ref[...], preferred_element_type=F32) for hs in halves]
    h2 = [_rms(yh, g_ffn_ref[...]).astype(BF16) for yh in y]
    _mlstm_weights(first)
    _mlstm_matmuls(first, c_ref, m_ref)
    gate = [jnp.dot(h, w_gate_ref[...], preferred_element_type=F32) for h in h2]
    ml_ref[chunks[0], :] = _mlstm_output(first, mo_t_ref, ng_ref, chunks[0])
    second = _mlstm_begin(*ml_args, chunks[1], None)
    up = [jnp.dot(h, w_up_ref[...], preferred_element_type=F32) for h in h2]
    _mlstm_weights(second)
    _mlstm_matmuls(second, c_ref, m_ref)
    act = [((g * (1.0 / (1.0 + jnp.exp(-g)))) * u).astype(BF16) for g, u in zip(gate, up)]
    y2 = [yh + jnp.dot(a, w_down_ref[...], preferred_element_type=F32) for yh, a in zip(y, act)]
    ml_ref[chunks[1], :] = _mlstm_output(second, mo_t_ref, ng_ref, chunks[1])
    for hs, yh in zip(halves, y2):
        out_ref[hs, :] = _rms(yh, g_final_ref[...])


def _out_ffn(x2d, attn, mq_t, mk, mv_t, mo_t, g_t, gcum_t, gain,
             w_out, g_ffn, w_gate, w_up, w_down, g_final, seq_len):
    tokens = x2d.shape[0]
    rows = FFN_ROWS
    n_tiles = tokens // rows
    ffn_blk = lambda width: pl.BlockSpec((rows, width), lambda i: (jnp.maximum(i - 1, 0), 0))
    ml_row = lambda width: pl.BlockSpec((rows, width), lambda i: (jnp.minimum(i, n_tiles - 1), 0))
    ml_col = lambda height: pl.BlockSpec((height, rows), lambda i: (0, jnp.minimum(i, n_tiles - 1)))
    return pl.pallas_call(
        functools.partial(_out_ffn_kernel, seq_len // rows, n_tiles),
        grid=(n_tiles + 1,),
        in_specs=[
            ffn_blk(D_MODEL), ffn_blk(DA_WIDTH),
            ml_col(ML_WIDTH), ml_row(ML_WIDTH), ml_col(ML_HEADS * V_EXT_ROWS), ml_col(ML_WIDTH),
            ml_col(2 * ML_HEADS), ml_col(2 * ML_HEADS),
            _const_spec(gain.shape),
            _const_spec(w_out.shape), _const_spec(g_ffn.shape),
            _const_spec(w_gate.shape), _const_spec(w_up.shape), _const_spec(w_down.shape),
            _const_spec(g_final.shape),
        ],
        out_specs=ffn_blk(D_MODEL),
        out_shape=jax.ShapeDtypeStruct((tokens, D_MODEL), F32),
        scratch_shapes=[pltpu.VMEM((rows, ML_WIDTH), BF16),
                        pltpu.VMEM((ML_HEADS, V_EXT_ROWS, ML_DIM), F32),
                        pltpu.VMEM((ML_HEADS, 1, 1), F32)],
        compiler_params=_compiler_params(1),
        name="out_ffn",
    )(x2d, attn, mq_t, mk, mv_t, mo_t, g_t, gcum_t, gain,
      w_out, g_ffn, w_gate, w_up, w_down, g_final)


def kernel(x, positions, mix_norm_g, w_in, da_lambda, da_subln_g, ml_conv_w, ml_conv_b, ml_gate_b,
           ml_norm_g, w_out, ffn_norm_g, w_gate, w_up, w_down, final_norm_g):
    batch, seq_len, _ = x.shape
    tokens = batch * seq_len
    depth = w_in.shape[0]
    assert depth == 1, "one trunk layer"
    assert seq_len % PROJ_ROWS == 0 and seq_len % ATTN_Q == 0 and seq_len % ML_CHUNK == 0
    assert ATTN_Q % ATTN_K == 0 and ATTN_K % CHUNK == 0 and tokens % FFN_ROWS == 0
    assert 2 * DA_QK_DIM + MASK_FEATS <= V7X_MXU_DEPTH and MASK_FEATS <= 2 * DA_QK_DIM
    assert DA_V_DIM == ML_DIM and DA_HEADS == ML_HEADS and FFN_ROWS == 2 * ML_CHUNK

    x2d = x.reshape(tokens, D_MODEL)
    pos3d = positions.reshape(tokens // PROJ_ROWS, 1, PROJ_ROWS)

    w = w_in[0]
    ml_v = OFF_ML + 2 * ML_WIDTH
    w_t = jnp.concatenate([w[:, OFF_DA_Q:OFF_ML], w[:, ml_v:OFF_GATE]], axis=1).T.astype(BF16)
    gate_cols = w[:, OFF_GATE:OFF_GATE + 2 * ML_HEADS]
    wg_t = jnp.pad(gate_cols.T, ((0, 2 * ML_HEADS), (0, 0))).astype(BF16)
    w_row = w[:, OFF_ML:ml_v].astype(BF16)
    inv_freq = (ROPE_THETA ** (-jnp.arange(0, ROT_DIM, 2, dtype=F32) / ROT_DIM)).reshape(ROT_HALF, 1)
    gb_col = ml_gate_b[0].astype(F32).reshape(2 * ML_HEADS, 1)

    q_t, k, v_t, mq_t, mk, mv_t, mo_t, g_t, gcum_t = _in_proj(
        x2d, pos3d, mix_norm_g[0].reshape(1, D_MODEL).astype(F32), w_t, wg_t, w_row, inv_freq,
        ml_conv_w[0].astype(F32), ml_conv_b[0].reshape(1, 2 * ML_WIDTH).astype(F32),
        gb_col, seq_len)

    attn = _diff_attn(da_lambda[0].astype(F32), q_t, k, v_t,
                      da_subln_g[0].astype(F32).reshape(DA_V_DIM, 1), batch, seq_len)
    ml_gain = jnp.broadcast_to(ml_norm_g[0].astype(F32).reshape(ML_WIDTH, 1), (ML_WIDTH, ML_CHUNK))

    out = _out_ffn(x2d, attn, mq_t, mk, mv_t, mo_t, g_t, gcum_t, ml_gain, w_out[0].astype(BF16),
                   ffn_norm_g[0].reshape(1, D_MODEL).astype(F32),
                   w_gate[0].astype(BF16), w_up[0].astype(BF16), w_down[0].astype(BF16),
                   final_norm_g.reshape(1, D_MODEL).astype(F32), seq_len)
    return out.reshape(batch, seq_len, D_MODEL)
```

```python
import functools
import math

import jax
import jax.numpy as jnp
from jax import lax
from jax.experimental import pallas as pl
from jax.experimental.pallas import tpu as pltpu

F32 = jnp.float32
BF16 = jnp.bfloat16

D_MODEL = 1024
CHUNK = 64
NORM_EPS = 1e-6
DA_HEADS = 4
DA_QK_DIM = 64
DA_V_DIM = 128
DA_WIDTH = DA_HEADS * DA_V_DIM
ROPE_THETA = 500000.0
ROT_DIM = DA_QK_DIM // 4
ROT_HALF = ROT_DIM // 2
ML_HEADS = 4
ML_DIM = 128
ML_WIDTH = ML_HEADS * ML_DIM
CONV_WIDTH = 4
D_FF = 2816
LAM_INIT = 0.8 - 0.6 * math.exp(-0.3 * 0)
Q_SCALE = DA_QK_DIM ** -0.5 * math.log2(math.e)

OFF_DA_Q = 0
OFF_DA_K = 512
OFF_DA_V = 1024
OFF_ML = 1536
OFF_GATE = 3584
ZT_ML_V = 1536
ZT_ML_O = 2048
ZT_GATE = 2560

V7X_LANES = 128
V7X_SUBLANES = 8
V7X_BF16_ROWS_PER_VREG = 16
V7X_MXU_DEPTH = 256
V7X_VMEM_LIMIT_BYTES = 56 * 1024 * 1024

PROJ_ROWS = 1024
ATTN_Q = 1024
ATTN_K = 512
MASK_FEATS = ATTN_Q // CHUNK
MASK_BIG = 1e30
V_EXT_ROWS = DA_V_DIM + V7X_BF16_ROWS_PER_VREG
ML_CHUNK = 256
FFN_ROWS = 512
CONV_HALO = V7X_SUBLANES

_NT = (((1,), (1,)), ((), ()))


def _compiler_params(n_axes):
    return pltpu.CompilerParams(
        dimension_semantics=("arbitrary",) * n_axes,
        vmem_limit_bytes=V7X_VMEM_LIMIT_BYTES,
    )


def _const_spec(shape):
    zeros = (0,) * len(shape)
    return pl.BlockSpec(shape, lambda *_: zeros, pipeline_mode=pl.Buffered(1))


def _rope_rows(zt, cos, sin):
    pieces = []
    for g in range(2):
        base = g * DA_QK_DIM
        x1 = zt[base:base + ROT_HALF]
        x2 = zt[base + ROT_HALF:base + ROT_DIM]
        pieces += [x1 * cos - x2 * sin, x2 * cos + x1 * sin, zt[base + ROT_DIM:base + DA_QK_DIM]]
    return jnp.concatenate(pieces, axis=0)


def _in_proj_kernel(tiles_per_seq,
                    x_ref, pos_ref, g_ref, w_t_ref, w_row_ref, invf_ref,
                    convw_ref, convb_ref, gb_col_ref,
                    q_t_ref, k_ref, v_t_ref, mq_t_ref, mk_ref, mv_t_ref, mo_t_ref, g_t_ref, gcum_t_ref,
                    halo_ref):
    rows = x_ref.shape[0]
    x = x_ref[...]
    ms = jnp.mean(x * x, axis=-1, keepdims=True)
    hb = ((x * lax.rsqrt(ms + NORM_EPS)) * g_ref[...]).astype(BF16)

    first = (pl.program_id(0) % tiles_per_seq) == 0
    halo_ref[0:CONV_HALO, :] = jnp.where(first, 0.0, halo_ref[rows:rows + CONV_HALO, :])
    halo_ref[CONV_HALO:CONV_HALO + rows, :] = jnp.dot(hb, w_row_ref[...], preferred_element_type=F32)

    def proj_t(lo, hi):
        return lax.dot_general(w_t_ref[lo:hi, :], hb, _NT, preferred_element_type=F32)

    zqk_t = proj_t(OFF_DA_Q, OFF_DA_V)

    xe = halo_ref[...]
    conv = convw_ref[0:1, :] * xe
    for j in range(1, CONV_WIDTH):
        conv = pltpu.roll(conv, 1, axis=0) + convw_ref[j:j + 1, :] * xe
    conv = conv[CONV_HALO:, :] + convb_ref[...]
    act = conv * (1.0 / (1.0 + jnp.exp(-conv)))
    mq_t_ref[...] = (act[:, 0:ML_WIDTH] * (ML_DIM ** -0.5)).T.astype(BF16)
    mk_ref[...] = act[:, ML_WIDTH:2 * ML_WIDTH].astype(BF16)

    ang = invf_ref[...] * pos_ref[0].astype(F32)
    cos = jnp.cos(ang)
    sin = jnp.sin(ang)
    for h in range(DA_HEADS):
        lo = h * 2 * DA_QK_DIM
        hi = lo + 2 * DA_QK_DIM
        q_rot = _rope_rows(zqk_t[OFF_DA_Q + lo:OFF_DA_Q + hi], cos, sin)
        q_t_ref[lo:hi, :] = (q_rot * Q_SCALE).astype(BF16)
        k_rot = _rope_rows(zqk_t[OFF_DA_K + lo:OFF_DA_K + hi], cos, sin)
        k_ref[:, lo:hi] = k_rot.T.astype(BF16)
    pad_rows = V_EXT_ROWS - DA_V_DIM
    ones_row = (lax.broadcasted_iota(jnp.int32, (pad_rows, rows), 0) == 0).astype(BF16)
    for out_ref, base in ((v_t_ref, OFF_DA_V), (mv_t_ref, ZT_ML_V)):
        zv_t = proj_t(base, base + DA_WIDTH)
        for h in range(DA_HEADS):
            v_lo = h * DA_V_DIM
            out_ref[h * V_EXT_ROWS:h * V_EXT_ROWS + DA_V_DIM, :] = zv_t[v_lo:v_lo + DA_V_DIM].astype(BF16)
            out_ref[h * V_EXT_ROWS + DA_V_DIM:(h + 1) * V_EXT_ROWS, :] = ones_row
    zo_t = proj_t(ZT_ML_O, ZT_GATE + V7X_BF16_ROWS_PER_VREG)
    mo_t_ref[...] = zo_t[0:ML_WIDTH].astype(BF16)

    a_t = zo_t[ML_WIDTH:ML_WIDTH + 2 * ML_HEADS] + gb_col_ref[...]
    ls_t = jnp.minimum(a_t, 0.0) - jnp.log1p(jnp.exp(-jnp.abs(a_t)))
    row_id = lax.broadcasted_iota(jnp.int32, a_t.shape, 0)
    log_gates = jnp.where(row_id < ML_HEADS, a_t, ls_t)
    g_t_ref[...] = log_gates
    src = lax.broadcasted_iota(jnp.int32, (rows, rows), 0)
    dst = lax.broadcasted_iota(jnp.int32, (rows, rows), 1)
    tri = ((src <= dst) & (src // ML_CHUNK == dst // ML_CHUNK)).astype(F32)
    gcum_t_ref[...] = jnp.dot(log_gates, tri, preferred_element_type=F32,
                              precision=lax.Precision.HIGHEST)


def _in_proj(x2d, pos3d, norm_g, w_t, w_row, inv_freq, conv_w, conv_b, gb_col, seq_len):
    tokens = x2d.shape[0]
    rows = PROJ_ROWS
    n_tiles = tokens // rows
    row_blk = lambda width: pl.BlockSpec((rows, width), lambda i: (i, 0))
    col_blk = lambda height: pl.BlockSpec((height, rows), lambda i: (0, i))
    out_shape = (
        jax.ShapeDtypeStruct((DA_WIDTH, tokens), BF16),
        jax.ShapeDtypeStruct((tokens, DA_WIDTH), BF16),
        jax.ShapeDtypeStruct((DA_HEADS * V_EXT_ROWS, tokens), BF16),
        jax.ShapeDtypeStruct((ML_WIDTH, tokens), BF16),
        jax.ShapeDtypeStruct((tokens, ML_WIDTH), BF16),
        jax.ShapeDtypeStruct((ML_HEADS * V_EXT_ROWS, tokens), BF16),
        jax.ShapeDtypeStruct((ML_WIDTH, tokens), BF16),
        jax.ShapeDtypeStruct((2 * ML_HEADS, tokens), F32),
        jax.ShapeDtypeStruct((2 * ML_HEADS, tokens), F32),
    )
    return pl.pallas_call(
        functools.partial(_in_proj_kernel, seq_len // rows),
        grid=(n_tiles,),
        in_specs=[
            row_blk(D_MODEL),
            pl.BlockSpec((1, 1, rows), lambda i: (i, 0, 0)),
            _const_spec(norm_g.shape),
            _const_spec(w_t.shape),
            _const_spec(w_row.shape),
            _const_spec(inv_freq.shape),
            _const_spec(conv_w.shape),
            _const_spec(conv_b.shape),
            _const_spec(gb_col.shape),
        ],
        out_specs=(
            col_blk(DA_WIDTH), row_blk(DA_WIDTH), col_blk(DA_HEADS * V_EXT_ROWS),
            col_blk(ML_WIDTH), row_blk(ML_WIDTH), col_blk(ML_HEADS * V_EXT_ROWS), col_blk(ML_WIDTH),
            col_blk(2 * ML_HEADS), col_blk(2 * ML_HEADS),
        ),
        out_shape=out_shape,
        scratch_shapes=[pltpu.VMEM((rows + CONV_HALO, 2 * ML_WIDTH), F32)],
        compiler_params=_compiler_params(1),
        name="in_proj",
    )(x2d, pos3d, norm_g, w_t, w_row, inv_freq, conv_w, conv_b, gb_col)


def _attn_kernel(lam_ref, q_t_ref, k_ref, v_t_ref, g_ref, o_ref,
                 qm_ref, ind_ref, s_ref, p_ref, acc_ref):
    seq = k_ref.shape[0]
    tq, tk = ATTN_Q, ATTN_K
    n_q = seq // tq
    diag_tiles = tq // tk

    lv = lam_ref[...]
    lam = (jnp.exp(jnp.sum(lv[0:1] * lv[1:2], axis=1, keepdims=True))
           - jnp.exp(jnp.sum(lv[2:3] * lv[3:4], axis=1, keepdims=True)) + LAM_INIT)

    feat = lax.broadcasted_iota(jnp.int32, (MASK_FEATS, 2 * tq), 0)
    qchunk = (lax.broadcasted_iota(jnp.int32, (MASK_FEATS, 2 * tq), 1) % tq) // CHUNK
    qm_ref[...] = jnp.zeros(qm_ref.shape, BF16)
    qm_ref[2 * DA_QK_DIM:2 * DA_QK_DIM + MASK_FEATS, :] = jnp.where(
        feat > qchunk, -MASK_BIG, 0.0).astype(BF16)
    kchunk = lax.broadcasted_iota(jnp.int32, (tk, 2 * DA_QK_DIM), 0) // CHUNK
    lane = lax.broadcasted_iota(jnp.int32, (tk, 2 * DA_QK_DIM), 1)
    ind_ref[0] = jnp.zeros((tk, 2 * DA_QK_DIM), BF16)
    for d in range(diag_tiles):
        ind_ref[d + 1] = (lane == kchunk + d * (tk // CHUNK)).astype(BF16)

    def q_tile(i, first_tile):
        q_off = pl.multiple_of(i * tq, tq)
        t_diag = diag_tiles * i
        n_t = t_diag + diag_tiles

        def score(par, t, off=0, which=None):
            k_t = k_ref[pl.ds(pl.multiple_of(t * tk, tk), tk), :]
            if which is None:
                which = jnp.maximum(t - t_diag + 1, 0)
            k_ext = jnp.concatenate([k_t, ind_ref[which]], axis=1)
            if off:
                rhs = jnp.concatenate([qm_ref[:, off:tq], qm_ref[:, tq + off:2 * tq]], axis=1)
            else:
                rhs = qm_ref[...]
            w = tq - off
            s = jnp.dot(k_ext, rhs, preferred_element_type=F32)
            s_ref[par, :, 0:2 * w] = s
            return tuple(jnp.max(s[:, mi * w:(mi + 1) * w], axis=0, keepdims=True) for mi in range(2))

        def softmax(par, maxes, tile_max, off=0):
            w = tq - off
            new, alphas = [], []
            for mi in range(2):
                lanes = slice(mi * w, (mi + 1) * w)
                s = s_ref[par, :, lanes]
                m_old = maxes[mi][:, off:tq]
                m_new = jnp.maximum(m_old, tile_max[mi])
                p_ref[par, :, lanes] = jnp.exp2(s - m_new).astype(BF16)
                alphas.append(jnp.exp2(m_old - m_new))
                new.append(jnp.concatenate([maxes[mi][:, 0:off], m_new], axis=1) if off else m_new)
            return tuple(new), tuple(alphas)

        def accumulate(par, t, alphas, off=0):
            w = tq - off
            v_t = v_t_ref[:, pl.ds(pl.multiple_of(t * tk, tk), tk)]
            pv = jnp.dot(v_t, p_ref[par, :, 0:2 * w], preferred_element_type=F32)
            for mi in range(2):
                lanes = slice(mi * tq + off, (mi + 1) * tq)
                acc_ref[:, lanes] = alphas[mi] * acc_ref[:, lanes] + pv[:, mi * w:(mi + 1) * w]

        def step(par, t, carry, first=False):
            maxes, alphas, tile_max = carry
            next_max = score(1 - par, t + 1)
            if not first:
                accumulate(1 - par, t - 1, alphas)
            return softmax(par, maxes, tile_max) + (next_max,)

        q_t = q_t_ref[:, pl.ds(q_off, tq)]
        qm_ref[0:DA_QK_DIM, 0:tq] = q_t[0:DA_QK_DIM]
        qm_ref[DA_QK_DIM:2 * DA_QK_DIM, tq:2 * tq] = q_t[DA_QK_DIM:]
        tile_max = score(0, 0)
        if not first_tile:
            finish(i - 1)
        acc_ref[...] = jnp.zeros(acc_ref.shape, F32)

        neg = jnp.full((1, tq), -jnp.inf, F32)
        maxes, alphas = (neg, neg), None
        if not first_tile:
            carry = step(0, 0, (maxes, alphas, tile_max), first=True)

            def pair(u, carry):
                carry = step(1, 2 * u + 1, carry)
                return step(0, 2 * u + 2, carry)

            carry = lax.fori_loop(0, t_diag // 2 - 1, pair, carry)
            maxes, alphas, tile_max = step(1, t_diag - 1, carry)
        for d in range(diag_tiles):
            t, par = t_diag + d, d % 2
            if d + 1 < diag_tiles:
                next_max = score(1 - par, t + 1, off=(d + 1) * tk, which=d + 2)
            if d or not first_tile:
                accumulate(1 - par, t - 1, alphas, off=max(d - 1, 0) * tk)
            maxes, alphas = softmax(par, maxes, tile_max, off=d * tk)
            tile_max = next_max
        accumulate((diag_tiles - 1) % 2, n_t - 1, alphas, off=(diag_tiles - 1) * tk)
        return 0

    def finish(i):
        acc = acc_ref[...]
        o1 = acc[0:DA_V_DIM, 0:tq] / acc[DA_V_DIM:DA_V_DIM + 1, 0:tq]
        o2 = acc[0:DA_V_DIM, tq:2 * tq] / acc[DA_V_DIM:DA_V_DIM + 1, tq:2 * tq]
        o_t = o1 - lam * o2
        ms = jnp.mean(o_t * o_t, axis=0, keepdims=True)
        y_t = (o_t * lax.rsqrt(ms + NORM_EPS)) * g_ref[...] * (1.0 - LAM_INIT)
        o_ref[pl.ds(pl.multiple_of(i * tq, tq), tq), :] = y_t.T.astype(BF16)

    q_tile(jnp.int32(0), True)
    lax.fori_loop(1, n_q, lambda i, c: q_tile(i, False), 0)
    finish(jnp.int32(n_q - 1))


def _diff_attn(lam_params, q_t, k, v_t, subln_col, batch, seq_len):
    tokens = k.shape[0]
    tq, tk = ATTN_Q, ATTN_K
    return pl.pallas_call(
        _attn_kernel,
        grid=(batch, DA_HEADS),
        in_specs=[
            _const_spec(lam_params.shape),
            pl.BlockSpec((DA_V_DIM, seq_len), lambda b, h: (h, b)),
            pl.BlockSpec((seq_len, DA_V_DIM), lambda b, h: (b, h)),
            pl.BlockSpec((V_EXT_ROWS, seq_len), lambda b, h: (h, b)),
            _const_spec(subln_col.shape),
        ],
        out_specs=pl.BlockSpec((seq_len, DA_V_DIM), lambda b, h: (b, h)),
        out_shape=jax.ShapeDtypeStruct((tokens, DA_WIDTH), BF16),
        scratch_shapes=[
            pltpu.VMEM((V7X_MXU_DEPTH, 2 * tq), BF16),
            pltpu.VMEM((1 + tq // tk, tk, 2 * DA_QK_DIM), BF16),
            pltpu.VMEM((2, tk, 2 * tq), F32),
            pltpu.VMEM((2, tk, 2 * tq), BF16),
            pltpu.VMEM((V_EXT_ROWS, 2 * tq), F32),
        ],
        compiler_params=_compiler_params(2),
        name="diff_attn",
    )(lam_params, q_t, k, v_t, subln_col)


def _mlstm_begin(q_t_ref, k_ref, v_t_ref, g_t_ref, gcum_t_ref, c_ref, m_ref, cols, reset):
    chunk = cols.stop - cols.start
    heads = range(ML_HEADS)
    g_t = g_t_ref[:, cols]
    cum_row = gcum_t_ref[:, cols]
    key_rows = g_t - pltpu.roll(cum_row, ML_HEADS, axis=0)
    key_cols = jnp.concatenate(
        [key_rows, jnp.zeros((V7X_LANES - 2 * ML_HEADS, chunk), F32)], axis=0).T
    st = dict(
        chunk=chunk,
        q_t=[q_t_ref[h * ML_DIM:(h + 1) * ML_DIM, cols] for h in heads],
        k=[k_ref[cols, h * ML_DIM:(h + 1) * ML_DIM] for h in heads],
        v_t=[v_t_ref[h * V_EXT_ROWS:(h + 1) * V_EXT_ROWS, cols] for h in heads],
        c_t=[c_ref[h] for h in heads],
        m_prev=[m_ref[h] for h in heads],
        b_row=[cum_row[ML_HEADS + h:ML_HEADS + h + 1, :] for h in heads],
        i_row=[g_t[h:h + 1, :] for h in heads],
        e_mat=[jnp.broadcast_to(key_cols[:, h:h + 1], (chunk, chunk)) for h in heads],
    )
    if reset is not None:
        st["c_t"] = [jnp.where(reset, 0.0, c) for c in st["c_t"]]
        st["m_prev"] = [jnp.where(reset, 0.0, m) for m in st["m_prev"]]
    st["kq"] = [jnp.dot(st["k"][h], st["q_t"][h], preferred_element_type=F32) for h in heads]
    st["inter_mm"] = [jnp.dot(st["c_t"][h].astype(BF16), st["q_t"][h], preferred_element_type=F32)
                      for h in heads]
    return st


def _mlstm_weights(st):
    chunk = st["chunk"]
    s_id = lax.broadcasted_iota(jnp.int32, (chunk, chunk), 0)
    t_id = lax.broadcasted_iota(jnp.int32, (chunk, chunk), 1)
    causal = s_id <= t_id
    st["m_row"], st["w_inter"], st["sc"] = [], [], []
    for h in range(ML_HEADS):
        d_mat = jnp.where(causal, st["e_mat"][h] + st["b_row"][h], -jnp.inf)
        inter = st["b_row"][h] + st["m_prev"][h]
        m_row = jnp.maximum(inter, jnp.max(d_mat, axis=0, keepdims=True))
        st["m_row"].append(m_row)
        st["w_inter"].append(jnp.exp(inter - m_row))
        st["sc"].append((st["kq"][h] * jnp.exp(d_mat - m_row)).astype(BF16))


def _mlstm_matmuls(st, c_ref, m_ref):
    chunk = st["chunk"]
    heads = range(ML_HEADS)
    st["intra_mm"] = [jnp.dot(st["v_t"][h], st["sc"][h], preferred_element_type=F32) for h in heads]
    for h in heads:
        b_row, m_prev = st["b_row"][h], st["m_prev"][h]
        b_last = b_row[:, chunk - 1:chunk]
        g_row = (b_last - b_row) + st["i_row"][h]
        m_new = jnp.maximum(b_last + m_prev, jnp.max(g_row, axis=1, keepdims=True))
        decay = jnp.exp(b_last + m_prev - m_new)
        wk = jnp.exp(st["e_mat"][h][:, 0:ML_DIM] + (b_last - m_new))
        kw = (st["k"][h].astype(F32) * wk).astype(BF16)
        c_ref[h] = decay * st["c_t"][h] + jnp.dot(st["v_t"][h], kw, preferred_element_type=F32)
        m_ref[h] = m_new


def _mlstm_output(st, o_t_ref, ng_ref, cols):
    outs = []
    for h in range(ML_HEADS):
        num = st["intra_mm"][h] + st["w_inter"][h] * st["inter_mm"][h]
        nq = num[ML_DIM:ML_DIM + 1, :]
        hid = num[0:ML_DIM, :] / jnp.maximum(jnp.abs(nq), jnp.exp(-st["m_row"][h]))
        ms = jnp.mean(hid * hid, axis=0, keepdims=True)
        hn = (hid * lax.rsqrt(ms + NORM_EPS)) * ng_ref[h * ML_DIM:(h + 1) * ML_DIM, :]
        og = o_t_ref[h * ML_DIM:(h + 1) * ML_DIM, cols].astype(F32)
        outs.append((hn * (1.0 / (1.0 + jnp.exp(-og)))).T.astype(BF16))
    return jnp.concatenate(outs, axis=1)


def _rms(x, g):
    ms = jnp.mean(x * x, axis=-1, keepdims=True)
    return (x * lax.rsqrt(ms + NORM_EPS)) * g


def _out_ffn_kernel(tiles_per_seq, n_tiles,
                    x_ref, attn_ref, mq_t_ref, mk_ref, mv_t_ref, mo_t_ref, g_t_ref, gcum_t_ref, ng_ref,
                    w_out_ref, g_ffn_ref, w_gate_ref, w_up_ref, w_down_ref, g_final_ref,
                    out_ref, ml_ref, c_ref, m_ref):
    i = pl.program_id(0)
    reset = (jnp.minimum(i, n_tiles - 1) % tiles_per_seq) == 0
    chunks = [slice(c * ML_CHUNK, (c + 1) * ML_CHUNK) for c in range(FFN_ROWS // ML_CHUNK)]
    ml_args = (mq_t_ref, mk_ref, mv_t_ref, g_t_ref, gcum_t_ref, c_ref, m_ref)

    @pl.when(i == 0)
    def _():
        ml_ref[...] = jnp.zeros(ml_ref.shape, BF16)
        c_ref[...] = jnp.zeros(c_ref.shape, F32)
        m_ref[...] = jnp.zeros(m_ref.shape, F32)

    halves = chunks
    mix = jnp.concatenate([attn_ref[...], ml_ref[...]], axis=1)
    first = _mlstm_begin(*ml_args, chunks[0], reset)
    y = [x_ref[hs, :] + jnp.dot(mix[hs], w_out_ref[...], preferred_element_type=F32) for hs in halves]
    h2 = [_rms(yh, g_ffn_ref[...]).astype(BF16) for yh in y]
    _mlstm_weights(first)
    _mlstm_matmuls(first, c_ref, m_ref)
    gate = [jnp.dot(h, w_gate_ref[...], preferred_element_type=F32) for h in h2]
    ml_ref[chunks[0], :] = _mlstm_output(first, mo_t_ref, ng_ref, chunks[0])
    second = _mlstm_begin(*ml_args, chunks[1], None)
    up = [jnp.dot(h, w_up_ref[...], preferred_element_type=F32) for h in h2]
    _mlstm_weights(second)
    _mlstm_matmuls(second, c_ref, m_ref)
    act = [((g * (1.0 / (1.0 + jnp.exp(-g)))) * u).astype(BF16) for g, u in zip(gate, up)]
    y2 = [yh + jnp.dot(a, w_down_ref[...], preferred_element_type=F32) for yh, a in zip(y, act)]
    ml_ref[chunks[1], :] = _mlstm_output(second, mo_t_ref, ng_ref, chunks[1])
    for hs, yh in zip(halves, y2):
        out_ref[hs, :] = _rms(yh, g_final_ref[...])


def _out_ffn(x2d, attn, mq_t, mk, mv_t, mo_t, g_t, gcum_t, gain,
             w_out, g_ffn, w_gate, w_up, w_down, g_final, seq_len):
    tokens = x2d.shape[0]
    rows = FFN_ROWS
    n_tiles = tokens // rows
    ffn_blk = lambda width: pl.BlockSpec((rows, width), lambda i: (jnp.maximum(i - 1, 0), 0))
    ml_row = lambda width: pl.BlockSpec((rows, width), lambda i: (jnp.minimum(i, n_tiles - 1), 0))
    ml_col = lambda height: pl.BlockSpec((height, rows), lambda i: (0, jnp.minimum(i, n_tiles - 1)))
    return pl.pallas_call(
        functools.partial(_out_ffn_kernel, seq_len // rows, n_tiles),
        grid=(n_tiles + 1,),
        in_specs=[
            ffn_blk(D_MODEL), ffn_blk(DA_WIDTH),
            ml_col(ML_WIDTH), ml_row(ML_WIDTH), ml_col(ML_HEADS * V_EXT_ROWS), ml_col(ML_WIDTH),
            ml_col(2 * ML_HEADS), ml_col(2 * ML_HEADS),
            _const_spec(gain.shape),
            _const_spec(w_out.shape), _const_spec(g_ffn.shape),
            _const_spec(w_gate.shape), _const_spec(w_up.shape), _const_spec(w_down.shape),
            _const_spec(g_final.shape),
        ],
        out_specs=ffn_blk(D_MODEL),
        out_shape=jax.ShapeDtypeStruct((tokens, D_MODEL), F32),
        scratch_shapes=[pltpu.VMEM((rows, ML_WIDTH), BF16),
                        pltpu.VMEM((ML_HEADS, V_EXT_ROWS, ML_DIM), F32),
                        pltpu.VMEM((ML_HEADS, 1, 1), F32)],
        compiler_params=_compiler_params(1),
        name="out_ffn",
    )(x2d, attn, mq_t, mk, mv_t, mo_t, g_t, gcum_t, gain,
      w_out, g_ffn, w_gate, w_up, w_down, g_final)


def kernel(x, positions, mix_norm_g, w_in, da_lambda, da_subln_g, ml_conv_w, ml_conv_b, ml_gate_b,
           ml_norm_g, w_out, ffn_norm_g, w_gate, w_up, w_down, final_norm_g):
    batch, seq_len, _ = x.shape
    tokens = batch * seq_len
    depth = w_in.shape[0]
    assert depth == 1, "one trunk layer"
    assert seq_len % PROJ_ROWS == 0 and seq_len % ATTN_Q == 0 and seq_len % ML_CHUNK == 0
    assert ATTN_Q % ATTN_K == 0 and ATTN_K % CHUNK == 0 and tokens % FFN_ROWS == 0
    assert 2 * DA_QK_DIM + MASK_FEATS <= V7X_MXU_DEPTH and MASK_FEATS <= 2 * DA_QK_DIM
    assert DA_V_DIM == ML_DIM and DA_HEADS == ML_HEADS and FFN_ROWS == 2 * ML_CHUNK
    assert PROJ_ROWS % ML_CHUNK == 0 and w_gate.shape[1:] == (D_MODEL, D_FF)

    x2d = x.reshape(tokens, D_MODEL)
    pos3d = positions.reshape(tokens // PROJ_ROWS, 1, PROJ_ROWS)

    w = w_in[0]
    ml_v = OFF_ML + 2 * ML_WIDTH
    gate_pad = jnp.zeros((D_MODEL, V7X_BF16_ROWS_PER_VREG - 2 * ML_HEADS), w.dtype)
    w_t = jnp.concatenate([w[:, OFF_DA_Q:OFF_ML], w[:, ml_v:OFF_GATE + 2 * ML_HEADS], gate_pad],
                          axis=1).T.astype(BF16)
    w_row = w[:, OFF_ML:ml_v].astype(BF16)
    inv_freq = (ROPE_THETA ** (-jnp.arange(0, ROT_DIM, 2, dtype=F32) / ROT_DIM)).reshape(ROT_HALF, 1)
    gb_col = ml_gate_b[0].astype(F32).reshape(2 * ML_HEADS, 1)

    q_t, k, v_t, mq_t, mk, mv_t, mo_t, g_t, gcum_t = _in_proj(
        x2d, pos3d, mix_norm_g[0].reshape(1, D_MODEL).astype(F32), w_t, w_row, inv_freq,
        ml_conv_w[0].astype(F32), ml_conv_b[0].reshape(1, 2 * ML_WIDTH).astype(F32),
        gb_col, seq_len)

    attn = _diff_attn(da_lambda[0].astype(F32), q_t, k, v_t,
                      da_subln_g[0].astype(F32).reshape(DA_V_DIM, 1), batch, seq_len)
    ml_gain = jnp.broadcast_to(ml_norm_g[0].astype(F32).reshape(ML_WIDTH, 1), (ML_WIDTH, ML_CHUNK))

    out = _out_ffn(x2d, attn, mq_t, mk, mv_t, mo_t, g_t, gcum_t, ml_gain, w_out[0].astype(BF16),
                   ffn_norm_g[0].reshape(1, D_MODEL).astype(F32),
                   w_gate[0].astype(BF16), w_up[0].astype(BF16), w_down[0].astype(BF16),
                   final_norm_g.reshape(1, D_MODEL).astype(F32), seq_len)
    return out.reshape(batch, seq_len, D_MODEL)
```

```python
import functools
import math

import jax
import jax.numpy as jnp
from jax import lax
from jax.experimental import pallas as pl
from jax.experimental.pallas import tpu as pltpu

F32 = jnp.float32
BF16 = jnp.bfloat16

D_MODEL = 1024
CHUNK = 64
NORM_EPS = 1e-6
DA_HEADS = 4
DA_QK_DIM = 64
DA_V_DIM = 128
DA_WIDTH = DA_HEADS * DA_V_DIM
ROPE_THETA = 500000.0
ROT_DIM = DA_QK_DIM // 4
ROT_HALF = ROT_DIM // 2
ML_HEADS = 4
ML_DIM = 128
ML_WIDTH = ML_HEADS * ML_DIM
CONV_WIDTH = 4
D_FF = 2816
LAM_INIT = 0.8 - 0.6 * math.exp(-0.3 * 0)
Q_SCALE = DA_QK_DIM ** -0.5 * math.log2(math.e)

OFF_DA_Q = 0
OFF_DA_K = 512
OFF_DA_V = 1024
OFF_ML = 1536
OFF_GATE = 3584
ZT_ML_V = 1536
ZT_ML_O = 2048
ZT_GATE = 2560

V7X_LANES = 128
V7X_SUBLANES = 8
V7X_BF16_ROWS_PER_VREG = 16
V7X_MXU_DEPTH = 256
V7X_VMEM_LIMIT_BYTES = 56 * 1024 * 1024

PROJ_ROWS = 1024
ATTN_Q = 1024
ATTN_K = 512
MASK_FEATS = ATTN_Q // CHUNK
MASK_BIG = 1e30
V_EXT_ROWS = DA_V_DIM + V7X_BF16_ROWS_PER_VREG
ML_CHUNK = 256
FFN_ROWS = 512
CONV_HALO = V7X_SUBLANES
CONV_COLS = 256

_NT = (((1,), (1,)), ((), ()))


def _compiler_params(n_axes):
    return pltpu.CompilerParams(
        dimension_semantics=("arbitrary",) * n_axes,
        vmem_limit_bytes=V7X_VMEM_LIMIT_BYTES,
    )


def _const_spec(shape):
    zeros = (0,) * len(shape)
    return pl.BlockSpec(shape, lambda *_: zeros, pipeline_mode=pl.Buffered(1))


def _rope_rows(zt, cos, sin):
    pieces = []
    for g in range(2):
        base = g * DA_QK_DIM
        x1 = zt[base:base + ROT_HALF]
        x2 = zt[base + ROT_HALF:base + ROT_DIM]
        pieces += [x1 * cos - x2 * sin, x2 * cos + x1 * sin, zt[base + ROT_DIM:base + DA_QK_DIM]]
    return jnp.concatenate(pieces, axis=0)


def _in_proj_kernel(tiles_per_seq,
                    x_ref, pos_ref, g_ref, w_t_ref, w_row_ref, invf_ref,
                    convw_ref, convb_ref, gb_col_ref,
                    q_t_ref, k_ref, v_t_ref, mq_t_ref, mk_ref, mv_t_ref, mo_t_ref, g_t_ref, gcum_t_ref,
                    halo_ref):
    rows = x_ref.shape[0]
    x = x_ref[...]
    ms = jnp.mean(x * x, axis=-1, keepdims=True)
    hb = ((x * lax.rsqrt(ms + NORM_EPS)) * g_ref[...]).astype(BF16)

    first = (pl.program_id(0) % tiles_per_seq) == 0
    halo_ref[0:CONV_HALO, :] = jnp.where(first, 0.0, halo_ref[rows:rows + CONV_HALO, :])
    halo_ref[CONV_HALO:CONV_HALO + rows, :] = jnp.dot(hb, w_row_ref[...], preferred_element_type=F32)

    def proj_t(lo, hi):
        return lax.dot_general(w_t_ref[lo:hi, :], hb, _NT, preferred_element_type=F32)

    zqk_t = proj_t(OFF_DA_Q, OFF_DA_V)

    def conv_group(g):
        cols = slice(g * CONV_COLS, (g + 1) * CONV_COLS)
        xe = halo_ref[:, cols]
        conv = convw_ref[0:1, cols] * xe
        for j in range(1, CONV_WIDTH):
            conv = pltpu.roll(conv, 1, axis=0) + convw_ref[j:j + 1, cols] * xe
        conv = conv[CONV_HALO:, :] + convb_ref[:, cols]
        act = conv * (1.0 / (1.0 + jnp.exp(-conv)))
        if cols.stop <= ML_WIDTH:
            mq_t_ref[cols, :] = (act * (ML_DIM ** -0.5)).T.astype(BF16)
        else:
            mk_ref[:, cols.start - ML_WIDTH:cols.stop - ML_WIDTH] = act.astype(BF16)

    conv_groups = iter(range(2 * ML_WIDTH // CONV_COLS))
    conv_group(next(conv_groups))

    ang = invf_ref[...] * pos_ref[0].astype(F32)
    cos = jnp.cos(ang)
    sin = jnp.sin(ang)
    for h in range(DA_HEADS):
        lo = h * 2 * DA_QK_DIM
        hi = lo + 2 * DA_QK_DIM
        q_rot = _rope_rows(zqk_t[OFF_DA_Q + lo:OFF_DA_Q + hi], cos, sin)
        q_t_ref[lo:hi, :] = (q_rot * Q_SCALE).astype(BF16)
        k_rot = _rope_rows(zqk_t[OFF_DA_K + lo:OFF_DA_K + hi], cos, sin)
        k_ref[:, lo:hi] = k_rot.T.astype(BF16)
    pad_rows = V_EXT_ROWS - DA_V_DIM
    ones_row = (lax.broadcasted_iota(jnp.int32, (pad_rows, rows), 0) == 0).astype(BF16)
    for out_ref, base in ((v_t_ref, OFF_DA_V), (mv_t_ref, ZT_ML_V)):
        zv_t = proj_t(base, base + DA_WIDTH)
        conv_group(next(conv_groups))
        for h in range(DA_HEADS):
            v_lo = h * DA_V_DIM
            out_ref[h * V_EXT_ROWS:h * V_EXT_ROWS + DA_V_DIM, :] = zv_t[v_lo:v_lo + DA_V_DIM].astype(BF16)
            out_ref[h * V_EXT_ROWS + DA_V_DIM:(h + 1) * V_EXT_ROWS, :] = ones_row
    zo_t = proj_t(ZT_ML_O, ZT_GATE + V7X_BF16_ROWS_PER_VREG)
    conv_group(next(conv_groups))
    mo_t_ref[...] = zo_t[0:ML_WIDTH].astype(BF16)

    a_t = zo_t[ML_WIDTH:ML_WIDTH + 2 * ML_HEADS] + gb_col_ref[...]
    ls_t = jnp.minimum(a_t, 0.0) - jnp.log1p(jnp.exp(-jnp.abs(a_t)))
    row_id = lax.broadcasted_iota(jnp.int32, a_t.shape, 0)
    log_gates = jnp.where(row_id < ML_HEADS, a_t, ls_t)
    g_t_ref[...] = log_gates
    src = lax.broadcasted_iota(jnp.int32, (rows, rows), 0)
    dst = lax.broadcasted_iota(jnp.int32, (rows, rows), 1)
    tri = ((src <= dst) & (src // ML_CHUNK == dst // ML_CHUNK)).astype(F32)
    gcum_t_ref[...] = jnp.dot(log_gates, tri, preferred_element_type=F32,
                              precision=lax.Precision.HIGHEST)


def _in_proj(x2d, pos3d, norm_g, w_t, w_row, inv_freq, conv_w, conv_b, gb_col, seq_len):
    tokens = x2d.shape[0]
    rows = PROJ_ROWS
    n_tiles = tokens // rows
    row_blk = lambda width: pl.BlockSpec((rows, width), lambda i: (i, 0))
    col_blk = lambda height: pl.BlockSpec((height, rows), lambda i: (0, i))
    out_shape = (
        jax.ShapeDtypeStruct((DA_WIDTH, tokens), BF16),
        jax.ShapeDtypeStruct((tokens, DA_WIDTH), BF16),
        jax.ShapeDtypeStruct((DA_HEADS * V_EXT_ROWS, tokens), BF16),
        jax.ShapeDtypeStruct((ML_WIDTH, tokens), BF16),
        jax.ShapeDtypeStruct((tokens, ML_WIDTH), BF16),
        jax.ShapeDtypeStruct((ML_HEADS * V_EXT_ROWS, tokens), BF16),
        jax.ShapeDtypeStruct((ML_WIDTH, tokens), BF16),
        jax.ShapeDtypeStruct((2 * ML_HEADS, tokens), F32),
        jax.ShapeDtypeStruct((2 * ML_HEADS, tokens), F32),
    )
    return pl.pallas_call(
        functools.partial(_in_proj_kernel, seq_len // rows),
        grid=(n_tiles,),
        in_specs=[
            row_blk(D_MODEL),
            pl.BlockSpec((1, 1, rows), lambda i: (i, 0, 0)),
            _const_spec(norm_g.shape),
            _const_spec(w_t.shape),
            _const_spec(w_row.shape),
            _const_spec(inv_freq.shape),
            _const_spec(conv_w.shape),
            _const_spec(conv_b.shape),
            _const_spec(gb_col.shape),
        ],
        out_specs=(
            col_blk(DA_WIDTH), row_blk(DA_WIDTH), col_blk(DA_HEADS * V_EXT_ROWS),
            col_blk(ML_WIDTH), row_blk(ML_WIDTH), col_blk(ML_HEADS * V_EXT_ROWS), col_blk(ML_WIDTH),
            col_blk(2 * ML_HEADS), col_blk(2 * ML_HEADS),
        ),
        out_shape=out_shape,
        scratch_shapes=[pltpu.VMEM((rows + CONV_HALO, 2 * ML_WIDTH), F32)],
        compiler_params=_compiler_params(1),
        name="in_proj",
    )(x2d, pos3d, norm_g, w_t, w_row, inv_freq, conv_w, conv_b, gb_col)


def _attn_kernel(lam_ref, q_t_ref, k_ref, v_t_ref, g_ref, o_ref,
                 qm_ref, ind_ref, s_ref, p_ref, acc_ref):
    seq = k_ref.shape[0]
    tq, tk = ATTN_Q, ATTN_K
    n_q = seq // tq
    diag_tiles = tq // tk

    lv = lam_ref[...]
    lam = (jnp.exp(jnp.sum(lv[0:1] * lv[1:2], axis=1, keepdims=True))
           - jnp.exp(jnp.sum(lv[2:3] * lv[3:4], axis=1, keepdims=True)) + LAM_INIT)

    feat = lax.broadcasted_iota(jnp.int32, (MASK_FEATS, 2 * tq), 0)
    qchunk = (lax.broadcasted_iota(jnp.int32, (MASK_FEATS, 2 * tq), 1) % tq) // CHUNK
    qm_ref[...] = jnp.zeros(qm_ref.shape, BF16)
    qm_ref[2 * DA_QK_DIM:2 * DA_QK_DIM + MASK_FEATS, :] = jnp.where(
        feat > qchunk, -MASK_BIG, 0.0).astype(BF16)
    kchunk = lax.broadcasted_iota(jnp.int32, (tk, 2 * DA_QK_DIM), 0) // CHUNK
    lane = lax.broadcasted_iota(jnp.int32, (tk, 2 * DA_QK_DIM), 1)
    ind_ref[0] = jnp.zeros((tk, 2 * DA_QK_DIM), BF16)
    for d in range(diag_tiles):
        ind_ref[d + 1] = (lane == kchunk + d * (tk // CHUNK)).astype(BF16)

    def q_tile(i, first_tile):
        q_off = pl.multiple_of(i * tq, tq)
        t_diag = diag_tiles * i
        n_t = t_diag + diag_tiles

        def score(par, t, off=0, which=None):
            k_t = k_ref[pl.ds(pl.multiple_of(t * tk, tk), tk), :]
            if which is None:
                which = jnp.maximum(t - t_diag + 1, 0)
            k_ext = jnp.concatenate([k_t, ind_ref[which]], axis=1)
            if off:
                rhs = jnp.concatenate([qm_ref[:, off:tq], qm_ref[:, tq + off:2 * tq]], axis=1)
            else:
                rhs = qm_ref[...]
            w = tq - off
            s = jnp.dot(k_ext, rhs, preferred_element_type=F32)
            s_ref[par, :, 0:2 * w] = s
            return tuple(jnp.max(s[:, mi * w:(mi + 1) * w], axis=0, keepdims=True) for mi in range(2))

        def softmax(par, maxes, tile_max, off=0):
            w = tq - off
            new, alphas = [], []
            for mi in range(2):
                lanes = slice(mi * w, (mi + 1) * w)
                s = s_ref[par, :, lanes]
                m_old = maxes[mi][:, off:tq]
                m_new = jnp.maximum(m_old, tile_max[mi])
                p_ref[par, :, lanes] = jnp.exp2(s - m_new).astype(BF16)
                alphas.append(jnp.exp2(m_old - m_new))
                new.append(jnp.concatenate([maxes[mi][:, 0:off], m_new], axis=1) if off else m_new)
            return tuple(new), tuple(alphas)

        def accumulate(par, t, alphas, off=0):
            w = tq - off
            v_t = v_t_ref[:, pl.ds(pl.multiple_of(t * tk, tk), tk)]
            pv = jnp.dot(v_t, p_ref[par, :, 0:2 * w], preferred_element_type=F32)
            for mi in range(2):
                lanes = slice(mi * tq + off, (mi + 1) * tq)
                acc_ref[:, lanes] = alphas[mi] * acc_ref[:, lanes] + pv[:, mi * w:(mi + 1) * w]

        def step(par, t, carry, first=False):
            maxes, alphas, tile_max = carry
            next_max = score(1 - par, t + 1)
            if not first:
                accumulate(1 - par, t - 1, alphas)
            return softmax(par, maxes, tile_max) + (next_max,)

        q_t = q_t_ref[:, pl.ds(q_off, tq)]
        qm_ref[0:DA_QK_DIM, 0:tq] = q_t[0:DA_QK_DIM]
        qm_ref[DA_QK_DIM:2 * DA_QK_DIM, tq:2 * tq] = q_t[DA_QK_DIM:]
        tile_max = score(0, 0)
        if not first_tile:
            finish(i - 1)
        acc_ref[...] = jnp.zeros(acc_ref.shape, F32)

        neg = jnp.full((1, tq), -jnp.inf, F32)
        maxes, alphas = (neg, neg), None
        if not first_tile:
            carry = step(0, 0, (maxes, alphas, tile_max), first=True)

            def pair(u, carry):
                carry = step(1, 2 * u + 1, carry)
                return step(0, 2 * u + 2, carry)

            carry = lax.fori_loop(0, t_diag // 2 - 1, pair, carry)
            maxes, alphas, tile_max = step(1, t_diag - 1, carry)
        for d in range(diag_tiles):
            t, par = t_diag + d, d % 2
            if d + 1 < diag_tiles:
                next_max = score(1 - par, t + 1, off=(d + 1) * tk, which=d + 2)
            if d or not first_tile:
                accumulate(1 - par, t - 1, alphas, off=max(d - 1, 0) * tk)
            maxes, alphas = softmax(par, maxes, tile_max, off=d * tk)
            tile_max = next_max
        accumulate((diag_tiles - 1) % 2, n_t - 1, alphas, off=(diag_tiles - 1) * tk)
        return 0

    def finish(i):
        acc = acc_ref[...]
        o1 = acc[0:DA_V_DIM, 0:tq] / acc[DA_V_DIM:DA_V_DIM + 1, 0:tq]
        o2 = acc[0:DA_V_DIM, tq:2 * tq] / acc[DA_V_DIM:DA_V_DIM + 1, tq:2 * tq]
        o_t = o1 - lam * o2
        ms = jnp.mean(o_t * o_t, axis=0, keepdims=True)
        y_t = (o_t * lax.rsqrt(ms + NORM_EPS)) * g_ref[...] * (1.0 - LAM_INIT)
        o_ref[pl.ds(pl.multiple_of(i * tq, tq), tq), :] = y_t.T.astype(BF16)

    q_tile(jnp.int32(0), True)
    lax.fori_loop(1, n_q, lambda i, c: q_tile(i, False), 0)
    finish(jnp.int32(n_q - 1))


def _diff_attn(lam_params, q_t, k, v_t, subln_col, batch, seq_len):
    tokens = k.shape[0]
    tq, tk = ATTN_Q, ATTN_K
    return pl.pallas_call(
        _attn_kernel,
        grid=(batch, DA_HEADS),
        in_specs=[
            _const_spec(lam_params.shape),
            pl.BlockSpec((DA_V_DIM, seq_len), lambda b, h: (h, b)),
            pl.BlockSpec((seq_len, DA_V_DIM), lambda b, h: (b, h)),
            pl.BlockSpec((V_EXT_ROWS, seq_len), lambda b, h: (h, b)),
            _const_spec(subln_col.shape),
        ],
        out_specs=pl.BlockSpec((seq_len, DA_V_DIM), lambda b, h: (b, h)),
        out_shape=jax.ShapeDtypeStruct((tokens, DA_WIDTH), BF16),
        scratch_shapes=[
            pltpu.VMEM((V7X_MXU_DEPTH, 2 * tq), BF16),
            pltpu.VMEM((1 + tq // tk, tk, 2 * DA_QK_DIM), BF16),
            pltpu.VMEM((2, tk, 2 * tq), F32),
            pltpu.VMEM((2, tk, 2 * tq), BF16),
            pltpu.VMEM((V_EXT_ROWS, 2 * tq), F32),
        ],
        compiler_params=_compiler_params(2),
        name="diff_attn",
    )(lam_params, q_t, k, v_t, subln_col)


def _mlstm_begin(q_t_ref, k_ref, v_t_ref, g_t_ref, gcum_t_ref, c_ref, m_ref, cols, reset):
    chunk = cols.stop - cols.start
    heads = range(ML_HEADS)
    g_t = g_t_ref[:, cols]
    cum_row = gcum_t_ref[:, cols]
    key_rows = g_t - pltpu.roll(cum_row, ML_HEADS, axis=0)
    key_cols = jnp.concatenate(
        [key_rows, jnp.zeros((V7X_LANES - 2 * ML_HEADS, chunk), F32)], axis=0).T
    st = dict(
        chunk=chunk,
        q_t=[q_t_ref[h * ML_DIM:(h + 1) * ML_DIM, cols] for h in heads],
        k=[k_ref[cols, h * ML_DIM:(h + 1) * ML_DIM] for h in heads],
        v_t=[v_t_ref[h * V_EXT_ROWS:(h + 1) * V_EXT_ROWS, cols] for h in heads],
        c_t=[c_ref[h] for h in heads],
        m_prev=[m_ref[h] for h in heads],
        b_row=[cum_row[ML_HEADS + h:ML_HEADS + h + 1, :] for h in heads],
        i_row=[g_t[h:h + 1, :] for h in heads],
        e_mat=[jnp.broadcast_to(key_cols[:, h:h + 1], (chunk, chunk)) for h in heads],
    )
    if reset is not None:
        st["c_t"] = [jnp.where(reset, 0.0, c) for c in st["c_t"]]
        st["m_prev"] = [jnp.where(reset, 0.0, m) for m in st["m_prev"]]
    st["kq"] = [jnp.dot(st["k"][h], st["q_t"][h], preferred_element_type=F32) for h in heads]
    st["inter_mm"] = [jnp.dot(st["c_t"][h].astype(BF16), st["q_t"][h], preferred_element_type=F32)
                      for h in heads]
    return st


def _mlstm_weights(st):
    chunk = st["chunk"]
    s_id = lax.broadcasted_iota(jnp.int32, (chunk, chunk), 0)
    t_id = lax.broadcasted_iota(jnp.int32, (chunk, chunk), 1)
    causal = s_id <= t_id
    st["m_row"], st["w_inter"], st["sc"] = [], [], []
    for h in range(ML_HEADS):
        d_mat = jnp.where(causal, st["e_mat"][h] + st["b_row"][h], -jnp.inf)
        inter = st["b_row"][h] + st["m_prev"][h]
        m_row = jnp.maximum(inter, jnp.max(d_mat, axis=0, keepdims=True))
        st["m_row"].append(m_row)
        st["w_inter"].append(jnp.exp(inter - m_row))
        st["sc"].append((st["kq"][h] * jnp.exp(d_mat - m_row)).astype(BF16))


def _mlstm_matmuls(st, c_ref, m_ref):
    chunk = st["chunk"]
    heads = range(ML_HEADS)
    st["intra_mm"] = [jnp.dot(st["v_t"][h], st["sc"][h], preferred_element_type=F32) for h in heads]
    for h in heads:
        b_row, m_prev = st["b_row"][h], st["m_prev"][h]
        b_last = b_row[:, chunk - 1:chunk]
        g_row = (b_last - b_row) + st["i_row"][h]
        m_new = jnp.maximum(b_last + m_prev, jnp.max(g_row, axis=1, keepdims=True))
        decay = jnp.exp(b_last + m_prev - m_new)
        wk = jnp.exp(st["e_mat"][h][:, 0:ML_DIM] + (b_last - m_new))
        kw = (st["k"][h].astype(F32) * wk).astype(BF16)
        c_ref[h] = decay * st["c_t"][h] + jnp.dot(st["v_t"][h], kw, preferred_element_type=F32)
        m_ref[h] = m_new


def _mlstm_output(st, o_t_ref, ng_ref, cols):
    outs = []
    for h in range(ML_HEADS):
        num = st["intra_mm"][h] + st["w_inter"][h] * st["inter_mm"][h]
        nq = num[ML_DIM:ML_DIM + 1, :]
        hid = num[0:ML_DIM, :] / jnp.maximum(jnp.abs(nq), jnp.exp(-st["m_row"][h]))
        ms = jnp.mean(hid * hid, axis=0, keepdims=True)
        hn = (hid * lax.rsqrt(ms + NORM_EPS)) * ng_ref[h * ML_DIM:(h + 1) * ML_DIM, :]
        og = o_t_ref[h * ML_DIM:(h + 1) * ML_DIM, cols].astype(F32)
        outs.append((hn * (1.0 / (1.0 + jnp.exp(-og)))).T.astype(BF16))
    return jnp.concatenate(outs, axis=1)


def _rms(x, g):
    ms = jnp.mean(x * x, axis=-1, keepdims=True)
    return (x * lax.rsqrt(ms + NORM_EPS)) * g


def _out_ffn_kernel(tiles_per_seq, n_tiles,
                    x_ref, attn_ref, mq_t_ref, mk_ref, mv_t_ref, mo_t_ref, g_t_ref, gcum_t_ref, ng_ref,
                    w_out_ref, g_ffn_ref, w_gate_ref, w_up_ref, w_down_ref, g_final_ref,
                    out_ref, ml_ref, c_ref, m_ref):
    i = pl.program_id(0)
    reset = (jnp.minimum(i, n_tiles - 1) % tiles_per_seq) == 0
    chunks = [slice(c * ML_CHUNK, (c + 1) * ML_CHUNK) for c in range(FFN_ROWS // ML_CHUNK)]
    ml_args = (mq_t_ref, mk_ref, mv_t_ref, g_t_ref, gcum_t_ref, c_ref, m_ref)

    @pl.when(i == 0)
    def _():
        ml_ref[...] = jnp.zeros(ml_ref.shape, BF16)
        c_ref[...] = jnp.zeros(c_ref.shape, F32)
        m_ref[...] = jnp.zeros(m_ref.shape, F32)

    halves = chunks
    mix = jnp.concatenate([attn_ref[...], ml_ref[...]], axis=1)
    first = _mlstm_begin(*ml_args, chunks[0], reset)
    y = [x_ref[hs, :] + jnp.dot(mix[hs], w_out_ref[...], preferred_element_type=F32) for hs in halves]
    h2 = [_rms(yh, g_ffn_ref[...]).astype(BF16) for yh in y]
    _mlstm_weights(first)
    _mlstm_matmuls(first, c_ref, m_ref)
    gate = [jnp.dot(h, w_gate_ref[...], preferred_element_type=F32) for h in h2]
    ml_ref[chunks[0], :] = _mlstm_output(first, mo_t_ref, ng_ref, chunks[0])
    second = _mlstm_begin(*ml_args, chunks[1], None)
    up = [jnp.dot(h, w_up_ref[...], preferred_element_type=F32) for h in h2]
    _mlstm_weights(second)
    _mlstm_matmuls(second, c_ref, m_ref)
    act = [((g * (1.0 / (1.0 + jnp.exp(-g)))) * u).astype(BF16) for g, u in zip(gate, up)]
    y2 = [yh + jnp.dot(a, w_down_ref[...], preferred_element_type=F32) for yh, a in zip(y, act)]
    ml_ref[chunks[1], :] = _mlstm_output(second, mo_t_ref, ng_ref, chunks[1])
    for hs, yh in zip(halves, y2):
        out_ref[hs, :] = _rms(yh, g_final_ref[...])


def _out_ffn(x2d, attn, mq_t, mk, mv_t, mo_t, g_t, gcum_t, gain,
             w_out, g_ffn, w_gate, w_up, w_down, g_final, seq_len):
    tokens = x2d.shape[0]
    rows = FFN_ROWS
    n_tiles = tokens // rows
    ffn_blk = lambda width: pl.BlockSpec((rows, width), lambda i: (jnp.maximum(i - 1, 0), 0))
    ml_row = lambda width: pl.BlockSpec((rows, width), lambda i: (jnp.minimum(i, n_tiles - 1), 0))
    ml_col = lambda height: pl.BlockSpec((height, rows), lambda i: (0, jnp.minimum(i, n_tiles - 1)))
    return pl.pallas_call(
        functools.partial(_out_ffn_kernel, seq_len // rows, n_tiles),
        grid=(n_tiles + 1,),
        in_specs=[
            ffn_blk(D_MODEL), ffn_blk(DA_WIDTH),
            ml_col(ML_WIDTH), ml_row(ML_WIDTH), ml_col(ML_HEADS * V_EXT_ROWS), ml_col(ML_WIDTH),
            ml_col(2 * ML_HEADS), ml_col(2 * ML_HEADS),
            _const_spec(gain.shape),
            _const_spec(w_out.shape), _const_spec(g_ffn.shape),
            _const_spec(w_gate.shape), _const_spec(w_up.shape), _const_spec(w_down.shape),
            _const_spec(g_final.shape),
        ],
        out_specs=ffn_blk(D_MODEL),
        out_shape=jax.ShapeDtypeStruct((tokens, D_MODEL), F32),
        scratch_shapes=[pltpu.VMEM((rows, ML_WIDTH), BF16),
                        pltpu.VMEM((ML_HEADS, V_EXT_ROWS, ML_DIM), F32),
                        pltpu.VMEM((ML_HEADS, 1, 1), F32)],
        compiler_params=_compiler_params(1),
        name="out_ffn",
    )(x2d, attn, mq_t, mk, mv_t, mo_t, g_t, gcum_t, gain,
      w_out, g_ffn, w_gate, w_up, w_down, g_final)


def kernel(x, positions, mix_norm_g, w_in, da_lambda, da_subln_g, ml_conv_w, ml_conv_b, ml_gate_b,
           ml_norm_g, w_out, ffn_norm_g, w_gate, w_up, w_down, final_norm_g):
    batch, seq_len, _ = x.shape
    tokens = batch * seq_len
    depth = w_in.shape[0]
    assert depth == 1, "one trunk layer"
    assert seq_len % PROJ_ROWS == 0 and seq_len % ATTN_Q == 0 and seq_len % ML_CHUNK == 0
    assert ATTN_Q % ATTN_K == 0 and ATTN_K % CHUNK == 0 and tokens % FFN_ROWS == 0
    assert 2 * DA_QK_DIM + MASK_FEATS <= V7X_MXU_DEPTH and MASK_FEATS <= 2 * DA_QK_DIM
    assert DA_V_DIM == ML_DIM and DA_HEADS == ML_HEADS and FFN_ROWS == 2 * ML_CHUNK
    assert PROJ_ROWS % ML_CHUNK == 0 and w_gate.shape[1:] == (D_MODEL, D_FF)

    x2d = x.reshape(tokens, D_MODEL)
    pos3d = positions.reshape(tokens // PROJ_ROWS, 1, PROJ_ROWS)

    w = w_in[0].astype(BF16)
    ml_v = OFF_ML + 2 * ML_WIDTH
    gate_pad = jnp.zeros((D_MODEL, V7X_BF16_ROWS_PER_VREG - 2 * ML_HEADS), BF16)
    w_t = jnp.concatenate([w[:, OFF_DA_Q:OFF_ML], w[:, ml_v:OFF_GATE + 2 * ML_HEADS], gate_pad],
                          axis=1).T
    w_row = w[:, OFF_ML:ml_v]
    inv_freq = (ROPE_THETA ** (-jnp.arange(0, ROT_DIM, 2, dtype=F32) / ROT_DIM)).reshape(ROT_HALF, 1)
    gb_col = ml_gate_b[0].astype(F32).reshape(2 * ML_HEADS, 1)

    q_t, k, v_t, mq_t, mk, mv_t, mo_t, g_t, gcum_t = _in_proj(
        x2d, pos3d, mix_norm_g[0].reshape(1, D_MODEL).astype(F32), w_t, w_row, inv_freq,
        ml_conv_w[0].astype(F32), ml_conv_b[0].reshape(1, 2 * ML_WIDTH).astype(F32),
        gb_col, seq_len)

    attn = _diff_attn(da_lambda[0].astype(F32), q_t, k, v_t,
                      da_subln_g[0].astype(F32).reshape(DA_V_DIM, 1), batch, seq_len)
    ml_gain = jnp.broadcast_to(ml_norm_g[0].astype(F32).reshape(ML_WIDTH, 1), (ML_WIDTH, ML_CHUNK))

    out = _out_ffn(x2d, attn, mq_t, mk, mv_t, mo_t, g_t, gcum_t, ml_gain, w_out[0].astype(BF16),
                   ffn_norm_g[0].reshape(1, D_MODEL).astype(F32),
                   w_gate[0].astype(BF16), w_up[0].astype(BF16), w_down[0].astype(BF16),
                   final_norm_g.reshape(1, D_MODEL).astype(F32), seq_len)
    return out.reshape(batch, seq_len, D_MODEL)
```

```python
import functools
import math

import jax
import jax.numpy as jnp
from jax import lax
from jax.experimental import pallas as pl
from jax.experimental.pallas import tpu as pltpu

F32 = jnp.float32
BF16 = jnp.bfloat16

D_MODEL = 1024
CHUNK = 64
NORM_EPS = 1e-6
DA_HEADS = 4
DA_QK_DIM = 64
DA_V_DIM = 128
DA_WIDTH = DA_HEADS * DA_V_DIM
ROPE_THETA = 500000.0
ROT_DIM = DA_QK_DIM // 4
ROT_HALF = ROT_DIM // 2
ML_HEADS = 4
ML_DIM = 128
ML_WIDTH = ML_HEADS * ML_DIM
CONV_WIDTH = 4
D_FF = 2816
LAM_INIT = 0.8 - 0.6 * math.exp(-0.3 * 0)
Q_SCALE = DA_QK_DIM ** -0.5 * math.log2(math.e)

OFF_DA_Q = 0
OFF_DA_K = 512
OFF_DA_V = 1024
OFF_ML = 1536
OFF_GATE = 3584
OFF_ML_V = OFF_ML + 2 * ML_WIDTH
ZT_ML_V = OFF_ML
ZT_ML_O = ZT_ML_V + ML_WIDTH
ZT_GATE = ZT_ML_O + ML_WIDTH

V7X_LANES = 128
V7X_SUBLANES = 8
V7X_BF16_ROWS_PER_VREG = 16
V7X_MXU_DEPTH = 256
V7X_VMEM_LIMIT_BYTES = 56 * 1024 * 1024

PROJ_ROWS = 1024
ATTN_Q = 1024
ATTN_K = 512
MASK_FEATS = ATTN_Q // CHUNK
MASK_BIG = 1e30
V_EXT_ROWS = DA_V_DIM + V7X_BF16_ROWS_PER_VREG
ML_CHUNK = 256
FFN_ROWS = 512
CONV_HALO = V7X_SUBLANES
CONV_COLS = 256
W_T_COLS = 256

_NT = (((1,), (1,)), ((), ()))


def _compiler_params(n_axes):
    return pltpu.CompilerParams(
        dimension_semantics=("arbitrary",) * n_axes,
        vmem_limit_bytes=V7X_VMEM_LIMIT_BYTES,
    )


def _const_spec(shape):
    zeros = (0,) * len(shape)
    return pl.BlockSpec(shape, lambda *_: zeros, pipeline_mode=pl.Buffered(1))


def _rope_rows(zt, cos, sin):
    pieces = []
    for g in range(2):
        base = g * DA_QK_DIM
        x1 = zt[base:base + ROT_HALF]
        x2 = zt[base + ROT_HALF:base + ROT_DIM]
        pieces += [x1 * cos - x2 * sin, x2 * cos + x1 * sin, zt[base + ROT_DIM:base + DA_QK_DIM]]
    return jnp.concatenate(pieces, axis=0)


def _in_proj_kernel(tiles_per_seq,
                    x_ref, pos_ref, g_ref, w_ref, ml_gate_w_t_ref, invf_ref,
                    convw_ref, convb_ref, gb_col_ref,
                    q_t_ref, k_ref, v_t_ref, mq_t_ref, mk_ref, mv_t_ref, mo_t_ref, g_t_ref, gcum_t_ref,
                    halo_ref, w_t_ref, w_row_ref):
    rows = x_ref.shape[0]

    @pl.when(pl.program_id(0) == 0)
    def _():
        w_row_ref[...] = w_ref[:, OFF_ML:OFF_ML_V].astype(BF16)
        for dst, src, width in ((OFF_DA_Q, OFF_DA_Q, OFF_ML), (ZT_ML_V, OFF_ML_V, 2 * ML_WIDTH)):
            for c in range(0, width, W_T_COLS):
                w_t_ref[dst + c:dst + c + W_T_COLS, :] = (
                    w_ref[:, src + c:src + c + W_T_COLS].T.astype(BF16))
        w_t_ref[ZT_GATE:ZT_GATE + V7X_BF16_ROWS_PER_VREG, :] = ml_gate_w_t_ref[...]

    x = x_ref[...]
    ms = jnp.mean(x * x, axis=-1, keepdims=True)
    hb = ((x * lax.rsqrt(ms + NORM_EPS)) * g_ref[...]).astype(BF16)

    first = (pl.program_id(0) % tiles_per_seq) == 0
    halo_ref[0:CONV_HALO, :] = jnp.where(first, 0.0, halo_ref[rows:rows + CONV_HALO, :])
    halo_ref[CONV_HALO:CONV_HALO + rows, :] = jnp.dot(hb, w_row_ref[...], preferred_element_type=F32)

    def proj_t(lo, hi):
        return lax.dot_general(w_t_ref[lo:hi, :], hb, _NT, preferred_element_type=F32)

    zqk_t = proj_t(OFF_DA_Q, OFF_DA_V)

    def conv_group(g):
        cols = slice(g * CONV_COLS, (g + 1) * CONV_COLS)
        xe = halo_ref[:, cols]
        conv = convw_ref[0:1, cols] * xe
        for j in range(1, CONV_WIDTH):
            conv = pltpu.roll(conv, 1, axis=0) + convw_ref[j:j + 1, cols] * xe
        conv = conv[CONV_HALO:, :] + convb_ref[:, cols]
        act = conv * (1.0 / (1.0 + jnp.exp(-conv)))
        if cols.stop <= ML_WIDTH:
            mq_t_ref[cols, :] = (act * (ML_DIM ** -0.5)).T.astype(BF16)
        else:
            mk_ref[:, cols.start - ML_WIDTH:cols.stop - ML_WIDTH] = act.astype(BF16)

    conv_groups = iter(range(2 * ML_WIDTH // CONV_COLS))
    conv_group(next(conv_groups))

    ang = invf_ref[...] * pos_ref[0].astype(F32)
    cos = jnp.cos(ang)
    sin = jnp.sin(ang)
    for h in range(DA_HEADS):
        lo = h * 2 * DA_QK_DIM
        hi = lo + 2 * DA_QK_DIM
        q_rot = _rope_rows(zqk_t[OFF_DA_Q + lo:OFF_DA_Q + hi], cos, sin)
        q_t_ref[lo:hi, :] = (q_rot * Q_SCALE).astype(BF16)
        k_rot = _rope_rows(zqk_t[OFF_DA_K + lo:OFF_DA_K + hi], cos, sin)
        k_ref[:, lo:hi] = k_rot.T.astype(BF16)
    pad_rows = V_EXT_ROWS - DA_V_DIM
    ones_row = (lax.broadcasted_iota(jnp.int32, (pad_rows, rows), 0) == 0).astype(BF16)
    for out_ref, base in ((v_t_ref, OFF_DA_V), (mv_t_ref, ZT_ML_V)):
        zv_t = proj_t(base, base + DA_WIDTH)
        conv_group(next(conv_groups))
        for h in range(DA_HEADS):
            v_lo = h * DA_V_DIM
            out_ref[h * V_EXT_ROWS:h * V_EXT_ROWS + DA_V_DIM, :] = zv_t[v_lo:v_lo + DA_V_DIM].astype(BF16)
            out_ref[h * V_EXT_ROWS + DA_V_DIM:(h + 1) * V_EXT_ROWS, :] = ones_row
    zo_t = proj_t(ZT_ML_O, ZT_GATE + V7X_BF16_ROWS_PER_VREG)
    conv_group(next(conv_groups))
    mo_t_ref[...] = zo_t[0:ML_WIDTH].astype(BF16)

    a_t = zo_t[ML_WIDTH:ML_WIDTH + 2 * ML_HEADS] + gb_col_ref[...]
    ls_t = jnp.minimum(a_t, 0.0) - jnp.log1p(jnp.exp(-jnp.abs(a_t)))
    row_id = lax.broadcasted_iota(jnp.int32, a_t.shape, 0)
    log_gates = jnp.where(row_id < ML_HEADS, a_t, ls_t)
    g_t_ref[...] = log_gates
    src = lax.broadcasted_iota(jnp.int32, (rows, rows), 0)
    dst = lax.broadcasted_iota(jnp.int32, (rows, rows), 1)
    tri = ((src <= dst) & (src // ML_CHUNK == dst // ML_CHUNK)).astype(F32)
    gcum_t_ref[...] = jnp.dot(log_gates, tri, preferred_element_type=F32,
                              precision=lax.Precision.HIGHEST)


def _in_proj(x2d, pos3d, norm_g, w, ml_gate_w_t, inv_freq, conv_w, conv_b, gb_col, seq_len):
    tokens = x2d.shape[0]
    rows = PROJ_ROWS
    n_tiles = tokens // rows
    row_blk = lambda width: pl.BlockSpec((rows, width), lambda i: (i, 0))
    col_blk = lambda height: pl.BlockSpec((height, rows), lambda i: (0, i))
    out_shape = (
        jax.ShapeDtypeStruct((DA_WIDTH, tokens), BF16),
        jax.ShapeDtypeStruct((tokens, DA_WIDTH), BF16),
        jax.ShapeDtypeStruct((DA_HEADS * V_EXT_ROWS, tokens), BF16),
        jax.ShapeDtypeStruct((ML_WIDTH, tokens), BF16),
        jax.ShapeDtypeStruct((tokens, ML_WIDTH), BF16),
        jax.ShapeDtypeStruct((ML_HEADS * V_EXT_ROWS, tokens), BF16),
        jax.ShapeDtypeStruct((ML_WIDTH, tokens), BF16),
        jax.ShapeDtypeStruct((2 * ML_HEADS, tokens), F32),
        jax.ShapeDtypeStruct((2 * ML_HEADS, tokens), F32),
    )
    return pl.pallas_call(
        functools.partial(_in_proj_kernel, seq_len // rows),
        grid=(n_tiles,),
        in_specs=[
            row_blk(D_MODEL),
            pl.BlockSpec((1, 1, rows), lambda i: (i, 0, 0)),
            _const_spec(norm_g.shape),
            _const_spec(w.shape),
            _const_spec(ml_gate_w_t.shape),
            _const_spec(inv_freq.shape),
            _const_spec(conv_w.shape),
            _const_spec(conv_b.shape),
            _const_spec(gb_col.shape),
        ],
        out_specs=(
            col_blk(DA_WIDTH), row_blk(DA_WIDTH), col_blk(DA_HEADS * V_EXT_ROWS),
            col_blk(ML_WIDTH), row_blk(ML_WIDTH), col_blk(ML_HEADS * V_EXT_ROWS), col_blk(ML_WIDTH),
            col_blk(2 * ML_HEADS), col_blk(2 * ML_HEADS),
        ),
        out_shape=out_shape,
        scratch_shapes=[
            pltpu.VMEM((rows + CONV_HALO, 2 * ML_WIDTH), F32),
            pltpu.VMEM((ZT_GATE + V7X_BF16_ROWS_PER_VREG, D_MODEL), BF16),
            pltpu.VMEM((D_MODEL, 2 * ML_WIDTH), BF16),
        ],
        compiler_params=_compiler_params(1),
        name="in_proj",
    )(x2d, pos3d, norm_g, w, ml_gate_w_t, inv_freq, conv_w, conv_b, gb_col)


def _attn_kernel(lam_ref, q_t_ref, k_ref, v_t_ref, g_ref, o_ref,
                 qm_ref, ind_ref, s_ref, p_ref, acc_ref):
    seq = k_ref.shape[0]
    tq, tk = ATTN_Q, ATTN_K
    n_q = seq // tq
    diag_tiles = tq // tk

    lv = lam_ref[...]
    lam = (jnp.exp(jnp.sum(lv[0:1] * lv[1:2], axis=1, keepdims=True))
           - jnp.exp(jnp.sum(lv[2:3] * lv[3:4], axis=1, keepdims=True)) + LAM_INIT)

    feat = lax.broadcasted_iota(jnp.int32, (MASK_FEATS, 2 * tq), 0)
    qchunk = (lax.broadcasted_iota(jnp.int32, (MASK_FEATS, 2 * tq), 1) % tq) // CHUNK
    qm_ref[...] = jnp.zeros(qm_ref.shape, BF16)
    qm_ref[2 * DA_QK_DIM:2 * DA_QK_DIM + MASK_FEATS, :] = jnp.where(
        feat > qchunk, -MASK_BIG, 0.0).astype(BF16)
    kchunk = lax.broadcasted_iota(jnp.int32, (tk, 2 * DA_QK_DIM), 0) // CHUNK
    lane = lax.broadcasted_iota(jnp.int32, (tk, 2 * DA_QK_DIM), 1)
    ind_ref[0] = jnp.zeros((tk, 2 * DA_QK_DIM), BF16)
    for d in range(diag_tiles):
        ind_ref[d + 1] = (lane == kchunk + d * (tk // CHUNK)).astype(BF16)

    def q_tile(i, first_tile):
        q_off = pl.multiple_of(i * tq, tq)
        t_diag = diag_tiles * i
        n_t = t_diag + diag_tiles

        def score(par, t, off=0, which=None):
            k_t = k_ref[pl.ds(pl.multiple_of(t * tk, tk), tk), :]
            if which is None:
                which = jnp.maximum(t - t_diag + 1, 0)
            k_ext = jnp.concatenate([k_t, ind_ref[which]], axis=1)
            if off:
                rhs = jnp.concatenate([qm_ref[:, off:tq], qm_ref[:, tq + off:2 * tq]], axis=1)
            else:
                rhs = qm_ref[...]
            w = tq - off
            s = jnp.dot(k_ext, rhs, preferred_element_type=F32)
            s_ref[par, :, 0:2 * w] = s
            return tuple(jnp.max(s[:, mi * w:(mi + 1) * w], axis=0, keepdims=True) for mi in range(2))

        def softmax(par, maxes, tile_max, off=0):
            w = tq - off
            new, alphas = [], []
            for mi in range(2):
                lanes = slice(mi * w, (mi + 1) * w)
                s = s_ref[par, :, lanes]
                m_old = maxes[mi][:, off:tq]
                m_new = jnp.maximum(m_old, tile_max[mi])
                p_ref[par, :, lanes] = jnp.exp2(s - m_new).astype(BF16)
                alphas.append(jnp.exp2(m_old - m_new))
                new.append(jnp.concatenate([maxes[mi][:, 0:off], m_new], axis=1) if off else m_new)
            return tuple(new), tuple(alphas)

        def accumulate(par, t, alphas, off=0):
            w = tq - off
            v_t = v_t_ref[:, pl.ds(pl.multiple_of(t * tk, tk), tk)]
            pv = jnp.dot(v_t, p_ref[par, :, 0:2 * w], preferred_element_type=F32)
            for mi in range(2):
                lanes = slice(mi * tq + off, (mi + 1) * tq)
                acc_ref[:, lanes] = alphas[mi] * acc_ref[:, lanes] + pv[:, mi * w:(mi + 1) * w]

        def step(par, t, carry, first=False):
            maxes, alphas, tile_max = carry
            next_max = score(1 - par, t + 1)
            if not first:
                accumulate(1 - par, t - 1, alphas)
            return softmax(par, maxes, tile_max) + (next_max,)

        q_t = q_t_ref[:, pl.ds(q_off, tq)]
        qm_ref[0:DA_QK_DIM, 0:tq] = q_t[0:DA_QK_DIM]
        qm_ref[DA_QK_DIM:2 * DA_QK_DIM, tq:2 * tq] = q_t[DA_QK_DIM:]
        tile_max = score(0, 0)
        if not first_tile:
            finish(i - 1)
        acc_ref[...] = jnp.zeros(acc_ref.shape, F32)

        neg = jnp.full((1, tq), -jnp.inf, F32)
        maxes, alphas = (neg, neg), None
        if not first_tile:
            carry = step(0, 0, (maxes, alphas, tile_max), first=True)

            def pair(u, carry):
                carry = step(1, 2 * u + 1, carry)
                return step(0, 2 * u + 2, carry)

            carry = lax.fori_loop(0, t_diag // 2 - 1, pair, carry)
            maxes, alphas, tile_max = step(1, t_diag - 1, carry)
        for d in range(diag_tiles):
            t, par = t_diag + d, d % 2
            if d + 1 < diag_tiles:
                next_max = score(1 - par, t + 1, off=(d + 1) * tk, which=d + 2)
            if d or not first_tile:
                accumulate(1 - par, t - 1, alphas, off=max(d - 1, 0) * tk)
            maxes, alphas = softmax(par, maxes, tile_max, off=d * tk)
            tile_max = next_max
        accumulate((diag_tiles - 1) % 2, n_t - 1, alphas, off=(diag_tiles - 1) * tk)
        return 0

    def finish(i):
        acc = acc_ref[...]
        o1 = acc[0:DA_V_DIM, 0:tq] / acc[DA_V_DIM:DA_V_DIM + 1, 0:tq]
        o2 = acc[0:DA_V_DIM, tq:2 * tq] / acc[DA_V_DIM:DA_V_DIM + 1, tq:2 * tq]
        o_t = o1 - lam * o2
        ms = jnp.mean(o_t * o_t, axis=0, keepdims=True)
        y_t = (o_t * lax.rsqrt(ms + NORM_EPS)) * g_ref[...] * (1.0 - LAM_INIT)
        o_ref[pl.ds(pl.multiple_of(i * tq, tq), tq), :] = y_t.T.astype(BF16)

    q_tile(jnp.int32(0), True)
    lax.fori_loop(1, n_q, lambda i, c: q_tile(i, False), 0)
    finish(jnp.int32(n_q - 1))


def _diff_attn(lam_params, q_t, k, v_t, subln_col, batch, seq_len):
    tokens = k.shape[0]
    tq, tk = ATTN_Q, ATTN_K
    return pl.pallas_call(
        _attn_kernel,
        grid=(batch, DA_HEADS),
        in_specs=[
            _const_spec(lam_params.shape),
            pl.BlockSpec((DA_V_DIM, seq_len), lambda b, h: (h, b)),
            pl.BlockSpec((seq_len, DA_V_DIM), lambda b, h: (b, h)),
            pl.BlockSpec((V_EXT_ROWS, seq_len), lambda b, h: (h, b)),
            _const_spec(subln_col.shape),
        ],
        out_specs=pl.BlockSpec((seq_len, DA_V_DIM), lambda b, h: (b, h)),
        out_shape=jax.ShapeDtypeStruct((tokens, DA_WIDTH), BF16),
        scratch_shapes=[
            pltpu.VMEM((V7X_MXU_DEPTH, 2 * tq), BF16),
            pltpu.VMEM((1 + tq // tk, tk, 2 * DA_QK_DIM), BF16),
            pltpu.VMEM((2, tk, 2 * tq), F32),
            pltpu.VMEM((2, tk, 2 * tq), BF16),
            pltpu.VMEM((V_EXT_ROWS, 2 * tq), F32),
        ],
        compiler_params=_compiler_params(2),
        name="diff_attn",
    )(lam_params, q_t, k, v_t, subln_col)


def _mlstm_begin(q_t_ref, k_ref, v_t_ref, g_t_ref, gcum_t_ref, c_ref, m_ref, cols, reset):
    chunk = cols.stop - cols.start
    heads = range(ML_HEADS)
    g_t = g_t_ref[:, cols]
    cum_row = gcum_t_ref[:, cols]
    key_rows = g_t - pltpu.roll(cum_row, ML_HEADS, axis=0)
    key_cols = jnp.concatenate(
        [key_rows, jnp.zeros((V7X_LANES - 2 * ML_HEADS, chunk), F32)], axis=0).T
    st = dict(
        chunk=chunk,
        q_t=[q_t_ref[h * ML_DIM:(h + 1) * ML_DIM, cols] for h in heads],
        k=[k_ref[cols, h * ML_DIM:(h + 1) * ML_DIM] for h in heads],
        v_t=[v_t_ref[h * V_EXT_ROWS:(h + 1) * V_EXT_ROWS, cols] for h in heads],
        c_t=[c_ref[h] for h in heads],
        m_prev=[m_ref[h] for h in heads],
        b_row=[cum_row[ML_HEADS + h:ML_HEADS + h + 1, :] for h in heads],
        i_row=[g_t[h:h + 1, :] for h in heads],
        e_mat=[jnp.broadcast_to(key_cols[:, h:h + 1], (chunk, chunk)) for h in heads],
    )
    if reset is not None:
        st["c_t"] = [jnp.where(reset, 0.0, c) for c in st["c_t"]]
        st["m_prev"] = [jnp.where(reset, 0.0, m) for m in st["m_prev"]]
    st["kq"] = [jnp.dot(st["k"][h], st["q_t"][h], preferred_element_type=F32) for h in heads]
    st["inter_mm"] = [jnp.dot(st["c_t"][h].astype(BF16), st["q_t"][h], preferred_element_type=F32)
                      for h in heads]
    return st


def _mlstm_weights(st):
    chunk = st["chunk"]
    s_id = lax.broadcasted_iota(jnp.int32, (chunk, chunk), 0)
    t_id = lax.broadcasted_iota(jnp.int32, (chunk, chunk), 1)
    causal = s_id <= t_id
    st["m_row"], st["w_inter"], st["sc"] = [], [], []
    for h in range(ML_HEADS):
        d_mat = jnp.where(causal, st["e_mat"][h] + st["b_row"][h], -jnp.inf)
        inter = st["b_row"][h] + st["m_prev"][h]
        m_row = jnp.maximum(inter, jnp.max(d_mat, axis=0, keepdims=True))
        st["m_row"].append(m_row)
        st["w_inter"].append(jnp.exp(inter - m_row))
        st["sc"].append((st["kq"][h] * jnp.exp(d_mat - m_row)).astype(BF16))


def _mlstm_matmuls(st, c_ref, m_ref):
    chunk = st["chunk"]
    heads = range(ML_HEADS)
    st["intra_mm"] = [jnp.dot(st["v_t"][h], st["sc"][h], preferred_element_type=F32) for h in heads]
    for h in heads:
        b_row, m_prev = st["b_row"][h], st["m_prev"][h]
        b_last = b_row[:, chunk - 1:chunk]
        g_row = (b_last - b_row) + st["i_row"][h]
        m_new = jnp.maximum(b_last + m_prev, jnp.max(g_row, axis=1, keepdims=True))
        decay = jnp.exp(b_last + m_prev - m_new)
        wk = jnp.exp(st["e_mat"][h][:, 0:ML_DIM] + (b_last - m_new))
        kw = (st["k"][h].astype(F32) * wk).astype(BF16)
        c_ref[h] = decay * st["c_t"][h] + jnp.dot(st["v_t"][h], kw, preferred_element_type=F32)
        m_ref[h] = m_new


def _mlstm_output(st, o_t_ref, ng_ref, cols):
    outs = []
    for h in range(ML_HEADS):
        num = st["intra_mm"][h] + st["w_inter"][h] * st["inter_mm"][h]
        nq = num[ML_DIM:ML_DIM + 1, :]
        hid = num[0:ML_DIM, :] / jnp.maximum(jnp.abs(nq), jnp.exp(-st["m_row"][h]))
        ms = jnp.mean(hid * hid, axis=0, keepdims=True)
        hn = (hid * lax.rsqrt(ms + NORM_EPS)) * ng_ref[h * ML_DIM:(h + 1) * ML_DIM, :]
        og = o_t_ref[h * ML_DIM:(h + 1) * ML_DIM, cols].astype(F32)
        outs.append((hn * (1.0 / (1.0 + jnp.exp(-og)))).T.astype(BF16))
    return jnp.concatenate(outs, axis=1)


def _rms(x, g):
    ms = jnp.mean(x * x, axis=-1, keepdims=True)
    return (x * lax.rsqrt(ms + NORM_EPS)) * g


def _out_ffn_kernel(tiles_per_seq, n_tiles,
                    x_ref, attn_ref, mq_t_ref, mk_ref, mv_t_ref, mo_t_ref, g_t_ref, gcum_t_ref, ng_ref,
                    w_out_ref, g_ffn_ref, w_gate_ref, w_up_ref, w_down_ref, g_final_ref,
                    out_ref, ml_ref, c_ref, m_ref):
    i = pl.program_id(0)
    reset = (jnp.minimum(i, n_tiles - 1) % tiles_per_seq) == 0
    chunks = [slice(c * ML_CHUNK, (c + 1) * ML_CHUNK) for c in range(FFN_ROWS // ML_CHUNK)]
    ml_args = (mq_t_ref, mk_ref, mv_t_ref, g_t_ref, gcum_t_ref, c_ref, m_ref)

    @pl.when(i == 0)
    def _():
        ml_ref[...] = jnp.zeros(ml_ref.shape, BF16)
        c_ref[...] = jnp.zeros(c_ref.shape, F32)
        m_ref[...] = jnp.zeros(m_ref.shape, F32)

    halves = chunks
    mix = jnp.concatenate([attn_ref[...], ml_ref[...]], axis=1)
    first = _mlstm_begin(*ml_args, chunks[0], reset)
    y = [x_ref[hs, :] + jnp.dot(mix[hs], w_out_ref[...], preferred_element_type=F32) for hs in halves]
    h2 = [_rms(yh, g_ffn_ref[...]).astype(BF16) for yh in y]
    _mlstm_weights(first)
    _mlstm_matmuls(first, c_ref, m_ref)
    gate = [jnp.dot(h, w_gate_ref[...], preferred_element_type=F32) for h in h2]
    ml_ref[chunks[0], :] = _mlstm_output(first, mo_t_ref, ng_ref, chunks[0])
    second = _mlstm_begin(*ml_args, chunks[1], None)
    up = [jnp.dot(h, w_up_ref[...], preferred_element_type=F32) for h in h2]
    _mlstm_weights(second)
    _mlstm_matmuls(second, c_ref, m_ref)
    act = [((g * (1.0 / (1.0 + jnp.exp(-g)))) * u).astype(BF16) for g, u in zip(gate, up)]
    y2 = [yh + jnp.dot(a, w_down_ref[...], preferred_element_type=F32) for yh, a in zip(y, act)]
    ml_ref[chunks[1], :] = _mlstm_output(second, mo_t_ref, ng_ref, chunks[1])
    for hs, yh in zip(halves, y2):
        out_ref[hs, :] = _rms(yh, g_final_ref[...])


def _out_ffn(x2d, attn, mq_t, mk, mv_t, mo_t, g_t, gcum_t, gain,
             w_out, g_ffn, w_gate, w_up, w_down, g_final, seq_len):
    tokens = x2d.shape[0]
    rows = FFN_ROWS
    n_tiles = tokens // rows
    ffn_blk = lambda width: pl.BlockSpec((rows, width), lambda i: (jnp.maximum(i - 1, 0), 0))
    ml_row = lambda width: pl.BlockSpec((rows, width), lambda i: (jnp.minimum(i, n_tiles - 1), 0))
    ml_col = lambda height: pl.BlockSpec((height, rows), lambda i: (0, jnp.minimum(i, n_tiles - 1)))
    return pl.pallas_call(
        functools.partial(_out_ffn_kernel, seq_len // rows, n_tiles),
        grid=(n_tiles + 1,),
        in_specs=[
            ffn_blk(D_MODEL), ffn_blk(DA_WIDTH),
            ml_col(ML_WIDTH), ml_row(ML_WIDTH), ml_col(ML_HEADS * V_EXT_ROWS), ml_col(ML_WIDTH),
            ml_col(2 * ML_HEADS), ml_col(2 * ML_HEADS),
            _const_spec(gain.shape),
            _const_spec(w_out.shape), _const_spec(g_ffn.shape),
            _const_spec(w_gate.shape), _const_spec(w_up.shape), _const_spec(w_down.shape),
            _const_spec(g_final.shape),
        ],
        out_specs=ffn_blk(D_MODEL),
        out_shape=jax.ShapeDtypeStruct((tokens, D_MODEL), F32),
        scratch_shapes=[pltpu.VMEM((rows, ML_WIDTH), BF16),
                        pltpu.VMEM((ML_HEADS, V_EXT_ROWS, ML_DIM), F32),
                        pltpu.VMEM((ML_HEADS, 1, 1), F32)],
        compiler_params=_compiler_params(1),
        name="out_ffn",
    )(x2d, attn, mq_t, mk, mv_t, mo_t, g_t, gcum_t, gain,
      w_out, g_ffn, w_gate, w_up, w_down, g_final)


def kernel(x, positions, mix_norm_g, w_in, da_lambda, da_subln_g, ml_conv_w, ml_conv_b, ml_gate_b,
           ml_norm_g, w_out, ffn_norm_g, w_gate, w_up, w_down, final_norm_g):
    batch, seq_len, _ = x.shape
    tokens = batch * seq_len
    depth = w_in.shape[0]
    assert depth == 1, "one trunk layer"
    assert seq_len % PROJ_ROWS == 0 and seq_len % ATTN_Q == 0 and seq_len % ML_CHUNK == 0
    assert ATTN_Q % ATTN_K == 0 and ATTN_K % CHUNK == 0 and tokens % FFN_ROWS == 0
    assert 2 * DA_QK_DIM + MASK_FEATS <= V7X_MXU_DEPTH and MASK_FEATS <= 2 * DA_QK_DIM
    assert DA_V_DIM == ML_DIM and DA_HEADS == ML_HEADS and FFN_ROWS == 2 * ML_CHUNK
    assert PROJ_ROWS % ML_CHUNK == 0 and w_gate.shape[1:] == (D_MODEL, D_FF)

    x2d = x.reshape(tokens, D_MODEL)
    pos3d = positions.reshape(tokens // PROJ_ROWS, 1, PROJ_ROWS)

    gate_pad = V7X_BF16_ROWS_PER_VREG - 2 * ML_HEADS
    ml_gate_w_t = jnp.pad(w_in[0][:, OFF_GATE:].T.astype(BF16), ((0, gate_pad), (0, 0)))
    inv_freq = (ROPE_THETA ** (-jnp.arange(0, ROT_DIM, 2, dtype=F32) / ROT_DIM)).reshape(ROT_HALF, 1)
    gb_col = ml_gate_b[0].astype(F32).reshape(2 * ML_HEADS, 1)

    q_t, k, v_t, mq_t, mk, mv_t, mo_t, g_t, gcum_t = _in_proj(
        x2d, pos3d, mix_norm_g[0].reshape(1, D_MODEL).astype(F32), w_in[0], ml_gate_w_t, inv_freq,
        ml_conv_w[0].astype(F32), ml_conv_b[0].reshape(1, 2 * ML_WIDTH).astype(F32),
        gb_col, seq_len)

    attn = _diff_attn(da_lambda[0].astype(F32), q_t, k, v_t,
                      da_subln_g[0].astype(F32).reshape(DA_V_DIM, 1), batch, seq_len)
    ml_gain = jnp.broadcast_to(ml_norm_g[0].astype(F32).reshape(ML_WIDTH, 1), (ML_WIDTH, ML_CHUNK))

    out = _out_ffn(x2d, attn, mq_t, mk, mv_t, mo_t, g_t, gcum_t, ml_gain, w_out[0].astype(BF16),
                   ffn_norm_g[0].reshape(1, D_MODEL).astype(F32),
                   w_gate[0].astype(BF16), w_up[0].astype(BF16), w_down[0].astype(BF16),
                   final_norm_g.reshape(1, D_MODEL).astype(F32), seq_len)
    return out.reshape(batch, seq_len, D_MODEL)
```

```python
import functools
import math

import jax
import jax.numpy as jnp
from jax import lax
from jax.experimental import pallas as pl
from jax.experimental.pallas import tpu as pltpu

F32 = jnp.float32
BF16 = jnp.bfloat16

D_MODEL = 1024
CHUNK = 64
NORM_EPS = 1e-6
DA_HEADS = 4
DA_QK_DIM = 64
DA_V_DIM = 128
DA_WIDTH = DA_HEADS * DA_V_DIM
ROPE_THETA = 500000.0
ROT_DIM = DA_QK_DIM // 4
ROT_HALF = ROT_DIM // 2
ML_HEADS = 4
ML_DIM = 128
ML_WIDTH = ML_HEADS * ML_DIM
CONV_WIDTH = 4
D_FF = 2816
LAM_INIT = 0.8 - 0.6 * math.exp(-0.3 * 0)
Q_SCALE = DA_QK_DIM ** -0.5 * math.log2(math.e)

OFF_DA_Q = 0
OFF_DA_K = 512
OFF_DA_V = 1024
OFF_ML = 1536
OFF_GATE = 3584
OFF_ML_V = OFF_ML + 2 * ML_WIDTH
ZT_ML_V = OFF_ML
ZT_ML_O = ZT_ML_V + ML_WIDTH
ZT_GATE = ZT_ML_O + ML_WIDTH

V7X_LANES = 128
V7X_SUBLANES = 8
V7X_BF16_ROWS_PER_VREG = 16
V7X_MXU_DEPTH = 256
V7X_VMEM_LIMIT_BYTES = 56 * 1024 * 1024

PROJ_ROWS = 1024
ATTN_Q = 1024
ATTN_K = 512
MASK_FEATS = ATTN_Q // CHUNK
MASK_BIG = 1e30
V_EXT_ROWS = DA_V_DIM + V7X_BF16_ROWS_PER_VREG
ML_CHUNK = 256
FFN_ROWS = 512
CONV_HALO = V7X_SUBLANES
CONV_COLS = 256
W_CAST_ROWS = 256

_NT = (((1,), (1,)), ((), ()))


def _compiler_params(n_axes):
    return pltpu.CompilerParams(
        dimension_semantics=("arbitrary",) * n_axes,
        vmem_limit_bytes=V7X_VMEM_LIMIT_BYTES,
    )


def _const_spec(shape):
    zeros = (0,) * len(shape)
    return pl.BlockSpec(shape, lambda *_: zeros, pipeline_mode=pl.Buffered(1))


def _rope_rows(zt, cos, sin):
    pieces = []
    for g in range(2):
        base = g * DA_QK_DIM
        x1 = zt[base:base + ROT_HALF]
        x2 = zt[base + ROT_HALF:base + ROT_DIM]
        pieces += [x1 * cos - x2 * sin, x2 * cos + x1 * sin, zt[base + ROT_DIM:base + DA_QK_DIM]]
    return jnp.concatenate(pieces, axis=0)


def _in_proj_kernel(tiles_per_seq,
                    x_ref, pos_ref, g_ref, w_in_t_ref, invf_ref,
                    convw_ref, convb_ref, gb_col_ref,
                    q_t_ref, k_ref, v_t_ref, mq_t_ref, mk_ref, mv_t_ref, mo_t_ref, g_t_ref, gcum_t_ref,
                    halo_ref, w_t_ref, w_row_ref):
    rows = x_ref.shape[0]

    @pl.when(pl.program_id(0) == 0)
    def _():
        for dst, src, height in ((OFF_DA_Q, OFF_DA_Q, OFF_ML), (ZT_ML_V, OFF_ML_V, 2 * ML_WIDTH)):
            for r in range(0, height, W_CAST_ROWS):
                w_t_ref[dst + r:dst + r + W_CAST_ROWS, :] = (
                    w_in_t_ref[src + r:src + r + W_CAST_ROWS, :].astype(BF16))
        gate_w_t = w_in_t_ref[OFF_GATE:OFF_GATE + 2 * ML_HEADS, :]
        w_t_ref[ZT_GATE:ZT_GATE + V7X_BF16_ROWS_PER_VREG, :] = jnp.concatenate(
            [gate_w_t, jnp.zeros((V7X_BF16_ROWS_PER_VREG - 2 * ML_HEADS, D_MODEL), F32)],
            axis=0).astype(BF16)
        for c in range(0, 2 * ML_WIDTH, W_CAST_ROWS):
            w_row_ref[:, c:c + W_CAST_ROWS] = (
                w_in_t_ref[OFF_ML + c:OFF_ML + c + W_CAST_ROWS, :].T.astype(BF16))

    x = x_ref[...]
    ms = jnp.mean(x * x, axis=-1, keepdims=True)
    hb = ((x * lax.rsqrt(ms + NORM_EPS)) * g_ref[...]).astype(BF16)

    first = (pl.program_id(0) % tiles_per_seq) == 0
    halo_ref[0:CONV_HALO, :] = jnp.where(first, 0.0, halo_ref[rows:rows + CONV_HALO, :])
    halo_ref[CONV_HALO:CONV_HALO + rows, :] = jnp.dot(hb, w_row_ref[...], preferred_element_type=F32)

    def proj_t(lo, hi):
        return lax.dot_general(w_t_ref[lo:hi, :], hb, _NT, preferred_element_type=F32)

    zqk_t = proj_t(OFF_DA_Q, OFF_DA_V)

    def conv_group(g):
        cols = slice(g * CONV_COLS, (g + 1) * CONV_COLS)
        xe = halo_ref[:, cols]
        conv = convw_ref[0:1, cols] * xe
        for j in range(1, CONV_WIDTH):
            conv = pltpu.roll(conv, 1, axis=0) + convw_ref[j:j + 1, cols] * xe
        conv = conv[CONV_HALO:, :] + convb_ref[:, cols]
        act = conv * (1.0 / (1.0 + jnp.exp(-conv)))
        if cols.stop <= ML_WIDTH:
            mq_t_ref[cols, :] = (act * (ML_DIM ** -0.5)).T.astype(BF16)
        else:
            mk_ref[:, cols.start - ML_WIDTH:cols.stop - ML_WIDTH] = act.astype(BF16)

    conv_groups = iter(range(2 * ML_WIDTH // CONV_COLS))
    conv_group(next(conv_groups))

    ang = invf_ref[...] * pos_ref[0].astype(F32)
    cos = jnp.cos(ang)
    sin = jnp.sin(ang)
    for h in range(DA_HEADS):
        lo = h * 2 * DA_QK_DIM
        hi = lo + 2 * DA_QK_DIM
        q_rot = _rope_rows(zqk_t[OFF_DA_Q + lo:OFF_DA_Q + hi], cos, sin)
        q_t_ref[lo:hi, :] = (q_rot * Q_SCALE).astype(BF16)
        k_rot = _rope_rows(zqk_t[OFF_DA_K + lo:OFF_DA_K + hi], cos, sin)
        k_ref[:, lo:hi] = k_rot.T.astype(BF16)
    pad_rows = V_EXT_ROWS - DA_V_DIM
    ones_row = (lax.broadcasted_iota(jnp.int32, (pad_rows, rows), 0) == 0).astype(BF16)
    for out_ref, base in ((v_t_ref, OFF_DA_V), (mv_t_ref, ZT_ML_V)):
        zv_t = proj_t(base, base + DA_WIDTH)
        conv_group(next(conv_groups))
        for h in range(DA_HEADS):
            v_lo = h * DA_V_DIM
            out_ref[h * V_EXT_ROWS:h * V_EXT_ROWS + DA_V_DIM, :] = zv_t[v_lo:v_lo + DA_V_DIM].astype(BF16)
            out_ref[h * V_EXT_ROWS + DA_V_DIM:(h + 1) * V_EXT_ROWS, :] = ones_row
    zo_t = proj_t(ZT_ML_O, ZT_GATE + V7X_BF16_ROWS_PER_VREG)
    conv_group(next(conv_groups))
    mo_t_ref[...] = zo_t[0:ML_WIDTH].astype(BF16)

    a_t = zo_t[ML_WIDTH:ML_WIDTH + 2 * ML_HEADS] + gb_col_ref[...]
    ls_t = jnp.minimum(a_t, 0.0) - jnp.log1p(jnp.exp(-jnp.abs(a_t)))
    row_id = lax.broadcasted_iota(jnp.int32, a_t.shape, 0)
    log_gates = jnp.where(row_id < ML_HEADS, a_t, ls_t)
    g_t_ref[...] = log_gates
    src = lax.broadcasted_iota(jnp.int32, (rows, rows), 0)
    dst = lax.broadcasted_iota(jnp.int32, (rows, rows), 1)
    tri = ((src <= dst) & (src // ML_CHUNK == dst // ML_CHUNK)).astype(F32)
    gcum_t_ref[...] = jnp.dot(log_gates, tri, preferred_element_type=F32,
                              precision=lax.Precision.HIGHEST)


def _in_proj(x2d, pos3d, norm_g, w_in_t, inv_freq, conv_w, conv_b, gb_col, seq_len):
    tokens = x2d.shape[0]
    rows = PROJ_ROWS
    n_tiles = tokens // rows
    row_blk = lambda width: pl.BlockSpec((rows, width), lambda i: (i, 0))
    col_blk = lambda height: pl.BlockSpec((height, rows), lambda i: (0, i))
    out_shape = (
        jax.ShapeDtypeStruct((DA_WIDTH, tokens), BF16),
        jax.ShapeDtypeStruct((tokens, DA_WIDTH), BF16),
        jax.ShapeDtypeStruct((DA_HEADS * V_EXT_ROWS, tokens), BF16),
        jax.ShapeDtypeStruct((ML_WIDTH, tokens), BF16),
        jax.ShapeDtypeStruct((tokens, ML_WIDTH), BF16),
        jax.ShapeDtypeStruct((ML_HEADS * V_EXT_ROWS, tokens), BF16),
        jax.ShapeDtypeStruct((ML_WIDTH, tokens), BF16),
        jax.ShapeDtypeStruct((2 * ML_HEADS, tokens), F32),
        jax.ShapeDtypeStruct((2 * ML_HEADS, tokens), F32),
    )
    return pl.pallas_call(
        functools.partial(_in_proj_kernel, seq_len // rows),
        grid=(n_tiles,),
        in_specs=[
            row_blk(D_MODEL),
            pl.BlockSpec((1, 1, rows), lambda i: (i, 0, 0)),
            _const_spec(norm_g.shape),
            _const_spec(w_in_t.shape),
            _const_spec(inv_freq.shape),
            _const_spec(conv_w.shape),
            _const_spec(conv_b.shape),
            _const_spec(gb_col.shape),
        ],
        out_specs=(
            col_blk(DA_WIDTH), row_blk(DA_WIDTH), col_blk(DA_HEADS * V_EXT_ROWS),
            col_blk(ML_WIDTH), row_blk(ML_WIDTH), col_blk(ML_HEADS * V_EXT_ROWS), col_blk(ML_WIDTH),
            col_blk(2 * ML_HEADS), col_blk(2 * ML_HEADS),
        ),
        out_shape=out_shape,
        scratch_shapes=[
            pltpu.VMEM((rows + CONV_HALO, 2 * ML_WIDTH), F32),
            pltpu.VMEM((ZT_GATE + V7X_BF16_ROWS_PER_VREG, D_MODEL), BF16),
            pltpu.VMEM((D_MODEL, 2 * ML_WIDTH), BF16),
        ],
        compiler_params=_compiler_params(1),
        name="in_proj",
    )(x2d, pos3d, norm_g, w_in_t, inv_freq, conv_w, conv_b, gb_col)


def _attn_kernel(lam_ref, q_t_ref, k_ref, v_t_ref, g_ref, o_ref,
                 qm_ref, ind_ref, s_ref, p_ref, acc_ref):
    seq = k_ref.shape[0]
    tq, tk = ATTN_Q, ATTN_K
    n_q = seq // tq
    diag_tiles = tq // tk

    lv = lam_ref[...]
    lam = (jnp.exp(jnp.sum(lv[0:1] * lv[1:2], axis=1, keepdims=True))
           - jnp.exp(jnp.sum(lv[2:3] * lv[3:4], axis=1, keepdims=True)) + LAM_INIT)

    feat = lax.broadcasted_iota(jnp.int32, (MASK_FEATS, 2 * tq), 0)
    qchunk = (lax.broadcasted_iota(jnp.int32, (MASK_FEATS, 2 * tq), 1) % tq) // CHUNK
    qm_ref[...] = jnp.zeros(qm_ref.shape, BF16)
    qm_ref[2 * DA_QK_DIM:2 * DA_QK_DIM + MASK_FEATS, :] = jnp.where(
        feat > qchunk, -MASK_BIG, 0.0).astype(BF16)
    kchunk = lax.broadcasted_iota(jnp.int32, (tk, 2 * DA_QK_DIM), 0) // CHUNK
    lane = lax.broadcasted_iota(jnp.int32, (tk, 2 * DA_QK_DIM), 1)
    ind_ref[0] = jnp.zeros((tk, 2 * DA_QK_DIM), BF16)
    for d in range(diag_tiles):
        ind_ref[d + 1] = (lane == kchunk + d * (tk // CHUNK)).astype(BF16)

    def q_tile(i, first_tile):
        q_off = pl.multiple_of(i * tq, tq)
        t_diag = diag_tiles * i
        n_t = t_diag + diag_tiles

        def score(par, t, off=0, which=None):
            k_t = k_ref[pl.ds(pl.multiple_of(t * tk, tk), tk), :]
            if which is None:
                which = jnp.maximum(t - t_diag + 1, 0)
            k_ext = jnp.concatenate([k_t, ind_ref[which]], axis=1)
            if off:
                rhs = jnp.concatenate([qm_ref[:, off:tq], qm_ref[:, tq + off:2 * tq]], axis=1)
            else:
                rhs = qm_ref[...]
            w = tq - off
            s = jnp.dot(k_ext, rhs, preferred_element_type=F32)
            s_ref[par, :, 0:2 * w] = s
            return tuple(jnp.max(s[:, mi * w:(mi + 1) * w], axis=0, keepdims=True) for mi in range(2))

        def softmax(par, maxes, tile_max, off=0):
            w = tq - off
            new, alphas = [], []
            for mi in range(2):
                lanes = slice(mi * w, (mi + 1) * w)
                s = s_ref[par, :, lanes]
                m_old = maxes[mi][:, off:tq]
                m_new = jnp.maximum(m_old, tile_max[mi])
                p_ref[par, :, lanes] = jnp.exp2(s - m_new).astype(BF16)
                alphas.append(jnp.exp2(m_old - m_new))
                new.append(jnp.concatenate([maxes[mi][:, 0:off], m_new], axis=1) if off else m_new)
            return tuple(new), tuple(alphas)

        def accumulate(par, t, alphas, off=0):
            w = tq - off
            v_t = v_t_ref[:, pl.ds(pl.multiple_of(t * tk, tk), tk)]
            pv = jnp.dot(v_t, p_ref[par, :, 0:2 * w], preferred_element_type=F32)
            for mi in range(2):
                lanes = slice(mi * tq + off, (mi + 1) * tq)
                acc_ref[:, lanes] = alphas[mi] * acc_ref[:, lanes] + pv[:, mi * w:(mi + 1) * w]

        def step(par, t, carry, first=False):
            maxes, alphas, tile_max = carry
            next_max = score(1 - par, t + 1)
            if not first:
                accumulate(1 - par, t - 1, alphas)
            return softmax(par, maxes, tile_max) + (next_max,)

        q_t = q_t_ref[:, pl.ds(q_off, tq)]
        qm_ref[0:DA_QK_DIM, 0:tq] = q_t[0:DA_QK_DIM]
        qm_ref[DA_QK_DIM:2 * DA_QK_DIM, tq:2 * tq] = q_t[DA_QK_DIM:]
        tile_max = score(0, 0)
        if not first_tile:
            finish(i - 1)
        acc_ref[...] = jnp.zeros(acc_ref.shape, F32)

        neg = jnp.full((1, tq), -jnp.inf, F32)
        maxes, alphas = (neg, neg), None
        if not first_tile:
            carry = step(0, 0, (maxes, alphas, tile_max), first=True)

            def pair(u, carry):
                carry = step(1, 2 * u + 1, carry)
                return step(0, 2 * u + 2, carry)

            carry = lax.fori_loop(0, t_diag // 2 - 1, pair, carry)
            maxes, alphas, tile_max = step(1, t_diag - 1, carry)
        for d in range(diag_tiles):
            t, par = t_diag + d, d % 2
            if d + 1 < diag_tiles:
                next_max = score(1 - par, t + 1, off=(d + 1) * tk, which=d + 2)
            if d or not first_tile:
                accumulate(1 - par, t - 1, alphas, off=max(d - 1, 0) * tk)
            maxes, alphas = softmax(par, maxes, tile_max, off=d * tk)
            tile_max = next_max
        accumulate((diag_tiles - 1) % 2, n_t - 1, alphas, off=(diag_tiles - 1) * tk)
        return 0

    def finish(i):
        acc = acc_ref[...]
        o1 = acc[0:DA_V_DIM, 0:tq] / acc[DA_V_DIM:DA_V_DIM + 1, 0:tq]
        o2 = acc[0:DA_V_DIM, tq:2 * tq] / acc[DA_V_DIM:DA_V_DIM + 1, tq:2 * tq]
        o_t = o1 - lam * o2
        ms = jnp.mean(o_t * o_t, axis=0, keepdims=True)
        y_t = (o_t * lax.rsqrt(ms + NORM_EPS)) * g_ref[...] * (1.0 - LAM_INIT)
        o_ref[pl.ds(pl.multiple_of(i * tq, tq), tq), :] = y_t.T.astype(BF16)

    q_tile(jnp.int32(0), True)
    lax.fori_loop(1, n_q, lambda i, c: q_tile(i, False), 0)
    finish(jnp.int32(n_q - 1))


def _diff_attn(lam_params, q_t, k, v_t, subln_col, batch, seq_len):
    tokens = k.shape[0]
    tq, tk = ATTN_Q, ATTN_K
    return pl.pallas_call(
        _attn_kernel,
        grid=(batch, DA_HEADS),
        in_specs=[
            _const_spec(lam_params.shape),
            pl.BlockSpec((DA_V_DIM, seq_len), lambda b, h: (h, b)),
            pl.BlockSpec((seq_len, DA_V_DIM), lambda b, h: (b, h)),
            pl.BlockSpec((V_EXT_ROWS, seq_len), lambda b, h: (h, b)),
            _const_spec(subln_col.shape),
        ],
        out_specs=pl.BlockSpec((seq_len, DA_V_DIM), lambda b, h: (b, h)),
        out_shape=jax.ShapeDtypeStruct((tokens, DA_WIDTH), BF16),
        scratch_shapes=[
            pltpu.VMEM((V7X_MXU_DEPTH, 2 * tq), BF16),
            pltpu.VMEM((1 + tq // tk, tk, 2 * DA_QK_DIM), BF16),
            pltpu.VMEM((2, tk, 2 * tq), F32),
            pltpu.VMEM((2, tk, 2 * tq), BF16),
            pltpu.VMEM((V_EXT_ROWS, 2 * tq), F32),
        ],
        compiler_params=_compiler_params(2),
        name="diff_attn",
    )(lam_params, q_t, k, v_t, subln_col)


def _mlstm_begin(q_t_ref, k_ref, v_t_ref, g_t_ref, gcum_t_ref, c_ref, m_ref, cols, reset):
    chunk = cols.stop - cols.start
    heads = range(ML_HEADS)
    g_t = g_t_ref[:, cols]
    cum_row = gcum_t_ref[:, cols]
    key_rows = g_t - pltpu.roll(cum_row, ML_HEADS, axis=0)
    key_cols = jnp.concatenate(
        [key_rows, jnp.zeros((V7X_LANES - 2 * ML_HEADS, chunk), F32)], axis=0).T
    st = dict(
        chunk=chunk,
        q_t=[q_t_ref[h * ML_DIM:(h + 1) * ML_DIM, cols] for h in heads],
        k=[k_ref[cols, h * ML_DIM:(h + 1) * ML_DIM] for h in heads],
        v_t=[v_t_ref[h * V_EXT_ROWS:(h + 1) * V_EXT_ROWS, cols] for h in heads],
        c_t=[c_ref[h] for h in heads],
        m_prev=[m_ref[h] for h in heads],
        b_row=[cum_row[ML_HEADS + h:ML_HEADS + h + 1, :] for h in heads],
        i_row=[g_t[h:h + 1, :] for h in heads],
        e_mat=[jnp.broadcast_to(key_cols[:, h:h + 1], (chunk, chunk)) for h in heads],
    )
    if reset is not None:
        st["c_t"] = [jnp.where(reset, 0.0, c) for c in st["c_t"]]
        st["m_prev"] = [jnp.where(reset, 0.0, m) for m in st["m_prev"]]
    st["kq"] = [jnp.dot(st["k"][h], st["q_t"][h], preferred_element_type=F32) for h in heads]
    st["inter_mm"] = [jnp.dot(st["c_t"][h].astype(BF16), st["q_t"][h], preferred_element_type=F32)
                      for h in heads]
    return st


def _mlstm_weights(st):
    chunk = st["chunk"]
    s_id = lax.broadcasted_iota(jnp.int32, (chunk, chunk), 0)
    t_id = lax.broadcasted_iota(jnp.int32, (chunk, chunk), 1)
    causal = s_id <= t_id
    st["m_row"], st["w_inter"], st["sc"] = [], [], []
    for h in range(ML_HEADS):
        d_mat = jnp.where(causal, st["e_mat"][h] + st["b_row"][h], -jnp.inf)
        inter = st["b_row"][h] + st["m_prev"][h]
        m_row = jnp.maximum(inter, jnp.max(d_mat, axis=0, keepdims=True))
        st["m_row"].append(m_row)
        st["w_inter"].append(jnp.exp(inter - m_row))
        st["sc"].append((st["kq"][h] * jnp.exp(d_mat - m_row)).astype(BF16))


def _mlstm_matmuls(st, c_ref, m_ref):
    chunk = st["chunk"]
    heads = range(ML_HEADS)
    st["intra_mm"] = [jnp.dot(st["v_t"][h], st["sc"][h], preferred_element_type=F32) for h in heads]
    for h in heads:
        b_row, m_prev = st["b_row"][h], st["m_prev"][h]
        b_last = b_row[:, chunk - 1:chunk]
        g_row = (b_last - b_row) + st["i_row"][h]
        m_new = jnp.maximum(b_last + m_prev, jnp.max(g_row, axis=1, keepdims=True))
        decay = jnp.exp(b_last + m_prev - m_new)
        wk = jnp.exp(st["e_mat"][h][:, 0:ML_DIM] + (b_last - m_new))
        kw = (st["k"][h].astype(F32) * wk).astype(BF16)
        c_ref[h] = decay * st["c_t"][h] + jnp.dot(st["v_t"][h], kw, preferred_element_type=F32)
        m_ref[h] = m_new


def _mlstm_output(st, o_t_ref, ng_ref, cols):
    outs = []
    for h in range(ML_HEADS):
        num = st["intra_mm"][h] + st["w_inter"][h] * st["inter_mm"][h]
        nq = num[ML_DIM:ML_DIM + 1, :]
        hid = num[0:ML_DIM, :] / jnp.maximum(jnp.abs(nq), jnp.exp(-st["m_row"][h]))
        ms = jnp.mean(hid * hid, axis=0, keepdims=True)
        hn = (hid * lax.rsqrt(ms + NORM_EPS)) * ng_ref[h * ML_DIM:(h + 1) * ML_DIM, :]
        og = o_t_ref[h * ML_DIM:(h + 1) * ML_DIM, cols].astype(F32)
        outs.append((hn * (1.0 / (1.0 + jnp.exp(-og)))).T.astype(BF16))
    return jnp.concatenate(outs, axis=1)


def _rms(x, g):
    ms = jnp.mean(x * x, axis=-1, keepdims=True)
    return (x * lax.rsqrt(ms + NORM_EPS)) * g


def _out_ffn_kernel(tiles_per_seq, n_tiles,
                    x_ref, attn_ref, mq_t_ref, mk_ref, mv_t_ref, mo_t_ref, g_t_ref, gcum_t_ref, ng_ref,
                    w_out_ref, g_ffn_ref, w_gate_ref, w_up_ref, w_down_ref, g_final_ref,
                    out_ref, ml_ref, c_ref, m_ref):
    i = pl.program_id(0)
    reset = (jnp.minimum(i, n_tiles - 1) % tiles_per_seq) == 0
    chunks = [slice(c * ML_CHUNK, (c + 1) * ML_CHUNK) for c in range(FFN_ROWS // ML_CHUNK)]
    ml_args = (mq_t_ref, mk_ref, mv_t_ref, g_t_ref, gcum_t_ref, c_ref, m_ref)

    @pl.when(i == 0)
    def _():
        ml_ref[...] = jnp.zeros(ml_ref.shape, BF16)
        c_ref[...] = jnp.zeros(c_ref.shape, F32)
        m_ref[...] = jnp.zeros(m_ref.shape, F32)

    halves = chunks
    mix = jnp.concatenate([attn_ref[...], ml_ref[...]], axis=1)
    first = _mlstm_begin(*ml_args, chunks[0], reset)
    y = [x_ref[hs, :] + jnp.dot(mix[hs], w_out_ref[...], preferred_element_type=F32) for hs in halves]
    h2 = [_rms(yh, g_ffn_ref[...]).astype(BF16) for yh in y]
    _mlstm_weights(first)
    _mlstm_matmuls(first, c_ref, m_ref)
    gate = [jnp.dot(h, w_gate_ref[...], preferred_element_type=F32) for h in h2]
    ml_ref[chunks[0], :] = _mlstm_output(first, mo_t_ref, ng_ref, chunks[0])
    second = _mlstm_begin(*ml_args, chunks[1], None)
    up = [jnp.dot(h, w_up_ref[...], preferred_element_type=F32) for h in h2]
    _mlstm_weights(second)
    _mlstm_matmuls(second, c_ref, m_ref)
    act = [((g * (1.0 / (1.0 + jnp.exp(-g)))) * u).astype(BF16) for g, u in zip(gate, up)]
    y2 = [yh + jnp.dot(a, w_down_ref[...], preferred_element_type=F32) for yh, a in zip(y, act)]
    ml_ref[chunks[1], :] = _mlstm_output(second, mo_t_ref, ng_ref, chunks[1])
    for hs, yh in zip(halves, y2):
        out_ref[hs, :] = _rms(yh, g_final_ref[...])


def _out_ffn(x2d, attn, mq_t, mk, mv_t, mo_t, g_t, gcum_t, gain,
             w_out, g_ffn, w_gate, w_up, w_down, g_final, seq_len):
    tokens = x2d.shape[0]
    rows = FFN_ROWS
    n_tiles = tokens // rows
    ffn_blk = lambda width: pl.BlockSpec((rows, width), lambda i: (jnp.maximum(i - 1, 0), 0))
    ml_row = lambda width: pl.BlockSpec((rows, width), lambda i: (jnp.minimum(i, n_tiles - 1), 0))
    ml_col = lambda height: pl.BlockSpec((height, rows), lambda i: (0, jnp.minimum(i, n_tiles - 1)))
    return pl.pallas_call(
        functools.partial(_out_ffn_kernel, seq_len // rows, n_tiles),
        grid=(n_tiles + 1,),
        in_specs=[
            ffn_blk(D_MODEL), ffn_blk(DA_WIDTH),
            ml_col(ML_WIDTH), ml_row(ML_WIDTH), ml_col(ML_HEADS * V_EXT_ROWS), ml_col(ML_WIDTH),
            ml_col(2 * ML_HEADS), ml_col(2 * ML_HEADS),
            _const_spec(gain.shape),
            _const_spec(w_out.shape), _const_spec(g_ffn.shape),
            _const_spec(w_gate.shape), _const_spec(w_up.shape), _const_spec(w_down.shape),
            _const_spec(g_final.shape),
        ],
        out_specs=ffn_blk(D_MODEL),
        out_shape=jax.ShapeDtypeStruct((tokens, D_MODEL), F32),
        scratch_shapes=[pltpu.VMEM((rows, ML_WIDTH), BF16),
                        pltpu.VMEM((ML_HEADS, V_EXT_ROWS, ML_DIM), F32),
                        pltpu.VMEM((ML_HEADS, 1, 1), F32)],
        compiler_params=_compiler_params(1),
        name="out_ffn",
    )(x2d, attn, mq_t, mk, mv_t, mo_t, g_t, gcum_t, gain,
      w_out, g_ffn, w_gate, w_up, w_down, g_final)


def kernel(x, positions, mix_norm_g, w_in, da_lambda, da_subln_g, ml_conv_w, ml_conv_b, ml_gate_b,
           ml_norm_g, w_out, ffn_norm_g, w_gate, w_up, w_down, final_norm_g):
    batch, seq_len, _ = x.shape
    tokens = batch * seq_len
    depth = w_in.shape[0]
    assert depth == 1, "one trunk layer"
    assert seq_len % PROJ_ROWS == 0 and seq_len % ATTN_Q == 0 and seq_len % ML_CHUNK == 0
    assert ATTN_Q % ATTN_K == 0 and ATTN_K % CHUNK == 0 and tokens % FFN_ROWS == 0
    assert 2 * DA_QK_DIM + MASK_FEATS <= V7X_MXU_DEPTH and MASK_FEATS <= 2 * DA_QK_DIM
    assert DA_V_DIM == ML_DIM and DA_HEADS == ML_HEADS and FFN_ROWS == 2 * ML_CHUNK
    assert PROJ_ROWS % ML_CHUNK == 0 and w_gate.shape[1:] == (D_MODEL, D_FF)

    x2d = x.reshape(tokens, D_MODEL)
    pos3d = positions.reshape(tokens // PROJ_ROWS, 1, PROJ_ROWS)

    w_in_t = w_in[0].T
    inv_freq = (ROPE_THETA ** (-jnp.arange(0, ROT_DIM, 2, dtype=F32) / ROT_DIM)).reshape(ROT_HALF, 1)
    gb_col = ml_gate_b[0].astype(F32).reshape(2 * ML_HEADS, 1)

    q_t, k, v_t, mq_t, mk, mv_t, mo_t, g_t, gcum_t = _in_proj(
        x2d, pos3d, mix_norm_g[0].reshape(1, D_MODEL).astype(F32), w_in_t, inv_freq,
        ml_conv_w[0].astype(F32), ml_conv_b[0].reshape(1, 2 * ML_WIDTH).astype(F32),
        gb_col, seq_len)

    attn = _diff_attn(da_lambda[0].astype(F32), q_t, k, v_t,
                      da_subln_g[0].astype(F32).reshape(DA_V_DIM, 1), batch, seq_len)
    ml_gain = jnp.broadcast_to(ml_norm_g[0].astype(F32).reshape(ML_WIDTH, 1), (ML_WIDTH, ML_CHUNK))

    out = _out_ffn(x2d, attn, mq_t, mk, mv_t, mo_t, g_t, gcum_t, ml_gain, w_out[0].astype(BF16),
                   ffn_norm_g[0].reshape(1, D_MODEL).astype(F32),
                   w_gate[0].astype(BF16), w_up[0].astype(BF16), w_down[0].astype(BF16),
                   final_norm_g.reshape(1, D_MODEL).astype(F32), seq_len)
    return out.reshape(batch, seq_len, D_MODEL)
```

```python
import functools
import math

import jax
import jax.numpy as jnp
from jax import lax
from jax.experimental import pallas as pl
from jax.experimental.pallas import tpu as pltpu

F32 = jnp.float32
BF16 = jnp.bfloat16

D_MODEL = 1024
CHUNK = 64
NORM_EPS = 1e-6
DA_HEADS = 4
DA_QK_DIM = 64
DA_V_DIM = 128
DA_WIDTH = DA_HEADS * DA_V_DIM
ROPE_THETA = 500000.0
ROT_DIM = DA_QK_DIM // 4
ROT_HALF = ROT_DIM // 2
ML_HEADS = 4
ML_DIM = 128
ML_WIDTH = ML_HEADS * ML_DIM
CONV_WIDTH = 4
D_FF = 2816
LAM_INIT = 0.8 - 0.6 * math.exp(-0.3 * 0)
Q_SCALE = DA_QK_DIM ** -0.5 * math.log2(math.e)

OFF_DA_Q = 0
OFF_DA_K = 512
OFF_DA_V = 1024
OFF_ML = 1536
OFF_GATE = 3584
OFF_ML_V = OFF_ML + 2 * ML_WIDTH
ZT_ML_V = OFF_ML
ZT_ML_O = ZT_ML_V + ML_WIDTH
ZT_GATE = ZT_ML_O + ML_WIDTH

V7X_LANES = 128
V7X_SUBLANES = 8
V7X_BF16_ROWS_PER_VREG = 16
V7X_MXU_DEPTH = 256
V7X_VMEM_LIMIT_BYTES = 56 * 1024 * 1024

PROJ_ROWS = 1024
ATTN_Q = 1024
ATTN_K = 512
MASK_FEATS = ATTN_Q // CHUNK
MASK_BIG = 1e30
V_EXT_ROWS = DA_V_DIM + V7X_BF16_ROWS_PER_VREG
ML_CHUNK = 256
FFN_ROWS = 512
CONV_HALO = V7X_SUBLANES
CONV_COLS = 256
W_CAST_ROWS = 256

_NT = (((1,), (1,)), ((), ()))


def _compiler_params(n_axes):
    return pltpu.CompilerParams(
        dimension_semantics=("arbitrary",) * n_axes,
        vmem_limit_bytes=V7X_VMEM_LIMIT_BYTES,
    )


def _const_spec(shape):
    zeros = (0,) * len(shape)
    return pl.BlockSpec(shape, lambda *_: zeros, pipeline_mode=pl.Buffered(1))


def _rope_rows(zt, cos, sin):
    pieces = []
    for g in range(2):
        base = g * DA_QK_DIM
        x1 = zt[base:base + ROT_HALF]
        x2 = zt[base + ROT_HALF:base + ROT_DIM]
        pieces += [x1 * cos - x2 * sin, x2 * cos + x1 * sin, zt[base + ROT_DIM:base + DA_QK_DIM]]
    return jnp.concatenate(pieces, axis=0)


def _in_proj_kernel(tiles_per_seq,
                    x_ref, pos_ref, g_ref, w_in_t_ref, invf_ref,
                    convw_ref, convb_ref, gb_col_ref,
                    q_t_ref, k_ref, v_t_ref, mq_t_ref, mk_ref, mv_t_ref, mo_t_ref, g_t_ref, gcum_t_ref,
                    halo_ref, w_t_ref, w_row_ref):
    rows = x_ref.shape[0]

    @pl.when(pl.program_id(0) == 0)
    def _():
        for dst, src, height in ((OFF_DA_Q, OFF_DA_Q, OFF_ML), (ZT_ML_V, OFF_ML_V, 2 * ML_WIDTH)):
            for r in range(0, height, W_CAST_ROWS):
                w_t_ref[dst + r:dst + r + W_CAST_ROWS, :] = (
                    w_in_t_ref[src + r:src + r + W_CAST_ROWS, :].astype(BF16))
        gate_w_t = w_in_t_ref[OFF_GATE:OFF_GATE + 2 * ML_HEADS, :]
        w_t_ref[ZT_GATE:ZT_GATE + V7X_BF16_ROWS_PER_VREG, :] = jnp.concatenate(
            [gate_w_t, jnp.zeros((V7X_BF16_ROWS_PER_VREG - 2 * ML_HEADS, D_MODEL), F32)],
            axis=0).astype(BF16)
        for c in range(0, 2 * ML_WIDTH, W_CAST_ROWS):
            w_row_ref[:, c:c + W_CAST_ROWS] = (
                w_in_t_ref[OFF_ML + c:OFF_ML + c + W_CAST_ROWS, :].T.astype(BF16))

    x = x_ref[...]
    ms = jnp.mean(x * x, axis=-1, keepdims=True)
    hb = ((x * lax.rsqrt(ms + NORM_EPS)) * g_ref[...]).astype(BF16)

    first = (pl.program_id(0) % tiles_per_seq) == 0
    halo_ref[0:CONV_HALO, :] = jnp.where(first, 0.0, halo_ref[rows:rows + CONV_HALO, :])
    halo_ref[CONV_HALO:CONV_HALO + rows, :] = jnp.dot(hb, w_row_ref[...], preferred_element_type=F32)

    def proj_t(lo, hi):
        return lax.dot_general(w_t_ref[lo:hi, :], hb, _NT, preferred_element_type=F32)

    zqk_t = proj_t(OFF_DA_Q, OFF_DA_V)

    def conv_group(g):
        cols = slice(g * CONV_COLS, (g + 1) * CONV_COLS)
        xe = halo_ref[:, cols]
        conv = convw_ref[0:1, cols] * xe
        for j in range(1, CONV_WIDTH):
            conv = pltpu.roll(conv, 1, axis=0) + convw_ref[j:j + 1, cols] * xe
        conv = conv[CONV_HALO:, :] + convb_ref[:, cols]
        act = conv * (1.0 / (1.0 + jnp.exp(-conv)))
        if cols.stop <= ML_WIDTH:
            mq_t_ref[cols, :] = (act * (ML_DIM ** -0.5)).T.astype(BF16)
        else:
            mk_ref[:, cols.start - ML_WIDTH:cols.stop - ML_WIDTH] = act.astype(BF16)

    conv_groups = iter(range(2 * ML_WIDTH // CONV_COLS))
    conv_group(next(conv_groups))

    ang = invf_ref[...] * pos_ref[0].astype(F32)
    cos = jnp.cos(ang)
    sin = jnp.sin(ang)
    for h in range(DA_HEADS):
        lo = h * 2 * DA_QK_DIM
        hi = lo + 2 * DA_QK_DIM
        q_rot = _rope_rows(zqk_t[OFF_DA_Q + lo:OFF_DA_Q + hi], cos, sin)
        q_t_ref[lo:hi, :] = (q_rot * Q_SCALE).astype(BF16)
        k_rot = _rope_rows(zqk_t[OFF_DA_K + lo:OFF_DA_K + hi], cos, sin)
        k_ref[:, lo:hi] = k_rot.T.astype(BF16)
    pad_rows = V_EXT_ROWS - DA_V_DIM
    ones_row = (lax.broadcasted_iota(jnp.int32, (pad_rows, rows), 0) == 0).astype(BF16)
    for out_ref, base in ((v_t_ref, OFF_DA_V), (mv_t_ref, ZT_ML_V)):
        zv_t = proj_t(base, base + DA_WIDTH)
        conv_group(next(conv_groups))
        for h in range(DA_HEADS):
            v_lo = h * DA_V_DIM
            out_ref[h * V_EXT_ROWS:h * V_EXT_ROWS + DA_V_DIM, :] = zv_t[v_lo:v_lo + DA_V_DIM].astype(BF16)
            out_ref[h * V_EXT_ROWS + DA_V_DIM:(h + 1) * V_EXT_ROWS, :] = ones_row
    zo_t = proj_t(ZT_ML_O, ZT_GATE + V7X_BF16_ROWS_PER_VREG)
    conv_group(next(conv_groups))
    mo_t_ref[...] = zo_t[0:ML_WIDTH].astype(BF16)

    a_t = zo_t[ML_WIDTH:ML_WIDTH + 2 * ML_HEADS] + gb_col_ref[...]
    ls_t = jnp.minimum(a_t, 0.0) - jnp.log1p(jnp.exp(-jnp.abs(a_t)))
    row_id = lax.broadcasted_iota(jnp.int32, a_t.shape, 0)
    log_gates = jnp.where(row_id < ML_HEADS, a_t, ls_t)
    g_t_ref[...] = log_gates
    src = lax.broadcasted_iota(jnp.int32, (rows, rows), 0)
    dst = lax.broadcasted_iota(jnp.int32, (rows, rows), 1)
    tri = ((src <= dst) & (src // ML_CHUNK == dst // ML_CHUNK)).astype(F32)
    gcum_t_ref[...] = jnp.dot(log_gates, tri, preferred_element_type=F32,
                              precision=lax.Precision.HIGHEST)


def _in_proj(x2d, pos3d, norm_g, w_in_t, inv_freq, conv_w, conv_b, gb_col, seq_len):
    tokens = x2d.shape[0]
    rows = PROJ_ROWS
    n_tiles = tokens // rows
    row_blk = lambda width: pl.BlockSpec((rows, width), lambda i: (i, 0))
    col_blk = lambda height: pl.BlockSpec((height, rows), lambda i: (0, i))
    out_shape = (
        jax.ShapeDtypeStruct((DA_WIDTH, tokens), BF16),
        jax.ShapeDtypeStruct((tokens, DA_WIDTH), BF16),
        jax.ShapeDtypeStruct((DA_HEADS * V_EXT_ROWS, tokens), BF16),
        jax.ShapeDtypeStruct((ML_WIDTH, tokens), BF16),
        jax.ShapeDtypeStruct((tokens, ML_WIDTH), BF16),
        jax.ShapeDtypeStruct((ML_HEADS * V_EXT_ROWS, tokens), BF16),
        jax.ShapeDtypeStruct((ML_WIDTH, tokens), BF16),
        jax.ShapeDtypeStruct((2 * ML_HEADS, tokens), F32),
        jax.ShapeDtypeStruct((2 * ML_HEADS, tokens), F32),
    )
    return pl.pallas_call(
        functools.partial(_in_proj_kernel, seq_len // rows),
        grid=(n_tiles,),
        in_specs=[
            row_blk(D_MODEL),
            pl.BlockSpec((1, 1, rows), lambda i: (i, 0, 0)),
            _const_spec(norm_g.shape),
            _const_spec(w_in_t.shape),
            _const_spec(inv_freq.shape),
            _const_spec(conv_w.shape),
            _const_spec(conv_b.shape),
            _const_spec(gb_col.shape),
        ],
        out_specs=(
            col_blk(DA_WIDTH), row_blk(DA_WIDTH), col_blk(DA_HEADS * V_EXT_ROWS),
            col_blk(ML_WIDTH), row_blk(ML_WIDTH), col_blk(ML_HEADS * V_EXT_ROWS), col_blk(ML_WIDTH),
            col_blk(2 * ML_HEADS), col_blk(2 * ML_HEADS),
        ),
        out_shape=out_shape,
        scratch_shapes=[
            pltpu.VMEM((rows + CONV_HALO, 2 * ML_WIDTH), F32),
            pltpu.VMEM((ZT_GATE + V7X_BF16_ROWS_PER_VREG, D_MODEL), BF16),
            pltpu.VMEM((D_MODEL, 2 * ML_WIDTH), BF16),
        ],
        compiler_params=_compiler_params(1),
        name="in_proj",
    )(x2d, pos3d, norm_g, w_in_t, inv_freq, conv_w, conv_b, gb_col)


def _attn_kernel(lam_ref, q_t_ref, k_ref, v_t_ref, g_ref, w0_ref, w1_ref, w2_ref, w3_ref,
                 o_ref, w0_bf_ref, w1_bf_ref, w2_bf_ref, w3_bf_ref,
                 qm_ref, ind_ref, s_ref, p_ref, acc_ref):
    for w_ref, w_bf_ref in ((w0_ref, w0_bf_ref), (w1_ref, w1_bf_ref),
                            (w2_ref, w2_bf_ref), (w3_ref, w3_bf_ref)):
        w_bf_ref[...] = w_ref[...].astype(BF16)

    seq = k_ref.shape[0]
    tq, tk = ATTN_Q, ATTN_K
    n_q = seq // tq
    diag_tiles = tq // tk

    lv = lam_ref[...]
    lam = (jnp.exp(jnp.sum(lv[0:1] * lv[1:2], axis=1, keepdims=True))
           - jnp.exp(jnp.sum(lv[2:3] * lv[3:4], axis=1, keepdims=True)) + LAM_INIT)

    feat = lax.broadcasted_iota(jnp.int32, (MASK_FEATS, 2 * tq), 0)
    qchunk = (lax.broadcasted_iota(jnp.int32, (MASK_FEATS, 2 * tq), 1) % tq) // CHUNK
    qm_ref[...] = jnp.zeros(qm_ref.shape, BF16)
    qm_ref[2 * DA_QK_DIM:2 * DA_QK_DIM + MASK_FEATS, :] = jnp.where(
        feat > qchunk, -MASK_BIG, 0.0).astype(BF16)
    kchunk = lax.broadcasted_iota(jnp.int32, (tk, 2 * DA_QK_DIM), 0) // CHUNK
    lane = lax.broadcasted_iota(jnp.int32, (tk, 2 * DA_QK_DIM), 1)
    ind_ref[0] = jnp.zeros((tk, 2 * DA_QK_DIM), BF16)
    for d in range(diag_tiles):
        ind_ref[d + 1] = (lane == kchunk + d * (tk // CHUNK)).astype(BF16)

    def q_tile(i, first_tile):
        q_off = pl.multiple_of(i * tq, tq)
        t_diag = diag_tiles * i
        n_t = t_diag + diag_tiles

        def score(par, t, off=0, which=None):
            k_t = k_ref[pl.ds(pl.multiple_of(t * tk, tk), tk), :]
            if which is None:
                which = jnp.maximum(t - t_diag + 1, 0)
            k_ext = jnp.concatenate([k_t, ind_ref[which]], axis=1)
            if off:
                rhs = jnp.concatenate([qm_ref[:, off:tq], qm_ref[:, tq + off:2 * tq]], axis=1)
            else:
                rhs = qm_ref[...]
            w = tq - off
            s = jnp.dot(k_ext, rhs, preferred_element_type=F32)
            s_ref[par, :, 0:2 * w] = s
            return tuple(jnp.max(s[:, mi * w:(mi + 1) * w], axis=0, keepdims=True) for mi in range(2))

        def softmax(par, maxes, tile_max, off=0):
            w = tq - off
            new, alphas = [], []
            for mi in range(2):
                lanes = slice(mi * w, (mi + 1) * w)
                s = s_ref[par, :, lanes]
                m_old = maxes[mi][:, off:tq]
                m_new = jnp.maximum(m_old, tile_max[mi])
                p_ref[par, :, lanes] = jnp.exp2(s - m_new).astype(BF16)
                alphas.append(jnp.exp2(m_old - m_new))
                new.append(jnp.concatenate([maxes[mi][:, 0:off], m_new], axis=1) if off else m_new)
            return tuple(new), tuple(alphas)

        def accumulate(par, t, alphas, off=0):
            w = tq - off
            v_t = v_t_ref[:, pl.ds(pl.multiple_of(t * tk, tk), tk)]
            pv = jnp.dot(v_t, p_ref[par, :, 0:2 * w], preferred_element_type=F32)
            for mi in range(2):
                lanes = slice(mi * tq + off, (mi + 1) * tq)
                acc_ref[:, lanes] = alphas[mi] * acc_ref[:, lanes] + pv[:, mi * w:(mi + 1) * w]

        def step(par, t, carry, first=False):
            maxes, alphas, tile_max = carry
            next_max = score(1 - par, t + 1)
            if not first:
                accumulate(1 - par, t - 1, alphas)
            return softmax(par, maxes, tile_max) + (next_max,)

        q_t = q_t_ref[:, pl.ds(q_off, tq)]
        qm_ref[0:DA_QK_DIM, 0:tq] = q_t[0:DA_QK_DIM]
        qm_ref[DA_QK_DIM:2 * DA_QK_DIM, tq:2 * tq] = q_t[DA_QK_DIM:]
        tile_max = score(0, 0)
        if not first_tile:
            finish(i - 1)
        acc_ref[...] = jnp.zeros(acc_ref.shape, F32)

        neg = jnp.full((1, tq), -jnp.inf, F32)
        maxes, alphas = (neg, neg), None
        if not first_tile:
            carry = step(0, 0, (maxes, alphas, tile_max), first=True)

            def pair(u, carry):
                carry = step(1, 2 * u + 1, carry)
                return step(0, 2 * u + 2, carry)

            carry = lax.fori_loop(0, t_diag // 2 - 1, pair, carry)
            maxes, alphas, tile_max = step(1, t_diag - 1, carry)
        for d in range(diag_tiles):
            t, par = t_diag + d, d % 2
            if d + 1 < diag_tiles:
                next_max = score(1 - par, t + 1, off=(d + 1) * tk, which=d + 2)
            if d or not first_tile:
                accumulate(1 - par, t - 1, alphas, off=max(d - 1, 0) * tk)
            maxes, alphas = softmax(par, maxes, tile_max, off=d * tk)
            tile_max = next_max
        accumulate((diag_tiles - 1) % 2, n_t - 1, alphas, off=(diag_tiles - 1) * tk)
        return 0

    def finish(i):
        acc = acc_ref[...]
        o1 = acc[0:DA_V_DIM, 0:tq] / acc[DA_V_DIM:DA_V_DIM + 1, 0:tq]
        o2 = acc[0:DA_V_DIM, tq:2 * tq] / acc[DA_V_DIM:DA_V_DIM + 1, tq:2 * tq]
        o_t = o1 - lam * o2
        ms = jnp.mean(o_t * o_t, axis=0, keepdims=True)
        y_t = (o_t * lax.rsqrt(ms + NORM_EPS)) * g_ref[...] * (1.0 - LAM_INIT)
        o_ref[pl.ds(pl.multiple_of(i * tq, tq), tq), :] = y_t.T.astype(BF16)

    q_tile(jnp.int32(0), True)
    lax.fori_loop(1, n_q, lambda i, c: q_tile(i, False), 0)
    finish(jnp.int32(n_q - 1))


def _diff_attn(lam_params, q_t, k, v_t, subln_col, later_weights, batch, seq_len):
    tokens = k.shape[0]
    tq, tk = ATTN_Q, ATTN_K
    steps = batch * DA_HEADS
    for w in later_weights:
        assert w.shape[0] % (steps * V7X_BF16_ROWS_PER_VREG) == 0, w.shape
    slab_specs = [pl.BlockSpec((w.shape[0] // steps, w.shape[1]), lambda b, h: (b * DA_HEADS + h, 0))
                  for w in later_weights]
    return pl.pallas_call(
        _attn_kernel,
        grid=(batch, DA_HEADS),
        in_specs=[
            _const_spec(lam_params.shape),
            pl.BlockSpec((DA_V_DIM, seq_len), lambda b, h: (h, b)),
            pl.BlockSpec((seq_len, DA_V_DIM), lambda b, h: (b, h)),
            pl.BlockSpec((V_EXT_ROWS, seq_len), lambda b, h: (h, b)),
            _const_spec(subln_col.shape),
            *slab_specs,
        ],
        out_specs=(pl.BlockSpec((seq_len, DA_V_DIM), lambda b, h: (b, h)), *slab_specs),
        out_shape=(jax.ShapeDtypeStruct((tokens, DA_WIDTH), BF16),
                   *[jax.ShapeDtypeStruct(w.shape, BF16) for w in later_weights]),
        scratch_shapes=[
            pltpu.VMEM((V7X_MXU_DEPTH, 2 * tq), BF16),
            pltpu.VMEM((1 + tq // tk, tk, 2 * DA_QK_DIM), BF16),
            pltpu.VMEM((2, tk, 2 * tq), F32),
            pltpu.VMEM((2, tk, 2 * tq), BF16),
            pltpu.VMEM((V_EXT_ROWS, 2 * tq), F32),
        ],
        compiler_params=_compiler_params(2),
        name="diff_attn",
    )(lam_params, q_t, k, v_t, subln_col, *later_weights)


def _mlstm_begin(q_t_ref, k_ref, v_t_ref, g_t_ref, gcum_t_ref, c_ref, m_ref, cols, reset):
    chunk = cols.stop - cols.start
    heads = range(ML_HEADS)
    g_t = g_t_ref[:, cols]
    cum_row = gcum_t_ref[:, cols]
    key_rows = g_t - pltpu.roll(cum_row, ML_HEADS, axis=0)
    key_cols = jnp.concatenate(
        [key_rows, jnp.zeros((V7X_LANES - 2 * ML_HEADS, chunk), F32)], axis=0).T
    st = dict(
        chunk=chunk,
        q_t=[q_t_ref[h * ML_DIM:(h + 1) * ML_DIM, cols] for h in heads],
        k=[k_ref[cols, h * ML_DIM:(h + 1) * ML_DIM] for h in heads],
        v_t=[v_t_ref[h * V_EXT_ROWS:(h + 1) * V_EXT_ROWS, cols] for h in heads],
        c_t=[c_ref[h] for h in heads],
        m_prev=[m_ref[h] for h in heads],
        b_row=[cum_row[ML_HEADS + h:ML_HEADS + h + 1, :] for h in heads],
        i_row=[g_t[h:h + 1, :] for h in heads],
        e_mat=[jnp.broadcast_to(key_cols[:, h:h + 1], (chunk, chunk)) for h in heads],
    )
    if reset is not None:
        st["c_t"] = [jnp.where(reset, 0.0, c) for c in st["c_t"]]
        st["m_prev"] = [jnp.where(reset, 0.0, m) for m in st["m_prev"]]
    st["kq"] = [jnp.dot(st["k"][h], st["q_t"][h], preferred_element_type=F32) for h in heads]
    st["inter_mm"] = [jnp.dot(st["c_t"][h].astype(BF16), st["q_t"][h], preferred_element_type=F32)
                      for h in heads]
    return st


def _mlstm_weights(st):
    chunk = st["chunk"]
    s_id = lax.broadcasted_iota(jnp.int32, (chunk, chunk), 0)
    t_id = lax.broadcasted_iota(jnp.int32, (chunk, chunk), 1)
    causal = s_id <= t_id
    st["m_row"], st["w_inter"], st["sc"] = [], [], []
    for h in range(ML_HEADS):
        d_mat = jnp.where(causal, st["e_mat"][h] + st["b_row"][h], -jnp.inf)
        inter = st["b_row"][h] + st["m_prev"][h]
        m_row = jnp.maximum(inter, jnp.max(d_mat, axis=0, keepdims=True))
        st["m_row"].append(m_row)
        st["w_inter"].append(jnp.exp(inter - m_row))
        st["sc"].append((st["kq"][h] * jnp.exp(d_mat - m_row)).astype(BF16))


def _mlstm_matmuls(st, c_ref, m_ref):
    chunk = st["chunk"]
    heads = range(ML_HEADS)
    st["intra_mm"] = [jnp.dot(st["v_t"][h], st["sc"][h], preferred_element_type=F32) for h in heads]
    for h in heads:
        b_row, m_prev = st["b_row"][h], st["m_prev"][h]
        b_last = b_row[:, chunk - 1:chunk]
        g_row = (b_last - b_row) + st["i_row"][h]
        m_new = jnp.maximum(b_last + m_prev, jnp.max(g_row, axis=1, keepdims=True))
        decay = jnp.exp(b_last + m_prev - m_new)
        wk = jnp.exp(st["e_mat"][h][:, 0:ML_DIM] + (b_last - m_new))
        kw = (st["k"][h].astype(F32) * wk).astype(BF16)
        c_ref[h] = decay * st["c_t"][h] + jnp.dot(st["v_t"][h], kw, preferred_element_type=F32)
        m_ref[h] = m_new


def _mlstm_output(st, o_t_ref, ng_ref, cols):
    outs = []
    for h in range(ML_HEADS):
        num = st["intra_mm"][h] + st["w_inter"][h] * st["inter_mm"][h]
        nq = num[ML_DIM:ML_DIM + 1, :]
        hid = num[0:ML_DIM, :] / jnp.maximum(jnp.abs(nq), jnp.exp(-st["m_row"][h]))
        ms = jnp.mean(hid * hid, axis=0, keepdims=True)
        hn = (hid * lax.rsqrt(ms + NORM_EPS)) * ng_ref[h * ML_DIM:(h + 1) * ML_DIM, :]
        og = o_t_ref[h * ML_DIM:(h + 1) * ML_DIM, cols].astype(F32)
        outs.append((hn * (1.0 / (1.0 + jnp.exp(-og)))).T.astype(BF16))
    return jnp.concatenate(outs, axis=1)


def _rms(x, g):
    ms = jnp.mean(x * x, axis=-1, keepdims=True)
    return (x * lax.rsqrt(ms + NORM_EPS)) * g


def _out_ffn_kernel(tiles_per_seq, n_tiles,
                    x_ref, attn_ref, mq_t_ref, mk_ref, mv_t_ref, mo_t_ref, g_t_ref, gcum_t_ref, ng_ref,
                    w_out_ref, g_ffn_ref, w_gate_ref, w_up_ref, w_down_ref, g_final_ref,
                    out_ref, ml_ref, c_ref, m_ref):
    i = pl.program_id(0)
    reset = (jnp.minimum(i, n_tiles - 1) % tiles_per_seq) == 0
    chunks = [slice(c * ML_CHUNK, (c + 1) * ML_CHUNK) for c in range(FFN_ROWS // ML_CHUNK)]
    ml_args = (mq_t_ref, mk_ref, mv_t_ref, g_t_ref, gcum_t_ref, c_ref, m_ref)

    @pl.when(i == 0)
    def _():
        ml_ref[...] = jnp.zeros(ml_ref.shape, BF16)
        c_ref[...] = jnp.zeros(c_ref.shape, F32)
        m_ref[...] = jnp.zeros(m_ref.shape, F32)

    halves = chunks
    mix = jnp.concatenate([attn_ref[...], ml_ref[...]], axis=1)
    first = _mlstm_begin(*ml_args, chunks[0], reset)
    y = [x_ref[hs, :] + jnp.dot(mix[hs], w_out_ref[...], preferred_element_type=F32) for hs in halves]
    h2 = [_rms(yh, g_ffn_ref[...]).astype(BF16) for yh in y]
    _mlstm_weights(first)
    _mlstm_matmuls(first, c_ref, m_ref)
    gate = [jnp.dot(h, w_gate_ref[...], preferred_element_type=F32) for h in h2]
    ml_ref[chunks[0], :] = _mlstm_output(first, mo_t_ref, ng_ref, chunks[0])
    second = _mlstm_begin(*ml_args, chunks[1], None)
    up = [jnp.dot(h, w_up_ref[...], preferred_element_type=F32) for h in h2]
    _mlstm_weights(second)
    _mlstm_matmuls(second, c_ref, m_ref)
    act = [((g * (1.0 / (1.0 + jnp.exp(-g)))) * u).astype(BF16) for g, u in zip(gate, up)]
    y2 = [yh + jnp.dot(a, w_down_ref[...], preferred_element_type=F32) for yh, a in zip(y, act)]
    ml_ref[chunks[1], :] = _mlstm_output(second, mo_t_ref, ng_ref, chunks[1])
    for hs, yh in zip(halves, y2):
        out_ref[hs, :] = _rms(yh, g_final_ref[...])


def _out_ffn(x2d, attn, mq_t, mk, mv_t, mo_t, g_t, gcum_t, gain,
             w_out, g_ffn, w_gate, w_up, w_down, g_final, seq_len):
    tokens = x2d.shape[0]
    rows = FFN_ROWS
    n_tiles = tokens // rows
    ffn_blk = lambda width: pl.BlockSpec((rows, width), lambda i: (jnp.maximum(i - 1, 0), 0))
    ml_row = lambda width: pl.BlockSpec((rows, width), lambda i: (jnp.minimum(i, n_tiles - 1), 0))
    ml_col = lambda height: pl.BlockSpec((height, rows), lambda i: (0, jnp.minimum(i, n_tiles - 1)))
    return pl.pallas_call(
        functools.partial(_out_ffn_kernel, seq_len // rows, n_tiles),
        grid=(n_tiles + 1,),
        in_specs=[
            ffn_blk(D_MODEL), ffn_blk(DA_WIDTH),
            ml_col(ML_WIDTH), ml_row(ML_WIDTH), ml_col(ML_HEADS * V_EXT_ROWS), ml_col(ML_WIDTH),
            ml_col(2 * ML_HEADS), ml_col(2 * ML_HEADS),
            _const_spec(gain.shape),
            _const_spec(w_out.shape), _const_spec(g_ffn.shape),
            _const_spec(w_gate.shape), _const_spec(w_up.shape), _const_spec(w_down.shape),
            _const_spec(g_final.shape),
        ],
        out_specs=ffn_blk(D_MODEL),
        out_shape=jax.ShapeDtypeStruct((tokens, D_MODEL), F32),
        scratch_shapes=[pltpu.VMEM((rows, ML_WIDTH), BF16),
                        pltpu.VMEM((ML_HEADS, V_EXT_ROWS, ML_DIM), F32),
                        pltpu.VMEM((ML_HEADS, 1, 1), F32)],
        compiler_params=_compiler_params(1),
        name="out_ffn",
    )(x2d, attn, mq_t, mk, mv_t, mo_t, g_t, gcum_t, gain,
      w_out, g_ffn, w_gate, w_up, w_down, g_final)


def kernel(x, positions, mix_norm_g, w_in, da_lambda, da_subln_g, ml_conv_w, ml_conv_b, ml_gate_b,
           ml_norm_g, w_out, ffn_norm_g, w_gate, w_up, w_down, final_norm_g):
    batch, seq_len, _ = x.shape
    tokens = batch * seq_len
    depth = w_in.shape[0]
    assert depth == 1, "one trunk layer"
    assert seq_len % PROJ_ROWS == 0 and seq_len % ATTN_Q == 0 and seq_len % ML_CHUNK == 0
    assert ATTN_Q % ATTN_K == 0 and ATTN_K % CHUNK == 0 and tokens % FFN_ROWS == 0
    assert 2 * DA_QK_DIM + MASK_FEATS <= V7X_MXU_DEPTH and MASK_FEATS <= 2 * DA_QK_DIM
    assert DA_V_DIM == ML_DIM and DA_HEADS == ML_HEADS and FFN_ROWS == 2 * ML_CHUNK
    assert PROJ_ROWS % ML_CHUNK == 0 and w_gate.shape[1:] == (D_MODEL, D_FF)

    x2d = x.reshape(tokens, D_MODEL)
    pos3d = positions.reshape(tokens // PROJ_ROWS, 1, PROJ_ROWS)

    w_in_t = w_in[0].T
    inv_freq = (ROPE_THETA ** (-jnp.arange(0, ROT_DIM, 2, dtype=F32) / ROT_DIM)).reshape(ROT_HALF, 1)
    gb_col = ml_gate_b[0].astype(F32).reshape(2 * ML_HEADS, 1)

    q_t, k, v_t, mq_t, mk, mv_t, mo_t, g_t, gcum_t = _in_proj(
        x2d, pos3d, mix_norm_g[0].reshape(1, D_MODEL).astype(F32), w_in_t, inv_freq,
        ml_conv_w[0].astype(F32), ml_conv_b[0].reshape(1, 2 * ML_WIDTH).astype(F32),
        gb_col, seq_len)

    attn, w_out_bf, w_gate_bf, w_up_bf, w_down_bf = _diff_attn(
        da_lambda[0].astype(F32), q_t, k, v_t, da_subln_g[0].astype(F32).reshape(DA_V_DIM, 1),
        (w_out[0], w_gate[0], w_up[0], w_down[0]), batch, seq_len)
    ml_gain = jnp.broadcast_to(ml_norm_g[0].astype(F32).reshape(ML_WIDTH, 1), (ML_WIDTH, ML_CHUNK))

    out = _out_ffn(x2d, attn, mq_t, mk, mv_t, mo_t, g_t, gcum_t, ml_gain, w_out_bf,
                   ffn_norm_g[0].reshape(1, D_MODEL).astype(F32), w_gate_bf, w_up_bf, w_down_bf,
                   final_norm_g.reshape(1, D_MODEL).astype(F32), seq_len)
    return out.reshape(batch, seq_len, D_MODEL)
```

```python
import functools
import math

import jax
import jax.numpy as jnp
from jax import lax
from jax.experimental import pallas as pl
from jax.experimental.pallas import tpu as pltpu

F32 = jnp.float32
BF16 = jnp.bfloat16

D_MODEL = 1024
CHUNK = 64
NORM_EPS = 1e-6
DA_HEADS = 4
DA_QK_DIM = 64
DA_V_DIM = 128
DA_WIDTH = DA_HEADS * DA_V_DIM
ROPE_THETA = 500000.0
ROT_DIM = DA_QK_DIM // 4
ROT_HALF = ROT_DIM // 2
ML_HEADS = 4
ML_DIM = 128
ML_WIDTH = ML_HEADS * ML_DIM
CONV_WIDTH = 4
D_FF = 2816
LAM_INIT = 0.8 - 0.6 * math.exp(-0.3 * 0)
Q_SCALE = DA_QK_DIM ** -0.5 * math.log2(math.e)

OFF_DA_Q = 0
OFF_DA_K = 512
OFF_DA_V = 1024
OFF_ML = 1536
OFF_GATE = 3584
OFF_ML_V = OFF_ML + 2 * ML_WIDTH
ZT_ML_V = OFF_ML
ZT_ML_O = ZT_ML_V + ML_WIDTH
ZT_GATE = ZT_ML_O + ML_WIDTH

V7X_LANES = 128
V7X_SUBLANES = 8
V7X_BF16_ROWS_PER_VREG = 16
V7X_MXU_DEPTH = 256
V7X_VMEM_LIMIT_BYTES = 56 * 1024 * 1024

PROJ_ROWS = 1024
ATTN_Q = 1024
ATTN_K = 512
MASK_FEATS = ATTN_Q // CHUNK
MASK_BIG = 1e30
V_EXT_ROWS = DA_V_DIM + V7X_BF16_ROWS_PER_VREG
ML_CHUNK = 256
FFN_ROWS = 512
CONV_HALO = V7X_SUBLANES
CONV_COLS = 256
W_CAST_ROWS = 256

_NT = (((1,), (1,)), ((), ()))


def _compiler_params(n_axes):
    return pltpu.CompilerParams(
        dimension_semantics=("arbitrary",) * n_axes,
        vmem_limit_bytes=V7X_VMEM_LIMIT_BYTES,
    )


def _const_spec(shape):
    zeros = (0,) * len(shape)
    return pl.BlockSpec(shape, lambda *_: zeros, pipeline_mode=pl.Buffered(1))


def _rope_rows(zt, cos, sin):
    pieces = []
    for g in range(2):
        base = g * DA_QK_DIM
        x1 = zt[base:base + ROT_HALF]
        x2 = zt[base + ROT_HALF:base + ROT_DIM]
        pieces += [x1 * cos - x2 * sin, x2 * cos + x1 * sin, zt[base + ROT_DIM:base + DA_QK_DIM]]
    return jnp.concatenate(pieces, axis=0)


def _in_proj_kernel(tiles_per_seq,
                    x_ref, pos_ref, g_ref, w_in_t_ref, invf_ref,
                    convw_ref, convb_ref, gb_col_ref,
                    q_t_ref, k_ref, v_t_ref, mq_t_ref, mk_ref, mv_t_ref, mo_t_ref, g_t_ref, gcum_t_ref,
                    halo_ref, w_t_ref, w_row_ref):
    rows = x_ref.shape[0]

    @pl.when(pl.program_id(0) == 0)
    def _():
        for dst, src, height in ((OFF_DA_Q, OFF_DA_Q, OFF_ML), (ZT_ML_V, OFF_ML_V, 2 * ML_WIDTH)):
            for r in range(0, height, W_CAST_ROWS):
                w_t_ref[dst + r:dst + r + W_CAST_ROWS, :] = (
                    w_in_t_ref[src + r:src + r + W_CAST_ROWS, :].astype(BF16))
        gate_w_t = w_in_t_ref[OFF_GATE:OFF_GATE + 2 * ML_HEADS, :]
        w_t_ref[ZT_GATE:ZT_GATE + V7X_BF16_ROWS_PER_VREG, :] = jnp.concatenate(
            [gate_w_t, jnp.zeros((V7X_BF16_ROWS_PER_VREG - 2 * ML_HEADS, D_MODEL), F32)],
            axis=0).astype(BF16)
        for c in range(0, 2 * ML_WIDTH, W_CAST_ROWS):
            w_row_ref[:, c:c + W_CAST_ROWS] = (
                w_in_t_ref[OFF_ML + c:OFF_ML + c + W_CAST_ROWS, :].T.astype(BF16))

    x = x_ref[...]
    ms = jnp.mean(x * x, axis=-1, keepdims=True)
    hb = ((x * lax.rsqrt(ms + NORM_EPS)) * g_ref[...]).astype(BF16)

    first = (pl.program_id(0) % tiles_per_seq) == 0
    halo_ref[0:CONV_HALO, :] = jnp.where(first, 0.0, halo_ref[rows:rows + CONV_HALO, :])
    halo_ref[CONV_HALO:CONV_HALO + rows, :] = jnp.dot(hb, w_row_ref[...], preferred_element_type=F32)

    def proj_t(lo, hi):
        return lax.dot_general(w_t_ref[lo:hi, :], hb, _NT, preferred_element_type=F32)

    zqk_t = proj_t(OFF_DA_Q, OFF_DA_V)

    def conv_group(g):
        cols = slice(g * CONV_COLS, (g + 1) * CONV_COLS)
        xe = halo_ref[:, cols]
        conv = convw_ref[0:1, cols] * xe
        for j in range(1, CONV_WIDTH):
            conv = pltpu.roll(conv, 1, axis=0) + convw_ref[j:j + 1, cols] * xe
        conv = conv[CONV_HALO:, :] + convb_ref[:, cols]
        act = conv * (1.0 / (1.0 + jnp.exp(-conv)))
        if cols.stop <= ML_WIDTH:
            mq_t_ref[cols, :] = (act * (ML_DIM ** -0.5)).T.astype(BF16)
        else:
            mk_ref[:, cols.start - ML_WIDTH:cols.stop - ML_WIDTH] = act.astype(BF16)

    conv_groups = iter(range(2 * ML_WIDTH // CONV_COLS))
    conv_group(next(conv_groups))

    ang = invf_ref[...] * pos_ref[0].astype(F32)
    cos = jnp.cos(ang)
    sin = jnp.sin(ang)
    for h in range(DA_HEADS):
        lo = h * 2 * DA_QK_DIM
        hi = lo + 2 * DA_QK_DIM
        q_rot = _rope_rows(zqk_t[OFF_DA_Q + lo:OFF_DA_Q + hi], cos, sin)
        q_t_ref[lo:hi, :] = (q_rot * Q_SCALE).astype(BF16)
        k_rot = _rope_rows(zqk_t[OFF_DA_K + lo:OFF_DA_K + hi], cos, sin)
        k_ref[:, lo:hi] = k_rot.T.astype(BF16)
    pad_rows = V_EXT_ROWS - DA_V_DIM
    ones_row = (lax.broadcasted_iota(jnp.int32, (pad_rows, rows), 0) == 0).astype(BF16)
    for out_ref, base in ((v_t_ref, OFF_DA_V), (mv_t_ref, ZT_ML_V)):
        zv_t = proj_t(base, base + DA_WIDTH)
        conv_group(next(conv_groups))
        for h in range(DA_HEADS):
            v_lo = h * DA_V_DIM
            out_ref[h * V_EXT_ROWS:h * V_EXT_ROWS + DA_V_DIM, :] = zv_t[v_lo:v_lo + DA_V_DIM].astype(BF16)
            out_ref[h * V_EXT_ROWS + DA_V_DIM:(h + 1) * V_EXT_ROWS, :] = ones_row
    zo_t = proj_t(ZT_ML_O, ZT_GATE + V7X_BF16_ROWS_PER_VREG)
    conv_group(next(conv_groups))
    mo_t_ref[...] = zo_t[0:ML_WIDTH].astype(BF16)

    a_t = zo_t[ML_WIDTH:ML_WIDTH + 2 * ML_HEADS] + gb_col_ref[...]
    ls_t = jnp.minimum(a_t, 0.0) - jnp.log1p(jnp.exp(-jnp.abs(a_t)))
    row_id = lax.broadcasted_iota(jnp.int32, a_t.shape, 0)
    log_gates = jnp.where(row_id < ML_HEADS, a_t, ls_t)
    g_t_ref[...] = log_gates
    src = lax.broadcasted_iota(jnp.int32, (ML_CHUNK, ML_CHUNK), 0)
    dst = lax.broadcasted_iota(jnp.int32, (ML_CHUNK, ML_CHUNK), 1)
    tri = (src <= dst).astype(F32)
    for c in range(0, rows, ML_CHUNK):
        gcum_t_ref[:, c:c + ML_CHUNK] = jnp.dot(log_gates[:, c:c + ML_CHUNK], tri,
                                                preferred_element_type=F32,
                                                precision=lax.Precision.HIGHEST)


def _in_proj(x2d, pos3d, norm_g, w_in_t, inv_freq, conv_w, conv_b, gb_col, seq_len):
    tokens = x2d.shape[0]
    rows = PROJ_ROWS
    n_tiles = tokens // rows
    row_blk = lambda width: pl.BlockSpec((rows, width), lambda i: (i, 0))
    col_blk = lambda height: pl.BlockSpec((height, rows), lambda i: (0, i))
    out_shape = (
        jax.ShapeDtypeStruct((DA_WIDTH, tokens), BF16),
        jax.ShapeDtypeStruct((tokens, DA_WIDTH), BF16),
        jax.ShapeDtypeStruct((DA_HEADS * V_EXT_ROWS, tokens), BF16),
        jax.ShapeDtypeStruct((ML_WIDTH, tokens), BF16),
        jax.ShapeDtypeStruct((tokens, ML_WIDTH), BF16),
        jax.ShapeDtypeStruct((ML_HEADS * V_EXT_ROWS, tokens), BF16),
        jax.ShapeDtypeStruct((ML_WIDTH, tokens), BF16),
        jax.ShapeDtypeStruct((2 * ML_HEADS, tokens), F32),
        jax.ShapeDtypeStruct((2 * ML_HEADS, tokens), F32),
    )
    return pl.pallas_call(
        functools.partial(_in_proj_kernel, seq_len // rows),
        grid=(n_tiles,),
        in_specs=[
            row_blk(D_MODEL),
            pl.BlockSpec((1, 1, rows), lambda i: (i, 0, 0)),
            _const_spec(norm_g.shape),
            _const_spec(w_in_t.shape),
            _const_spec(inv_freq.shape),
            _const_spec(conv_w.shape),
            _const_spec(conv_b.shape),
            _const_spec(gb_col.shape),
        ],
        out_specs=(
            col_blk(DA_WIDTH), row_blk(DA_WIDTH), col_blk(DA_HEADS * V_EXT_ROWS),
            col_blk(ML_WIDTH), row_blk(ML_WIDTH), col_blk(ML_HEADS * V_EXT_ROWS), col_blk(ML_WIDTH),
            col_blk(2 * ML_HEADS), col_blk(2 * ML_HEADS),
        ),
        out_shape=out_shape,
        scratch_shapes=[
            pltpu.VMEM((rows + CONV_HALO, 2 * ML_WIDTH), F32),
            pltpu.VMEM((ZT_GATE + V7X_BF16_ROWS_PER_VREG, D_MODEL), BF16),
            pltpu.VMEM((D_MODEL, 2 * ML_WIDTH), BF16),
        ],
        compiler_params=_compiler_params(1),
        name="in_proj",
    )(x2d, pos3d, norm_g, w_in_t, inv_freq, conv_w, conv_b, gb_col)


def _attn_kernel(lam_ref, q_t_ref, k_ref, v_t_ref, g_ref, w0_ref, w1_ref, w2_ref, w3_ref,
                 o_ref, w0_bf_ref, w1_bf_ref, w2_bf_ref, w3_bf_ref,
                 qm_ref, ind_ref, s_ref, p_ref, acc_ref):
    for w_ref, w_bf_ref in ((w0_ref, w0_bf_ref), (w1_ref, w1_bf_ref),
                            (w2_ref, w2_bf_ref), (w3_ref, w3_bf_ref)):
        w_bf_ref[...] = w_ref[...].astype(BF16)

    seq = k_ref.shape[0]
    tq, tk = ATTN_Q, ATTN_K
    n_q = seq // tq
    diag_tiles = tq // tk

    lv = lam_ref[...]
    lam = (jnp.exp(jnp.sum(lv[0:1] * lv[1:2], axis=1, keepdims=True))
           - jnp.exp(jnp.sum(lv[2:3] * lv[3:4], axis=1, keepdims=True)) + LAM_INIT)

    feat = lax.broadcasted_iota(jnp.int32, (MASK_FEATS, 2 * tq), 0)
    qchunk = (lax.broadcasted_iota(jnp.int32, (MASK_FEATS, 2 * tq), 1) % tq) // CHUNK
    qm_ref[...] = jnp.zeros(qm_ref.shape, BF16)
    qm_ref[2 * DA_QK_DIM:2 * DA_QK_DIM + MASK_FEATS, :] = jnp.where(
        feat > qchunk, -MASK_BIG, 0.0).astype(BF16)
    kchunk = lax.broadcasted_iota(jnp.int32, (tk, 2 * DA_QK_DIM), 0) // CHUNK
    lane = lax.broadcasted_iota(jnp.int32, (tk, 2 * DA_QK_DIM), 1)
    ind_ref[0] = jnp.zeros((tk, 2 * DA_QK_DIM), BF16)
    for d in range(diag_tiles):
        ind_ref[d + 1] = (lane == kchunk + d * (tk // CHUNK)).astype(BF16)

    def q_tile(i, first_tile):
        q_off = pl.multiple_of(i * tq, tq)
        t_diag = diag_tiles * i
        n_t = t_diag + diag_tiles

        def score(par, t, off=0, which=None):
            k_t = k_ref[pl.ds(pl.multiple_of(t * tk, tk), tk), :]
            if which is None:
                which = jnp.maximum(t - t_diag + 1, 0)
            k_ext = jnp.concatenate([k_t, ind_ref[which]], axis=1)
            if off:
                rhs = jnp.concatenate([qm_ref[:, off:tq], qm_ref[:, tq + off:2 * tq]], axis=1)
            else:
                rhs = qm_ref[...]
            w = tq - off
            s = jnp.dot(k_ext, rhs, preferred_element_type=F32)
            s_ref[par, :, 0:2 * w] = s
            return tuple(jnp.max(s[:, mi * w:(mi + 1) * w], axis=0, keepdims=True) for mi in range(2))

        def softmax(par, maxes, tile_max, off=0):
            w = tq - off
            new, alphas = [], []
            for mi in range(2):
                lanes = slice(mi * w, (mi + 1) * w)
                s = s_ref[par, :, lanes]
                m_old = maxes[mi][:, off:tq]
                m_new = jnp.maximum(m_old, tile_max[mi])
                p_ref[par, :, lanes] = jnp.exp2(s - m_new).astype(BF16)
                alphas.append(jnp.exp2(m_old - m_new))
                new.append(jnp.concatenate([maxes[mi][:, 0:off], m_new], axis=1) if off else m_new)
            return tuple(new), tuple(alphas)

        def accumulate(par, t, alphas, off=0):
            w = tq - off
            v_t = v_t_ref[:, pl.ds(pl.multiple_of(t * tk, tk), tk)]
            pv = jnp.dot(v_t, p_ref[par, :, 0:2 * w], preferred_element_type=F32)
            for mi in range(2):
                lanes = slice(mi * tq + off, (mi + 1) * tq)
                acc_ref[:, lanes] = alphas[mi] * acc_ref[:, lanes] + pv[:, mi * w:(mi + 1) * w]

        def step(par, t, carry, first=False):
            maxes, alphas, tile_max = carry
            next_max = score(1 - par, t + 1)
            if not first:
                accumulate(1 - par, t - 1, alphas)
            return softmax(par, maxes, tile_max) + (next_max,)

        q_t = q_t_ref[:, pl.ds(q_off, tq)]
        qm_ref[0:DA_QK_DIM, 0:tq] = q_t[0:DA_QK_DIM]
        qm_ref[DA_QK_DIM:2 * DA_QK_DIM, tq:2 * tq] = q_t[DA_QK_DIM:]
        tile_max = score(0, 0)
        if not first_tile:
            finish(i - 1)
        acc_ref[...] = jnp.zeros(acc_ref.shape, F32)

        neg = jnp.full((1, tq), -jnp.inf, F32)
        maxes, alphas = (neg, neg), None
        if not first_tile:
            carry = step(0, 0, (maxes, alphas, tile_max), first=True)

            def pair(u, carry):
                carry = step(1, 2 * u + 1, carry)
                return step(0, 2 * u + 2, carry)

            carry = lax.fori_loop(0, t_diag // 2 - 1, pair, carry)
            maxes, alphas, tile_max = step(1, t_diag - 1, carry)
        for d in range(diag_tiles):
            t, par = t_diag + d, d % 2
            if d + 1 < diag_tiles:
                next_max = score(1 - par, t + 1, off=(d + 1) * tk, which=d + 2)
            if d or not first_tile:
                accumulate(1 - par, t - 1, alphas, off=max(d - 1, 0) * tk)
            maxes, alphas = softmax(par, maxes, tile_max, off=d * tk)
            tile_max = next_max
        accumulate((diag_tiles - 1) % 2, n_t - 1, alphas, off=(diag_tiles - 1) * tk)
        return 0

    def finish(i):
        acc = acc_ref[...]
        o1 = acc[0:DA_V_DIM, 0:tq] / acc[DA_V_DIM:DA_V_DIM + 1, 0:tq]
        o2 = acc[0:DA_V_DIM, tq:2 * tq] / acc[DA_V_DIM:DA_V_DIM + 1, tq:2 * tq]
        o_t = o1 - lam * o2
        ms = jnp.mean(o_t * o_t, axis=0, keepdims=True)
        y_t = (o_t * lax.rsqrt(ms + NORM_EPS)) * g_ref[...] * (1.0 - LAM_INIT)
        o_ref[pl.ds(pl.multiple_of(i * tq, tq), tq), :] = y_t.T.astype(BF16)

    q_tile(jnp.int32(0), True)
    lax.fori_loop(1, n_q, lambda i, c: q_tile(i, False), 0)
    finish(jnp.int32(n_q - 1))


def _diff_attn(lam_params, q_t, k, v_t, subln_col, later_weights, batch, seq_len):
    tokens = k.shape[0]
    tq, tk = ATTN_Q, ATTN_K
    steps = batch * DA_HEADS
    for w in later_weights:
        assert w.shape[0] % (steps * V7X_BF16_ROWS_PER_VREG) == 0, w.shape
    slab_specs = [pl.BlockSpec((w.shape[0] // steps, w.shape[1]), lambda b, h: (b * DA_HEADS + h, 0))
                  for w in later_weights]
    return pl.pallas_call(
        _attn_kernel,
        grid=(batch, DA_HEADS),
        in_specs=[
            _const_spec(lam_params.shape),
            pl.BlockSpec((DA_V_DIM, seq_len), lambda b, h: (h, b)),
            pl.BlockSpec((seq_len, DA_V_DIM), lambda b, h: (b, h)),
            pl.BlockSpec((V_EXT_ROWS, seq_len), lambda b, h: (h, b)),
            _const_spec(subln_col.shape),
            *slab_specs,
        ],
        out_specs=(pl.BlockSpec((seq_len, DA_V_DIM), lambda b, h: (b, h)), *slab_specs),
        out_shape=(jax.ShapeDtypeStruct((tokens, DA_WIDTH), BF16),
                   *[jax.ShapeDtypeStruct(w.shape, BF16) for w in later_weights]),
        scratch_shapes=[
            pltpu.VMEM((V7X_MXU_DEPTH, 2 * tq), BF16),
            pltpu.VMEM((1 + tq // tk, tk, 2 * DA_QK_DIM), BF16),
            pltpu.VMEM((2, tk, 2 * tq), F32),
            pltpu.VMEM((2, tk, 2 * tq), BF16),
            pltpu.VMEM((V_EXT_ROWS, 2 * tq), F32),
        ],
        compiler_params=_compiler_params(2),
        name="diff_attn",
    )(lam_params, q_t, k, v_t, subln_col, *later_weights)


def _mlstm_begin(q_t_ref, k_ref, v_t_ref, g_t_ref, gcum_t_ref, c_ref, m_ref, cols, reset):
    chunk = cols.stop - cols.start
    heads = range(ML_HEADS)
    g_t = g_t_ref[:, cols]
    cum_row = gcum_t_ref[:, cols]
    key_rows = g_t - pltpu.roll(cum_row, ML_HEADS, axis=0)
    key_cols = jnp.concatenate(
        [key_rows, jnp.zeros((V7X_LANES - 2 * ML_HEADS, chunk), F32)], axis=0).T
    st = dict(
        chunk=chunk,
        q_t=[q_t_ref[h * ML_DIM:(h + 1) * ML_DIM, cols] for h in heads],
        k=[k_ref[cols, h * ML_DIM:(h + 1) * ML_DIM] for h in heads],
        v_t=[v_t_ref[h * V_EXT_ROWS:(h + 1) * V_EXT_ROWS, cols] for h in heads],
        c_t=[c_ref[h] for h in heads],
        m_prev=[m_ref[h] for h in heads],
        b_row=[cum_row[ML_HEADS + h:ML_HEADS + h + 1, :] for h in heads],
        i_row=[g_t[h:h + 1, :] for h in heads],
        e_mat=[jnp.broadcast_to(key_cols[:, h:h + 1], (chunk, chunk)) for h in heads],
    )
    if reset is not None:
        st["c_t"] = [jnp.where(reset, 0.0, c) for c in st["c_t"]]
        st["m_prev"] = [jnp.where(reset, 0.0, m) for m in st["m_prev"]]
    st["kq"] = [jnp.dot(st["k"][h], st["q_t"][h], preferred_element_type=F32) for h in heads]
    st["inter_mm"] = [jnp.dot(st["c_t"][h].astype(BF16), st["q_t"][h], preferred_element_type=F32)
                      for h in heads]
    return st


def _mlstm_weights(st):
    chunk = st["chunk"]
    s_id = lax.broadcasted_iota(jnp.int32, (chunk, chunk), 0)
    t_id = lax.broadcasted_iota(jnp.int32, (chunk, chunk), 1)
    causal = s_id <= t_id
    st["m_row"], st["w_inter"], st["sc"] = [], [], []
    for h in range(ML_HEADS):
        d_mat = jnp.where(causal, st["e_mat"][h] + st["b_row"][h], -jnp.inf)
        inter = st["b_row"][h] + st["m_prev"][h]
        m_row = jnp.maximum(inter, jnp.max(d_mat, axis=0, keepdims=True))
        st["m_row"].append(m_row)
        st["w_inter"].append(jnp.exp(inter - m_row))
        st["sc"].append((st["kq"][h] * jnp.exp(d_mat - m_row)).astype(BF16))


def _mlstm_matmuls(st, c_ref, m_ref):
    chunk = st["chunk"]
    heads = range(ML_HEADS)
    st["intra_mm"] = [jnp.dot(st["v_t"][h], st["sc"][h], preferred_element_type=F32) for h in heads]
    for h in heads:
        b_row, m_prev = st["b_row"][h], st["m_prev"][h]
        b_last = b_row[:, chunk - 1:chunk]
        g_row = (b_last - b_row) + st["i_row"][h]
        m_new = jnp.maximum(b_last + m_prev, jnp.max(g_row, axis=1, keepdims=True))
        decay = jnp.exp(b_last + m_prev - m_new)
        wk = jnp.exp(st["e_mat"][h][:, 0:ML_DIM] + (b_last - m_new))
        kw = (st["k"][h].astype(F32) * wk).astype(BF16)
        c_ref[h] = decay * st["c_t"][h] + jnp.dot(st["v_t"][h], kw, preferred_element_type=F32)
        m_ref[h] = m_new


def _mlstm_output(st, o_t_ref, ng_ref, cols):
    outs = []
    for h in range(ML_HEADS):
        num = st["intra_mm"][h] + st["w_inter"][h] * st["inter_mm"][h]
        nq = num[ML_DIM:ML_DIM + 1, :]
        hid = num[0:ML_DIM, :] / jnp.maximum(jnp.abs(nq), jnp.exp(-st["m_row"][h]))
        ms = jnp.mean(hid * hid, axis=0, keepdims=True)
        hn = (hid * lax.rsqrt(ms + NORM_EPS)) * ng_ref[h * ML_DIM:(h + 1) * ML_DIM, :]
        og = o_t_ref[h * ML_DIM:(h + 1) * ML_DIM, cols].astype(F32)
        outs.append((hn * (1.0 / (1.0 + jnp.exp(-og)))).T.astype(BF16))
    return jnp.concatenate(outs, axis=1)


def _rms(x, g):
    ms = jnp.mean(x * x, axis=-1, keepdims=True)
    return (x * lax.rsqrt(ms + NORM_EPS)) * g


def _out_ffn_kernel(tiles_per_seq, n_tiles,
                    x_ref, attn_ref, mq_t_ref, mk_ref, mv_t_ref, mo_t_ref, g_t_ref, gcum_t_ref, ng_ref,
                    w_out_ref, g_ffn_ref, w_gate_ref, w_up_ref, w_down_ref, g_final_ref,
                    out_ref, ml_ref, c_ref, m_ref):
    i = pl.program_id(0)
    reset = (jnp.minimum(i, n_tiles - 1) % tiles_per_seq) == 0
    chunks = [slice(c * ML_CHUNK, (c + 1) * ML_CHUNK) for c in range(FFN_ROWS // ML_CHUNK)]
    ml_args = (mq_t_ref, mk_ref, mv_t_ref, g_t_ref, gcum_t_ref, c_ref, m_ref)

    @pl.when(i == 0)
    def _():
        ml_ref[...] = jnp.zeros(ml_ref.shape, BF16)
        c_ref[...] = jnp.zeros(c_ref.shape, F32)
        m_ref[...] = jnp.zeros(m_ref.shape, F32)

    halves = chunks
    mix = jnp.concatenate([attn_ref[...], ml_ref[...]], axis=1)
    first = _mlstm_begin(*ml_args, chunks[0], reset)
    y = [x_ref[hs, :] + jnp.dot(mix[hs], w_out_ref[...], preferred_element_type=F32) for hs in halves]
    h2 = [_rms(yh, g_ffn_ref[...]).astype(BF16) for yh in y]
    _mlstm_weights(first)
    _mlstm_matmuls(first, c_ref, m_ref)
    gate = [jnp.dot(h, w_gate_ref[...], preferred_element_type=F32) for h in h2]
    ml_ref[chunks[0], :] = _mlstm_output(first, mo_t_ref, ng_ref, chunks[0])
    second = _mlstm_begin(*ml_args, chunks[1], None)
    up = [jnp.dot(h, w_up_ref[...], preferred_element_type=F32) for h in h2]
    _mlstm_weights(second)
    _mlstm_matmuls(second, c_ref, m_ref)
    act = [((g * (1.0 / (1.0 + jnp.exp(-g)))) * u).astype(BF16) for g, u in zip(gate, up)]
    y2 = [yh + jnp.dot(a, w_down_ref[...], preferred_element_type=F32) for yh, a in zip(y, act)]
    ml_ref[chunks[1], :] = _mlstm_output(second, mo_t_ref, ng_ref, chunks[1])
    for hs, yh in zip(halves, y2):
        out_ref[hs, :] = _rms(yh, g_final_ref[...])


def _out_ffn(x2d, attn, mq_t, mk, mv_t, mo_t, g_t, gcum_t, gain,
             w_out, g_ffn, w_gate, w_up, w_down, g_final, seq_len):
    tokens = x2d.shape[0]
    rows = FFN_ROWS
    n_tiles = tokens // rows
    ffn_blk = lambda width: pl.BlockSpec((rows, width), lambda i: (jnp.maximum(i - 1, 0), 0))
    ml_row = lambda width: pl.BlockSpec((rows, width), lambda i: (jnp.minimum(i, n_tiles - 1), 0))
    ml_col = lambda height: pl.BlockSpec((height, rows), lambda i: (0, jnp.minimum(i, n_tiles - 1)))
    return pl.pallas_call(
        functools.partial(_out_ffn_kernel, seq_len // rows, n_tiles),
        grid=(n_tiles + 1,),
        in_specs=[
            ffn_blk(D_MODEL), ffn_blk(DA_WIDTH),
            ml_col(ML_WIDTH), ml_row(ML_WIDTH), ml_col(ML_HEADS * V_EXT_ROWS), ml_col(ML_WIDTH),
            ml_col(2 * ML_HEADS), ml_col(2 * ML_HEADS),
            _const_spec(gain.shape),
            _const_spec(w_out.shape), _const_spec(g_ffn.shape),
            _const_spec(w_gate.shape), _const_spec(w_up.shape), _const_spec(w_down.shape),
            _const_spec(g_final.shape),
        ],
        out_specs=ffn_blk(D_MODEL),
        out_shape=jax.ShapeDtypeStruct((tokens, D_MODEL), F32),
        scratch_shapes=[pltpu.VMEM((rows, ML_WIDTH), BF16),
                        pltpu.VMEM((ML_HEADS, V_EXT_ROWS, ML_DIM), F32),
                        pltpu.VMEM((ML_HEADS, 1, 1), F32)],
        compiler_params=_compiler_params(1),
        name="out_ffn",
    )(x2d, attn, mq_t, mk, mv_t, mo_t, g_t, gcum_t, gain,
      w_out, g_ffn, w_gate, w_up, w_down, g_final)


def kernel(x, positions, mix_norm_g, w_in, da_lambda, da_subln_g, ml_conv_w, ml_conv_b, ml_gate_b,
           ml_norm_g, w_out, ffn_norm_g, w_gate, w_up, w_down, final_norm_g):
    batch, seq_len, _ = x.shape
    tokens = batch * seq_len
    depth = w_in.shape[0]
    assert depth == 1, "one trunk layer"
    assert seq_len % PROJ_ROWS == 0 and seq_len % ATTN_Q == 0 and seq_len % ML_CHUNK == 0
    assert ATTN_Q % ATTN_K == 0 and ATTN_K % CHUNK == 0 and tokens % FFN_ROWS == 0
    assert 2 * DA_QK_DIM + MASK_FEATS <= V7X_MXU_DEPTH and MASK_FEATS <= 2 * DA_QK_DIM
    assert DA_V_DIM == ML_DIM and DA_HEADS == ML_HEADS and FFN_ROWS == 2 * ML_CHUNK
    assert PROJ_ROWS % ML_CHUNK == 0 and w_gate.shape[1:] == (D_MODEL, D_FF)

    x2d = x.reshape(tokens, D_MODEL)
    pos3d = positions.reshape(tokens // PROJ_ROWS, 1, PROJ_ROWS)

    w_in_t = w_in[0].T
    inv_freq = (ROPE_THETA ** (-jnp.arange(0, ROT_DIM, 2, dtype=F32) / ROT_DIM)).reshape(ROT_HALF, 1)
    gb_col = ml_gate_b[0].astype(F32).reshape(2 * ML_HEADS, 1)

    q_t, k, v_t, mq_t, mk, mv_t, mo_t, g_t, gcum_t = _in_proj(
        x2d, pos3d, mix_norm_g[0].reshape(1, D_MODEL).astype(F32), w_in_t, inv_freq,
        ml_conv_w[0].astype(F32), ml_conv_b[0].reshape(1, 2 * ML_WIDTH).astype(F32),
        gb_col, seq_len)

    attn, w_out_bf, w_gate_bf, w_up_bf, w_down_bf = _diff_attn(
        da_lambda[0].astype(F32), q_t, k, v_t, da_subln_g[0].astype(F32).reshape(DA_V_DIM, 1),
        (w_out[0], w_gate[0], w_up[0], w_down[0]), batch, seq_len)
    ml_gain = jnp.broadcast_to(ml_norm_g[0].astype(F32).reshape(ML_WIDTH, 1), (ML_WIDTH, ML_CHUNK))

    out = _out_ffn(x2d, attn, mq_t, mk, mv_t, mo_t, g_t, gcum_t, ml_gain, w_out_bf,
                   ffn_norm_g[0].reshape(1, D_MODEL).astype(F32), w_gate_bf, w_up_bf, w_down_bf,
                   final_norm_g.reshape(1, D_MODEL).astype(F32), seq_len)
    return out.reshape(batch, seq_len, D_MODEL)
```

```python
import functools
import math

import jax
import jax.numpy as jnp
from jax import lax
from jax.experimental import pallas as pl
from jax.experimental.pallas import tpu as pltpu

F32 = jnp.float32
BF16 = jnp.bfloat16

D_MODEL = 1024
CHUNK = 64
NORM_EPS = 1e-6
DA_HEADS = 4
DA_QK_DIM = 64
DA_V_DIM = 128
DA_WIDTH = DA_HEADS * DA_V_DIM
ROPE_THETA = 500000.0
ROT_DIM = DA_QK_DIM // 4
ROT_HALF = ROT_DIM // 2
ML_HEADS = 4
ML_DIM = 128
ML_WIDTH = ML_HEADS * ML_DIM
CONV_WIDTH = 4
D_FF = 2816
LAM_INIT = 0.8 - 0.6 * math.exp(-0.3 * 0)
Q_SCALE = DA_QK_DIM ** -0.5 * math.log2(math.e)

OFF_DA_Q = 0
OFF_DA_K = 512
OFF_DA_V = 1024
OFF_ML = 1536
OFF_GATE = 3584
OFF_ML_V = OFF_ML + 2 * ML_WIDTH
ZT_ML_V = OFF_ML
ZT_ML_O = ZT_ML_V + ML_WIDTH
ZT_GATE = ZT_ML_O + ML_WIDTH

V7X_LANES = 128
V7X_SUBLANES = 8
V7X_BF16_ROWS_PER_VREG = 16
V7X_MXU_DEPTH = 256
V7X_VMEM_LIMIT_BYTES = 56 * 1024 * 1024

PROJ_ROWS = 1024
ATTN_Q = 1024
ATTN_K = 512
MASK_FEATS = ATTN_Q // CHUNK
MASK_BIG = 1e30
V_EXT_ROWS = DA_V_DIM + V7X_BF16_ROWS_PER_VREG
ML_CHUNK = 256
FFN_ROWS = 512
CONV_HALO = V7X_SUBLANES
CONV_COLS = 256
W_CAST_ROWS = 256

_NT = (((1,), (1,)), ((), ()))


def _compiler_params(n_axes):
    return pltpu.CompilerParams(
        dimension_semantics=("arbitrary",) * n_axes,
        vmem_limit_bytes=V7X_VMEM_LIMIT_BYTES,
    )


def _const_spec(shape):
    zeros = (0,) * len(shape)
    return pl.BlockSpec(shape, lambda *_: zeros, pipeline_mode=pl.Buffered(1))


def _rope_rows(zt, cos, sin):
    pieces = []
    for g in range(2):
        base = g * DA_QK_DIM
        x1 = zt[base:base + ROT_HALF]
        x2 = zt[base + ROT_HALF:base + ROT_DIM]
        pieces += [x1 * cos - x2 * sin, x2 * cos + x1 * sin, zt[base + ROT_DIM:base + DA_QK_DIM]]
    return jnp.concatenate(pieces, axis=0)


def _in_proj_kernel(tiles_per_seq,
                    x_ref, pos_ref, g_ref, w_in_t_ref, invf_ref,
                    convw_ref, convb_ref, gb_col_ref,
                    q_t_ref, k_ref, v_t_ref, mq_t_ref, mk_ref, mv_t_ref, mo_t_ref, g_t_ref, gcum_t_ref,
                    halo_ref, w_t_ref, w_row_ref):
    rows = x_ref.shape[0]

    @pl.when(pl.program_id(0) == 0)
    def _():
        for dst, src, height in ((OFF_DA_Q, OFF_DA_Q, OFF_ML), (ZT_ML_V, OFF_ML_V, 2 * ML_WIDTH)):
            for r in range(0, height, W_CAST_ROWS):
                w_t_ref[dst + r:dst + r + W_CAST_ROWS, :] = (
                    w_in_t_ref[src + r:src + r + W_CAST_ROWS, :].astype(BF16))
        gate_w_t = w_in_t_ref[OFF_GATE:OFF_GATE + 2 * ML_HEADS, :]
        w_t_ref[ZT_GATE:ZT_GATE + V7X_BF16_ROWS_PER_VREG, :] = jnp.concatenate(
            [gate_w_t, jnp.zeros((V7X_BF16_ROWS_PER_VREG - 2 * ML_HEADS, D_MODEL), F32)],
            axis=0).astype(BF16)
        for c in range(0, 2 * ML_WIDTH, W_CAST_ROWS):
            w_row_ref[:, c:c + W_CAST_ROWS] = (
                w_in_t_ref[OFF_ML + c:OFF_ML + c + W_CAST_ROWS, :].T.astype(BF16))

    x = x_ref[...]
    ms = jnp.mean(x * x, axis=-1, keepdims=True)
    hb = ((x * lax.rsqrt(ms + NORM_EPS)) * g_ref[...]).astype(BF16)

    first = (pl.program_id(0) % tiles_per_seq) == 0
    halo_ref[0:CONV_HALO, :] = jnp.where(first, 0.0, halo_ref[rows:rows + CONV_HALO, :])
    halo_ref[CONV_HALO:CONV_HALO + rows, :] = jnp.dot(hb, w_row_ref[...], preferred_element_type=F32)

    def proj_t(lo, hi):
        return lax.dot_general(w_t_ref[lo:hi, :], hb, _NT, preferred_element_type=F32)

    zqk_t = proj_t(OFF_DA_Q, OFF_DA_V)

    def conv_group(g):
        cols = slice(g * CONV_COLS, (g + 1) * CONV_COLS)
        xe = halo_ref[:, cols]
        conv = convw_ref[0:1, cols] * xe
        for j in range(1, CONV_WIDTH):
            conv = pltpu.roll(conv, 1, axis=0) + convw_ref[j:j + 1, cols] * xe
        conv = conv[CONV_HALO:, :] + convb_ref[:, cols]
        act = conv * (1.0 / (1.0 + jnp.exp(-conv)))
        if cols.stop <= ML_WIDTH:
            mq_t_ref[cols, :] = (act * (ML_DIM ** -0.5)).T.astype(BF16)
        else:
            mk_ref[:, cols.start - ML_WIDTH:cols.stop - ML_WIDTH] = act.astype(BF16)

    conv_groups = iter(range(2 * ML_WIDTH // CONV_COLS))
    conv_group(next(conv_groups))

    ang = invf_ref[...] * pos_ref[0].astype(F32)
    cos = jnp.cos(ang)
    sin = jnp.sin(ang)
    for h in range(DA_HEADS):
        lo = h * 2 * DA_QK_DIM
        hi = lo + 2 * DA_QK_DIM
        q_rot = _rope_rows(zqk_t[OFF_DA_Q + lo:OFF_DA_Q + hi], cos, sin)
        q_t_ref[lo:hi, :] = (q_rot * Q_SCALE).astype(BF16)
        k_rot = _rope_rows(zqk_t[OFF_DA_K + lo:OFF_DA_K + hi], cos, sin)
        k_ref[:, lo:hi] = k_rot.T.astype(BF16)
    pad_rows = V_EXT_ROWS - DA_V_DIM
    ones_row = (lax.broadcasted_iota(jnp.int32, (pad_rows, rows), 0) == 0).astype(BF16)
    for out_ref, base in ((v_t_ref, OFF_DA_V), (mv_t_ref, ZT_ML_V)):
        zv_t = proj_t(base, base + DA_WIDTH)
        conv_group(next(conv_groups))
        for h in range(DA_HEADS):
            v_lo = h * DA_V_DIM
            out_ref[h * V_EXT_ROWS:h * V_EXT_ROWS + DA_V_DIM, :] = zv_t[v_lo:v_lo + DA_V_DIM].astype(BF16)
            out_ref[h * V_EXT_ROWS + DA_V_DIM:(h + 1) * V_EXT_ROWS, :] = ones_row
    zo_t = proj_t(ZT_ML_O, ZT_GATE + V7X_BF16_ROWS_PER_VREG)
    conv_group(next(conv_groups))
    mo_t_ref[...] = zo_t[0:ML_WIDTH].astype(BF16)

    a_t = zo_t[ML_WIDTH:ML_WIDTH + 2 * ML_HEADS] + gb_col_ref[...]
    ls_t = jnp.minimum(a_t, 0.0) - jnp.log1p(jnp.exp(-jnp.abs(a_t)))
    row_id = lax.broadcasted_iota(jnp.int32, a_t.shape, 0)
    log_gates = jnp.where(row_id < ML_HEADS, a_t, ls_t)
    g_t_ref[...] = log_gates
    src = lax.broadcasted_iota(jnp.int32, (ML_CHUNK, ML_CHUNK), 0)
    dst = lax.broadcasted_iota(jnp.int32, (ML_CHUNK, ML_CHUNK), 1)
    tri = (src <= dst).astype(F32)
    for c in range(0, rows, ML_CHUNK):
        gcum_t_ref[:, c:c + ML_CHUNK] = jnp.dot(log_gates[:, c:c + ML_CHUNK], tri,
                                                preferred_element_type=F32,
                                                precision=lax.Precision.HIGHEST)


def _in_proj(x2d, pos3d, norm_g, w_in_t, inv_freq, conv_w, conv_b, gb_col, seq_len):
    tokens = x2d.shape[0]
    rows = PROJ_ROWS
    n_tiles = tokens // rows
    row_blk = lambda width: pl.BlockSpec((rows, width), lambda i: (i, 0))
    col_blk = lambda height: pl.BlockSpec((height, rows), lambda i: (0, i))
    out_shape = (
        jax.ShapeDtypeStruct((DA_WIDTH, tokens), BF16),
        jax.ShapeDtypeStruct((tokens, DA_WIDTH), BF16),
        jax.ShapeDtypeStruct((DA_HEADS * V_EXT_ROWS, tokens), BF16),
        jax.ShapeDtypeStruct((ML_WIDTH, tokens), BF16),
        jax.ShapeDtypeStruct((tokens, ML_WIDTH), BF16),
        jax.ShapeDtypeStruct((ML_HEADS * V_EXT_ROWS, tokens), BF16),
        jax.ShapeDtypeStruct((ML_WIDTH, tokens), BF16),
        jax.ShapeDtypeStruct((2 * ML_HEADS, tokens), F32),
        jax.ShapeDtypeStruct((2 * ML_HEADS, tokens), F32),
    )
    return pl.pallas_call(
        functools.partial(_in_proj_kernel, seq_len // rows),
        grid=(n_tiles,),
        in_specs=[
            row_blk(D_MODEL),
            pl.BlockSpec((1, 1, rows), lambda i: (i, 0, 0)),
            _const_spec(norm_g.shape),
            _const_spec(w_in_t.shape),
            _const_spec(inv_freq.shape),
            _const_spec(conv_w.shape),
            _const_spec(conv_b.shape),
            _const_spec(gb_col.shape),
        ],
        out_specs=(
            col_blk(DA_WIDTH), row_blk(DA_WIDTH), col_blk(DA_HEADS * V_EXT_ROWS),
            col_blk(ML_WIDTH), row_blk(ML_WIDTH), col_blk(ML_HEADS * V_EXT_ROWS), col_blk(ML_WIDTH),
            col_blk(2 * ML_HEADS), col_blk(2 * ML_HEADS),
        ),
        out_shape=out_shape,
        scratch_shapes=[
            pltpu.VMEM((rows + CONV_HALO, 2 * ML_WIDTH), F32),
            pltpu.VMEM((ZT_GATE + V7X_BF16_ROWS_PER_VREG, D_MODEL), BF16),
            pltpu.VMEM((D_MODEL, 2 * ML_WIDTH), BF16),
        ],
        compiler_params=_compiler_params(1),
        name="in_proj",
    )(x2d, pos3d, norm_g, w_in_t, inv_freq, conv_w, conv_b, gb_col)


def _attn_kernel(lam_ref, q_t_ref, k_ref, v_t_ref, g_ref, w0_ref, w1_ref, w2_ref, w3_ref,
                 o_ref, w0_bf_ref, w1_bf_ref, w2_bf_ref, w3_bf_ref,
                 qm_ref, ind_ref, s_ref, p_ref, acc_ref):
    for w_ref, w_bf_ref in ((w0_ref, w0_bf_ref), (w1_ref, w1_bf_ref),
                            (w2_ref, w2_bf_ref), (w3_ref, w3_bf_ref)):
        w_bf_ref[...] = w_ref[...].astype(BF16)

    seq = k_ref.shape[0]
    tq, tk = ATTN_Q, ATTN_K
    n_q = seq // tq
    diag_tiles = tq // tk

    lv = lam_ref[...]
    lam = (jnp.exp(jnp.sum(lv[0:1] * lv[1:2], axis=1, keepdims=True))
           - jnp.exp(jnp.sum(lv[2:3] * lv[3:4], axis=1, keepdims=True)) + LAM_INIT)

    feat = lax.broadcasted_iota(jnp.int32, (MASK_FEATS, 2 * tq), 0)
    qchunk = (lax.broadcasted_iota(jnp.int32, (MASK_FEATS, 2 * tq), 1) % tq) // CHUNK
    qm_ref[...] = jnp.zeros(qm_ref.shape, BF16)
    qm_ref[2 * DA_QK_DIM:2 * DA_QK_DIM + MASK_FEATS, :] = jnp.where(
        feat > qchunk, -MASK_BIG, 0.0).astype(BF16)
    kchunk = lax.broadcasted_iota(jnp.int32, (tk, 2 * DA_QK_DIM), 0) // CHUNK
    lane = lax.broadcasted_iota(jnp.int32, (tk, 2 * DA_QK_DIM), 1)
    ind_ref[0] = jnp.zeros((tk, 2 * DA_QK_DIM), BF16)
    for d in range(diag_tiles):
        ind_ref[d + 1] = (lane == kchunk + d * (tk // CHUNK)).astype(BF16)

    def q_tile(i, first_tile):
        q_off = pl.multiple_of(i * tq, tq)
        t_diag = diag_tiles * i
        n_t = t_diag + diag_tiles

        def score(par, t, off=0, which=None):
            k_t = k_ref[pl.ds(pl.multiple_of(t * tk, tk), tk), :]
            if which is None:
                which = jnp.maximum(t - t_diag + 1, 0)
            k_ext = jnp.concatenate([k_t, ind_ref[which]], axis=1)
            if off:
                rhs = jnp.concatenate([qm_ref[:, off:tq], qm_ref[:, tq + off:2 * tq]], axis=1)
            else:
                rhs = qm_ref[...]
            w = tq - off
            s = jnp.dot(k_ext, rhs, preferred_element_type=F32)
            s_ref[par, :, 0:2 * w] = s
            return tuple(jnp.max(s[:, mi * w:(mi + 1) * w], axis=0, keepdims=True) for mi in range(2))

        def softmax(par, maxes, tile_max, off=0):
            w = tq - off
            new, alphas = [], []
            for mi in range(2):
                lanes = slice(mi * w, (mi + 1) * w)
                s = s_ref[par, :, lanes]
                m_old = maxes[mi][:, off:tq]
                m_new = jnp.maximum(m_old, tile_max[mi])
                p_ref[par, :, lanes] = jnp.exp2(s - m_new).astype(BF16)
                alphas.append(jnp.exp2(m_old - m_new))
                new.append(jnp.concatenate([maxes[mi][:, 0:off], m_new], axis=1) if off else m_new)
            return tuple(new), tuple(alphas)

        def accumulate(par, t, alphas, off=0):
            w = tq - off
            v_t = v_t_ref[:, pl.ds(pl.multiple_of(t * tk, tk), tk)]
            pv = jnp.dot(v_t, p_ref[par, :, 0:2 * w], preferred_element_type=F32)
            for mi in range(2):
                lanes = slice(mi * tq + off, (mi + 1) * tq)
                acc_ref[:, lanes] = alphas[mi] * acc_ref[:, lanes] + pv[:, mi * w:(mi + 1) * w]

        def step(par, t, carry, first=False):
            maxes, alphas, tile_max = carry
            next_max = score(1 - par, t + 1)
            if not first:
                accumulate(1 - par, t - 1, alphas)
            return softmax(par, maxes, tile_max) + (next_max,)

        q_t = q_t_ref[:, pl.ds(q_off, tq)]
        qm_ref[0:DA_QK_DIM, 0:tq] = q_t[0:DA_QK_DIM]
        qm_ref[DA_QK_DIM:2 * DA_QK_DIM, tq:2 * tq] = q_t[DA_QK_DIM:]
        tile_max = score(0, 0)
        if not first_tile:
            finish(i - 1)
        acc_ref[...] = jnp.zeros(acc_ref.shape, F32)

        neg = jnp.full((1, tq), -jnp.inf, F32)
        maxes, alphas = (neg, neg), None
        if not first_tile:
            carry = step(0, 0, (maxes, alphas, tile_max), first=True)

            def pair(u, carry):
                carry = step(1, 2 * u + 1, carry)
                return step(0, 2 * u + 2, carry)

            carry = lax.fori_loop(0, t_diag // 2 - 1, pair, carry)
            maxes, alphas, tile_max = step(1, t_diag - 1, carry)
        for d in range(diag_tiles):
            t, par = t_diag + d, d % 2
            if d + 1 < diag_tiles:
                next_max = score(1 - par, t + 1, off=(d + 1) * tk, which=d + 2)
            if d or not first_tile:
                accumulate(1 - par, t - 1, alphas, off=max(d - 1, 0) * tk)
            maxes, alphas = softmax(par, maxes, tile_max, off=d * tk)
            tile_max = next_max
        accumulate((diag_tiles - 1) % 2, n_t - 1, alphas, off=(diag_tiles - 1) * tk)
        return 0

    def finish(i):
        acc = acc_ref[...]
        o1 = acc[0:DA_V_DIM, 0:tq] / acc[DA_V_DIM:DA_V_DIM + 1, 0:tq]
        o2 = acc[0:DA_V_DIM, tq:2 * tq] / acc[DA_V_DIM:DA_V_DIM + 1, tq:2 * tq]
        o_t = o1 - lam * o2
        ms = jnp.mean(o_t * o_t, axis=0, keepdims=True)
        y_t = (o_t * lax.rsqrt(ms + NORM_EPS)) * g_ref[...] * (1.0 - LAM_INIT)
        o_ref[pl.ds(pl.multiple_of(i * tq, tq), tq), :] = y_t.T.astype(BF16)

    q_tile(jnp.int32(0), True)
    lax.fori_loop(1, n_q, lambda i, c: q_tile(i, False), 0)
    finish(jnp.int32(n_q - 1))


def _diff_attn(lam_params, q_t, k, v_t, subln_col, later_weights, batch, seq_len):
    tokens = k.shape[0]
    tq, tk = ATTN_Q, ATTN_K
    steps = batch * DA_HEADS
    for w in later_weights:
        assert w.shape[0] % (steps * V7X_BF16_ROWS_PER_VREG) == 0, w.shape
    slab_specs = [pl.BlockSpec((w.shape[0] // steps, w.shape[1]), lambda b, h: (b * DA_HEADS + h, 0))
                  for w in later_weights]
    return pl.pallas_call(
        _attn_kernel,
        grid=(batch, DA_HEADS),
        in_specs=[
            _const_spec(lam_params.shape),
            pl.BlockSpec((DA_V_DIM, seq_len), lambda b, h: (h, b)),
            pl.BlockSpec((seq_len, DA_V_DIM), lambda b, h: (b, h)),
            pl.BlockSpec((V_EXT_ROWS, seq_len), lambda b, h: (h, b)),
            _const_spec(subln_col.shape),
            *slab_specs,
        ],
        out_specs=(pl.BlockSpec((seq_len, DA_V_DIM), lambda b, h: (b, h)), *slab_specs),
        out_shape=(jax.ShapeDtypeStruct((tokens, DA_WIDTH), BF16),
                   *[jax.ShapeDtypeStruct(w.shape, BF16) for w in later_weights]),
        scratch_shapes=[
            pltpu.VMEM((V7X_MXU_DEPTH, 2 * tq), BF16),
            pltpu.VMEM((1 + tq // tk, tk, 2 * DA_QK_DIM), BF16),
            pltpu.VMEM((2, tk, 2 * tq), F32),
            pltpu.VMEM((2, tk, 2 * tq), BF16),
            pltpu.VMEM((V_EXT_ROWS, 2 * tq), F32),
        ],
        compiler_params=_compiler_params(2),
        name="diff_attn",
    )(lam_params, q_t, k, v_t, subln_col, *later_weights)


def _mlstm_begin(q_t_ref, k_ref, v_t_ref, g_t_ref, gcum_t_ref, c_ref, m_ref, cols, reset):
    chunk = cols.stop - cols.start
    heads = range(ML_HEADS)
    g_t = g_t_ref[:, cols]
    cum_row = gcum_t_ref[:, cols]
    key_rows = g_t - pltpu.roll(cum_row, ML_HEADS, axis=0)
    key_cols = jnp.concatenate(
        [key_rows, jnp.zeros((V7X_LANES - 2 * ML_HEADS, chunk), F32)], axis=0).T
    st = dict(
        chunk=chunk,
        q_t=[q_t_ref[h * ML_DIM:(h + 1) * ML_DIM, cols] for h in heads],
        k=[k_ref[cols, h * ML_DIM:(h + 1) * ML_DIM] for h in heads],
        v_t=[v_t_ref[h * V_EXT_ROWS:(h + 1) * V_EXT_ROWS, cols] for h in heads],
        c_t=[c_ref[h] for h in heads],
        m_prev=[m_ref[h] for h in heads],
        b_row=[cum_row[ML_HEADS + h:ML_HEADS + h + 1, :] for h in heads],
        i_row=[g_t[h:h + 1, :] for h in heads],
        e_mat=[jnp.broadcast_to(key_cols[:, h:h + 1], (chunk, chunk)) for h in heads],
    )
    if reset is not None:
        st["c_t"] = [jnp.where(reset, 0.0, c) for c in st["c_t"]]
        st["m_prev"] = [jnp.where(reset, 0.0, m) for m in st["m_prev"]]
    st["kq"] = [jnp.dot(st["k"][h], st["q_t"][h], preferred_element_type=F32) for h in heads]
    st["inter_mm"] = [jnp.dot(st["c_t"][h].astype(BF16), st["q_t"][h], preferred_element_type=F32)
                      for h in heads]
    return st


def _mlstm_weights(st):
    chunk = st["chunk"]
    s_id = lax.broadcasted_iota(jnp.int32, (chunk, chunk), 0)
    t_id = lax.broadcasted_iota(jnp.int32, (chunk, chunk), 1)
    causal = s_id <= t_id
    st["m_row"], st["w_inter"], st["sc"] = [], [], []
    for h in range(ML_HEADS):
        d_mat = jnp.where(causal, st["e_mat"][h] + st["b_row"][h], -jnp.inf)
        inter = st["b_row"][h] + st["m_prev"][h]
        m_row = jnp.maximum(inter, jnp.max(d_mat, axis=0, keepdims=True))
        st["m_row"].append(m_row)
        st["w_inter"].append(jnp.exp(inter - m_row))
        st["sc"].append((st["kq"][h] * jnp.exp(d_mat - m_row)).astype(BF16))


def _mlstm_matmuls(st, c_ref, m_ref):
    chunk = st["chunk"]
    heads = range(ML_HEADS)
    st["intra_mm"] = [jnp.dot(st["v_t"][h], st["sc"][h], preferred_element_type=F32) for h in heads]
    for h in heads:
        b_row, m_prev = st["b_row"][h], st["m_prev"][h]
        b_last = b_row[:, chunk - 1:chunk]
        g_row = (b_last - b_row) + st["i_row"][h]
        m_new = jnp.maximum(b_last + m_prev, jnp.max(g_row, axis=1, keepdims=True))
        decay = jnp.exp(b_last + m_prev - m_new)
        wk = jnp.exp(st["e_mat"][h][:, 0:ML_DIM] + (b_last - m_new))
        kw = (st["k"][h].astype(F32) * wk).astype(BF16)
        c_ref[h] = decay * st["c_t"][h] + jnp.dot(st["v_t"][h], kw, preferred_element_type=F32)
        m_ref[h] = m_new


def _mlstm_output(st, o_t_ref, ng_ref, cols):
    outs = []
    for h in range(ML_HEADS):
        num = st["intra_mm"][h] + st["w_inter"][h] * st["inter_mm"][h]
        nq = num[ML_DIM:ML_DIM + 1, :]
        hid = num[0:ML_DIM, :] / jnp.maximum(jnp.abs(nq), jnp.exp(-st["m_row"][h]))
        ms = jnp.mean(hid * hid, axis=0, keepdims=True)
        hn = (hid * lax.rsqrt(ms + NORM_EPS)) * ng_ref[h * ML_DIM:(h + 1) * ML_DIM, :]
        og = o_t_ref[h * ML_DIM:(h + 1) * ML_DIM, cols].astype(F32)
        outs.append((hn * (1.0 / (1.0 + jnp.exp(-og)))).T.astype(BF16))
    return jnp.concatenate(outs, axis=1)


def _rms(x, g):
    ms = jnp.mean(x * x, axis=-1, keepdims=True)
    return (x * lax.rsqrt(ms + NORM_EPS)) * g


def _out_ffn_kernel(tiles_per_seq, n_tiles,
                    x_ref, attn_ref, mq_t_ref, mk_ref, mv_t_ref, mo_t_ref, g_t_ref, gcum_t_ref, ng_ref,
                    w_out_ref, g_ffn_ref, w_gate_ref, w_up_ref, w_down_ref, g_final_ref,
                    out_ref, ml_ref, c_ref, m_ref):
    i = pl.program_id(0)
    reset = (jnp.minimum(i, n_tiles - 1) % tiles_per_seq) == 0
    chunks = [slice(c * ML_CHUNK, (c + 1) * ML_CHUNK) for c in range(FFN_ROWS // ML_CHUNK)]
    ml_args = (mq_t_ref, mk_ref, mv_t_ref, g_t_ref, gcum_t_ref, c_ref, m_ref)

    @pl.when(i == 0)
    def _():
        c_ref[...] = jnp.zeros(c_ref.shape, F32)
        m_ref[...] = jnp.zeros(m_ref.shape, F32)
        for n, cols in enumerate(chunks):
            st = _mlstm_begin(*ml_args, cols, reset if n == 0 else None)
            _mlstm_weights(st)
            _mlstm_matmuls(st, c_ref, m_ref)
            ml_ref[cols, :] = _mlstm_output(st, mo_t_ref, ng_ref, cols)

    @pl.when(i > 0)
    def _():
        halves = chunks
        mix = jnp.concatenate([attn_ref[...], ml_ref[...]], axis=1)
        first = _mlstm_begin(*ml_args, chunks[0], reset)
        y = [x_ref[hs, :] + jnp.dot(mix[hs], w_out_ref[...], preferred_element_type=F32)
             for hs in halves]
        h2 = [_rms(yh, g_ffn_ref[...]).astype(BF16) for yh in y]
        _mlstm_weights(first)
        _mlstm_matmuls(first, c_ref, m_ref)
        gate = [jnp.dot(h, w_gate_ref[...], preferred_element_type=F32) for h in h2]
        ml_ref[chunks[0], :] = _mlstm_output(first, mo_t_ref, ng_ref, chunks[0])
        second = _mlstm_begin(*ml_args, chunks[1], None)
        up = [jnp.dot(h, w_up_ref[...], preferred_element_type=F32) for h in h2]
        _mlstm_weights(second)
        _mlstm_matmuls(second, c_ref, m_ref)
        act = [((g * (1.0 / (1.0 + jnp.exp(-g)))) * u).astype(BF16) for g, u in zip(gate, up)]
        y2 = [yh + jnp.dot(a, w_down_ref[...], preferred_element_type=F32) for yh, a in zip(y, act)]
        ml_ref[chunks[1], :] = _mlstm_output(second, mo_t_ref, ng_ref, chunks[1])
        for hs, yh in zip(halves, y2):
            out_ref[hs, :] = _rms(yh, g_final_ref[...])


def _out_ffn(x2d, attn, mq_t, mk, mv_t, mo_t, g_t, gcum_t, gain,
             w_out, g_ffn, w_gate, w_up, w_down, g_final, seq_len):
    tokens = x2d.shape[0]
    rows = FFN_ROWS
    n_tiles = tokens // rows
    ffn_blk = lambda width: pl.BlockSpec((rows, width), lambda i: (jnp.maximum(i - 1, 0), 0))
    ml_row = lambda width: pl.BlockSpec((rows, width), lambda i: (jnp.minimum(i, n_tiles - 1), 0))
    ml_col = lambda height: pl.BlockSpec((height, rows), lambda i: (0, jnp.minimum(i, n_tiles - 1)))
    return pl.pallas_call(
        functools.partial(_out_ffn_kernel, seq_len // rows, n_tiles),
        grid=(n_tiles + 1,),
        in_specs=[
            ffn_blk(D_MODEL), ffn_blk(DA_WIDTH),
            ml_col(ML_WIDTH), ml_row(ML_WIDTH), ml_col(ML_HEADS * V_EXT_ROWS), ml_col(ML_WIDTH),
            ml_col(2 * ML_HEADS), ml_col(2 * ML_HEADS),
            _const_spec(gain.shape),
            _const_spec(w_out.shape), _const_spec(g_ffn.shape),
            _const_spec(w_gate.shape), _const_spec(w_up.shape), _const_spec(w_down.shape),
            _const_spec(g_final.shape),
        ],
        out_specs=ffn_blk(D_MODEL),
        out_shape=jax.ShapeDtypeStruct((tokens, D_MODEL), F32),
        scratch_shapes=[pltpu.VMEM((rows, ML_WIDTH), BF16),
                        pltpu.VMEM((ML_HEADS, V_EXT_ROWS, ML_DIM), F32),
                        pltpu.VMEM((ML_HEADS, 1, 1), F32)],
        compiler_params=_compiler_params(1),
        name="out_ffn",
    )(x2d, attn, mq_t, mk, mv_t, mo_t, g_t, gcum_t, gain,
      w_out, g_ffn, w_gate, w_up, w_down, g_final)


def kernel(x, positions, mix_norm_g, w_in, da_lambda, da_subln_g, ml_conv_w, ml_conv_b, ml_gate_b,
           ml_norm_g, w_out, ffn_norm_g, w_gate, w_up, w_down, final_norm_g):
    batch, seq_len, _ = x.shape
    tokens = batch * seq_len
    depth = w_in.shape[0]
    assert depth == 1, "one trunk layer"
    assert seq_len % PROJ_ROWS == 0 and seq_len % ATTN_Q == 0 and seq_len % ML_CHUNK == 0
    assert ATTN_Q % ATTN_K == 0 and ATTN_K % CHUNK == 0 and tokens % FFN_ROWS == 0
    assert 2 * DA_QK_DIM + MASK_FEATS <= V7X_MXU_DEPTH and MASK_FEATS <= 2 * DA_QK_DIM
    assert DA_V_DIM == ML_DIM and DA_HEADS == ML_HEADS and FFN_ROWS == 2 * ML_CHUNK
    assert PROJ_ROWS % ML_CHUNK == 0 and w_gate.shape[1:] == (D_MODEL, D_FF)

    x2d = x.reshape(tokens, D_MODEL)
    pos3d = positions.reshape(tokens // PROJ_ROWS, 1, PROJ_ROWS)

    w_in_t = w_in[0].T
    inv_freq = (ROPE_THETA ** (-jnp.arange(0, ROT_DIM, 2, dtype=F32) / ROT_DIM)).reshape(ROT_HALF, 1)
    gb_col = ml_gate_b[0].astype(F32).reshape(2 * ML_HEADS, 1)

    q_t, k, v_t, mq_t, mk, mv_t, mo_t, g_t, gcum_t = _in_proj(
        x2d, pos3d, mix_norm_g[0].reshape(1, D_MODEL).astype(F32), w_in_t, inv_freq,
        ml_conv_w[0].astype(F32), ml_conv_b[0].reshape(1, 2 * ML_WIDTH).astype(F32),
        gb_col, seq_len)

    attn, w_out_bf, w_gate_bf, w_up_bf, w_down_bf = _diff_attn(
        da_lambda[0].astype(F32), q_t, k, v_t, da_subln_g[0].astype(F32).reshape(DA_V_DIM, 1),
        (w_out[0], w_gate[0], w_up[0], w_down[0]), batch, seq_len)
    ml_gain = jnp.broadcast_to(ml_norm_g[0].astype(F32).reshape(ML_WIDTH, 1), (ML_WIDTH, ML_CHUNK))

    out = _out_ffn(x2d, attn, mq_t, mk, mv_t, mo_t, g_t, gcum_t, ml_gain, w_out_bf,
                   ffn_norm_g[0].reshape(1, D_MODEL).astype(F32), w_gate_bf, w_up_bf, w_down_bf,
                   final_norm_g.reshape(1, D_MODEL).astype(F32), seq_len)
    return out.reshape(batch, seq_len, D_MODEL)
```
